```python
import math
import jax, jax.numpy as jnp
from jax import lax
import numpy as np

D_MODEL = 1024
BATCH = 32
SEQ = 256
DEPTH = 2
DEC_BATCH = 8
DEC_SEQ = 2048
PAST_LEN = 256

GRID_W = 64
HEAD_DIM = 64
N_HEADS_A = 8
N_KV_A = 2
N_HEADS_B = 8
N_KV_CACHE = N_KV_A + N_HEADS_B
WIN_R = 8
WIN_C = 16
Q_BLOCK = 128
ROPE_THETA = 10000.0
N_HEADS_C = 8
DK_C = 128
DV_C = 128
CONV_K = 3
CHUNK = 64
D_FF = 4 * D_MODEL
N_MOD = 6
EPS = 1e-6
ATTN_IN = (N_HEADS_A + 2 * N_KV_A + 3 * N_HEADS_B) * HEAD_DIM
ATTN_OUT = (N_HEADS_A + N_HEADS_B) * HEAD_DIM
DELTA_QKV = N_HEADS_C * (2 * DK_C + DV_C)
DELTA_OUT = N_HEADS_C * DV_C
DELTA_IN = DELTA_QKV + DELTA_OUT + 4 * N_HEADS_C

kernel_name = 'hybrid_diffusion_prefix_step'


def split_last(x, sizes):
    offs, acc = [], 0
    for s in sizes[:-1]:
        acc += s
        offs.append(acc)
    return jnp.split(x, offs, axis=-1)


def rms_norm(x, g):
    xf = x.astype(jnp.float32)
    y = xf * lax.rsqrt(jnp.mean(xf * xf, axis=-1, keepdims=True) + EPS)
    return (y * g.astype(jnp.float32)).astype(x.dtype)


def l2_normalize(x):
    return x * lax.rsqrt(jnp.sum(x * x, axis=-1, keepdims=True) + EPS)


def modulation(cvec, w, b):
    m = jax.nn.silu(cvec) @ w + b
    return jnp.split(m[:, None, :], N_MOD, axis=-1)


def squared_relu_mlp(h, w1, w2):
    return jnp.square(jax.nn.relu(h @ w1)) @ w2


def axial_rope_tables(T, dtype):
    t = jnp.arange(T, dtype=jnp.int32)
    pos = jnp.stack([t // GRID_W, t % GRID_W], axis=-1).astype(jnp.float32)
    axis_dim = HEAD_DIM // 2
    inv_freq = 1.0 / (ROPE_THETA ** (jnp.arange(0, axis_dim, 2, dtype=jnp.float32) / axis_dim))
    ang = pos[:, :, None] * inv_freq
    return jnp.cos(ang).astype(dtype), jnp.sin(ang).astype(dtype)


def apply_axial_rope(x, cos, sin):
    B, T, H, _ = x.shape
    xs = x.reshape(B, T, H, 2, 2, HEAD_DIM // 4)
    x1, x2 = xs[..., 0, :], xs[..., 1, :]
    c = cos[None, :, None]
    s = sin[None, :, None]
    out = jnp.stack([x1 * c - x2 * s, x2 * c + x1 * s], axis=-2)
    return out.reshape(B, T, H, HEAD_DIM)


def blocked_attention(q, k, v):
    B, T = q.shape[:2]
    nb = T // Q_BLOCK
    qb = jnp.swapaxes(q.reshape(B, nb, Q_BLOCK, *q.shape[2:]), 0, 1)
    scale = HEAD_DIM ** -0.5

    def one_block(qi):
        s = jnp.einsum('bqkgd,bskd->bkgqs', qi, k).astype(jnp.float32) * scale
        p = jax.nn.softmax(s, axis=-1).astype(v.dtype)
        return jnp.einsum('bkgqs,bskd->bqkgd', p, v)

    o = lax.map(one_block, qb)
    return jnp.swapaxes(o, 0, 1).reshape(q.shape)


def neighbourhood_attention(q, k, v, k_ctx, v_ctx, rel_bias):
    B, T, H, hd = q.shape
    rows = T // GRID_W
    wr = min(WIN_R, rows)
    qg = q.reshape(B, rows, GRID_W, H, hd)
    kg = k.reshape(B, rows, GRID_W, H, hd)
    vg = v.reshape(B, rows, GRID_W, H, hd)
    cols = jnp.arange(GRID_W, dtype=jnp.int32)
    col_start = jnp.clip(cols - WIN_C // 2, 0, GRID_W - WIN_C)
    col_idx = col_start[:, None] + jnp.arange(WIN_C, dtype=jnp.int32)
    col_off = col_idx - cols[:, None] + (WIN_C - 1)
    n_loc = wr * WIN_C
    scale = hd ** -0.5

    def row_block(r):
        rs = jnp.clip(r - wr // 2, 0, rows - wr)
        kb = lax.dynamic_slice_in_dim(kg, rs, wr, axis=1)
        vb = lax.dynamic_slice_in_dim(vg, rs, wr, axis=1)
        kn = kb[:, :, col_idx]
        vn = vb[:, :, col_idx]
        qr = lax.dynamic_index_in_dim(qg, r, axis=1, keepdims=False)
        row_off = rs + jnp.arange(wr, dtype=jnp.int32) - r + (WIN_R - 1)
        bias = rel_bias[:, row_off][:, :, col_off].astype(jnp.float32)
        s_loc = jnp.einsum('bqhd,brqwhd->bhqrw', qr, kn).astype(jnp.float32) * scale
        s_loc = s_loc + jnp.transpose(bias, (0, 2, 1, 3))[None]
        s_ctx = jnp.einsum('bqhd,bphd->bhqp', qr, k_ctx).astype(jnp.float32) * scale
        s = jnp.concatenate([s_loc.reshape(B, H, GRID_W, n_loc), s_ctx], axis=-1)
        p = jax.nn.softmax(s, axis=-1).astype(v.dtype)
        p_loc = p[..., :n_loc].reshape(B, H, GRID_W, wr, WIN_C)
        p_ctx = p[..., n_loc:]
        return (jnp.einsum('bhqrw,brqwhd->bqhd', p_loc, vn)
                + jnp.einsum('bhqp,bphd->bqhd', p_ctx, v_ctx))

    out = lax.map(row_block, jnp.arange(rows, dtype=jnp.int32))
    return jnp.transpose(out, (1, 0, 2, 3, 4)).reshape(B, T, H, hd)


def attn_project(h, w_in, qn_a, kn_a, qn_b, kn_b):
    B, T, _ = h.shape
    sizes = tuple(n * HEAD_DIM for n in (N_HEADS_A, N_KV_A, N_KV_A, N_HEADS_B, N_HEADS_B, N_HEADS_B))
    parts = [p.reshape(B, T, -1, HEAD_DIM) for p in split_last(h @ w_in, sizes)]
    q_a, k_a, v_a, q_b, k_b, v_b = parts
    return (rms_norm(q_a, qn_a), rms_norm(k_a, kn_a), v_a,
            rms_norm(q_b, qn_b), rms_norm(k_b, kn_b), v_b)


def attn_mixer_context(h, w_in, qn_a, kn_a, qn_b, kn_b, w_out):
    B, T, _ = h.shape
    q_a, k_a, v_a, q_b, k_b, v_b = attn_project(h, w_in, qn_a, kn_a, qn_b, kn_b)
    o_a = blocked_attention(q_a.reshape(B, T, N_KV_A, N_HEADS_A // N_KV_A, HEAD_DIM), k_a, v_a)
    o_b = blocked_attention(q_b[:, :, :, None, :], k_b, v_b)
    o = jnp.concatenate([o_a.reshape(B, T, -1), o_b.reshape(B, T, -1)], axis=-1)
    return (o @ w_out, jnp.concatenate([k_a, k_b], axis=2), jnp.concatenate([v_a, v_b], axis=2))


def attn_mixer_latent(h, ctx_k, ctx_v, w_in, qn_a, kn_a, qn_b, kn_b, rel_bias, w_out):
    B, T, _ = h.shape
    q_a, k_a, v_a, q_b, k_b, v_b = attn_project(h, w_in, qn_a, kn_a, qn_b, kn_b)
    cos, sin = axial_rope_tables(T, h.dtype)
    q_a = apply_axial_rope(q_a, cos, sin)
    k_a = apply_axial_rope(k_a, cos, sin)
    k_all = jnp.concatenate([k_a, ctx_k[:, :, :N_KV_A]], axis=1)
    v_all = jnp.concatenate([v_a, ctx_v[:, :, :N_KV_A]], axis=1)
    o_a = blocked_attention(q_a.reshape(B, T, N_KV_A, N_HEADS_A // N_KV_A, HEAD_DIM), k_all, v_all)
    o_b = neighbourhood_attention(q_b, k_b, v_b, ctx_k[:, :, N_KV_A:], ctx_v[:, :, N_KV_A:], rel_bias)
    o = jnp.concatenate([o_a.reshape(B, T, -1), o_b.reshape(B, T, -1)], axis=-1)
    return o @ w_out


def centred_depthwise_conv(x, w):
    K = w.shape[0]
    return lax.conv_general_dilated(x, w.astype(x.dtype)[:, None, :], window_strides=(1,),
                                    padding=[((K - 1) // 2, K // 2)],
                                    dimension_numbers=('NWC', 'WIO', 'NWC'),
                                    feature_group_count=x.shape[-1])


def gated_delta_chunked(q, k, v, log_a, beta, s0):
    B, T, H, _ = q.shape
    DV = v.shape[-1]
    N = T // CHUNK

    def to_chunks(x):
        x = x.reshape(B, N, CHUNK, H, *x.shape[3:])
        return jnp.moveaxis(jnp.moveaxis(x, 1, 0), 3, 2)

    qc, kc, vc = to_chunks(q), to_chunks(k), to_chunks(v)
    bc = to_chunks(beta)
    g = jnp.cumsum(to_chunks(log_a), axis=-1)
    incl = jnp.tril(jnp.ones((CHUNK, CHUNK), dtype=bool))
    strict = jnp.tril(jnp.ones((CHUNK, CHUNK), dtype=bool), -1)
    diff = g[..., :, None] - g[..., None, :]
    decay = jnp.where(incl, jnp.exp(jnp.where(incl, diff, 0.0)), 0.0)
    kbeta = kc * bc[..., None]
    a_mat = jnp.where(strict, jnp.einsum('nbhid,nbhjd->nbhij', kbeta, kc) * decay, 0.0)
    m = a_mat + jnp.eye(CHUNK, dtype=a_mat.dtype)
    u = lax.linalg.triangular_solve(m, vc * bc[..., None], left_side=True, lower=True, unit_diagonal=True)
    w = lax.linalg.triangular_solve(m, kbeta * jnp.exp(g)[..., None], left_side=True, lower=True,
                                    unit_diagonal=True)

    def step(s, inp):
        qi, ki, ui, wi, gi, di = inp
        v_new = ui - jnp.einsum('bhck,bhkv->bhcv', wi, s)
        att = jnp.einsum('bhik,bhjk->bhij', qi, ki) * di
        o = (jnp.einsum('bhck,bhkv->bhcv', qi * jnp.exp(gi)[..., None], s)
             + jnp.einsum('bhij,bhjv->bhiv', att, v_new))
        g_last = gi[..., -1:]
        s = (s * jnp.exp(g_last)[..., None]
             + jnp.einsum('bhck,bhcv->bhkv', ki * jnp.exp(g_last - gi)[..., None], v_new))
        return s, o

    s_fin, o = lax.scan(step, s0, (qc, kc, u, w, g, decay))
    o = jnp.moveaxis(jnp.moveaxis(o, 2, 3), 0, 1).reshape(B, T, H, DV)
    return o, s_fin


def delta_project(h, w_in, conv_w, a_log, dt_bias):
    B, T, _ = h.shape
    qkv, z, ab = split_last(h @ w_in, (DELTA_QKV, DELTA_OUT, 4 * N_HEADS_C))
    qkv = jax.nn.silu(centred_depthwise_conv(qkv, conv_w))
    q, k, v = split_last(qkv, (N_HEADS_C * DK_C, N_HEADS_C * DK_C, N_HEADS_C * DV_C))
    q = l2_normalize(q.reshape(B, T, N_HEADS_C, DK_C).astype(jnp.float32)) * (DK_C ** -0.5)
    k = l2_normalize(k.reshape(B, T, N_HEADS_C, DK_C).astype(jnp.float32))
    v = v.reshape(B, T, N_HEADS_C, DV_C).astype(jnp.float32)
    ab = ab.astype(jnp.float32).reshape(B, T, 2, 2, N_HEADS_C)
    log_a = -jnp.exp(a_log.astype(jnp.float32)) * jax.nn.softplus(ab[:, :, 0] + dt_bias.astype(jnp.float32))
    beta = jax.nn.sigmoid(ab[:, :, 1])
    return q, k, v, z, log_a, beta


def bidirectional_delta(q, k, v, log_a, beta, s0_f, s0_b):
    flip = lambda t: jnp.flip(t, axis=1)
    o_f, s_f = gated_delta_chunked(q, k, v, log_a[:, :, 0], beta[:, :, 0], s0_f)
    o_b, s_b = gated_delta_chunked(flip(q), flip(k), flip(v), flip(log_a[:, :, 1]), flip(beta[:, :, 1]), s0_b)
    return o_f + flip(o_b), s_f, s_b


def delta_output(o, z, out_norm, w_out):
    B, T = o.shape[:2]
    zh = z.reshape(B, T, N_HEADS_C, DV_C)
    y = rms_norm(o.astype(z.dtype), out_norm) * jax.nn.silu(zh)
    return y.reshape(B, T, DELTA_OUT) @ w_out


def delta_mixer_context(h, w_in, conv_w, a_log, dt_bias, out_norm, w_out):
    q, k, v, z, log_a, beta = delta_project(h, w_in, conv_w, a_log, dt_bias)
    s0 = jnp.zeros((h.shape[0], N_HEADS_C, DK_C, DV_C), jnp.float32)
    o, s_f, s_b = bidirectional_delta(q, k, v, log_a, beta, s0, s0)
    return delta_output(o, z, out_norm, w_out), jnp.stack([s_f, s_b], axis=1)


def delta_mixer_latent(h, state, w_in, conv_w, a_log, dt_bias, out_norm, w_out):
    q, k, v, z, log_a, beta = delta_project(h, w_in, conv_w, a_log, dt_bias)
    st = state.astype(jnp.float32)
    o, _, _ = bidirectional_delta(q, k, v, log_a, beta, st[:, 0], st[:, 1])
    return delta_output(o, z, out_norm, w_out)


def setup_inputs(seed: int = 0) -> dict:
    key = jax.random.key(seed)
    ks = jax.random.split(key, 32)
    nrm = lambda i, shape, scale: jax.random.normal(ks[i], shape, jnp.float32) * scale
    gain = lambda i, n: 1.0 + nrm(i, (n,), 0.05)
    dt = jnp.exp(jax.random.uniform(ks[26], (2, N_HEADS_C), jnp.float32, math.log(1e-3), math.log(1e-1)))
    return {
        'x_prompt': nrm(0, (BATCH, SEQ, D_MODEL), 1.0),
        'x_sample': nrm(1, (DEC_BATCH, DEC_SEQ, D_MODEL), 1.0),
        'c': nrm(2, (DEC_BATCH, D_MODEL), 1.0),
        'cache_l0_k': nrm(3, (DEC_BATCH, PAST_LEN, N_KV_CACHE, HEAD_DIM), 1.0),
        'cache_l0_v': nrm(4, (DEC_BATCH, PAST_LEN, N_KV_CACHE, HEAD_DIM), 1.0),
        'state_l1': nrm(5, (DEC_BATCH, 2, N_HEADS_C, DK_C, DV_C), 0.1),
        'c_ctx': nrm(6, (D_MODEL,), 1.0),
        'l0_mod_w': nrm(7, (D_MODEL, N_MOD * D_MODEL), 0.5 * D_MODEL ** -0.5),
        'l0_mod_b': nrm(8, (N_MOD * D_MODEL,), 0.02),
        'l0_norm1': gain(9, D_MODEL),
        'l0_w_in': nrm(10, (D_MODEL, ATTN_IN), D_MODEL ** -0.5),
        'l0_q_norm_a': gain(11, HEAD_DIM),
        'l0_k_norm_a': gain(12, HEAD_DIM),
        'l0_q_norm_b': gain(13, HEAD_DIM),
        'l0_k_norm_b': gain(14, HEAD_DIM),
        'l0_rel_bias': nrm(15, (N_HEADS_B, 2 * WIN_R - 1, 2 * WIN_C - 1), 0.1),
        'l0_w_out': nrm(16, (ATTN_OUT, D_MODEL), ATTN_OUT ** -0.5),
        'l0_norm2': gain(17, D_MODEL),
        'l0_mlp_w1': nrm(18, (D_MODEL, D_FF), D_MODEL ** -0.5),
        'l0_mlp_w2': nrm(19, (D_FF, D_MODEL), D_FF ** -0.5),
        'l1_mod_w': nrm(20, (D_MODEL, N_MOD * D_MODEL), 0.5 * D_MODEL ** -0.5),
        'l1_mod_b': nrm(21, (N_MOD * D_MODEL,), 0.02),
        'l1_norm1': gain(22, D_MODEL),
        'l1_w_in': nrm(23, (D_MODEL, DELTA_IN), D_MODEL ** -0.5),
        'l1_conv_w': nrm(24, (CONV_K, DELTA_QKV), CONV_K ** -0.5),
        'l1_a_log': jnp.log(jax.random.uniform(ks[25], (2, N_HEADS_C), jnp.float32, 1.0, 16.0)),
        'l1_dt_bias': dt + jnp.log(-jnp.expm1(-dt)),
        'l1_out_norm': gain(27, DV_C),
        'l1_w_out': nrm(28, (DELTA_OUT, D_MODEL), DELTA_OUT ** -0.5),
        'l1_norm2': gain(29, D_MODEL),
        'l1_mlp_w1': nrm(30, (D_MODEL, D_FF), D_MODEL ** -0.5),
        'l1_mlp_w2': nrm(31, (D_FF, D_MODEL), D_FF ** -0.5),
    }


def reference(x_prompt, x_sample, c, cache_l0_k, cache_l0_v, state_l1, c_ctx,
              l0_mod_w, l0_mod_b, l0_norm1, l0_w_in, l0_q_norm_a, l0_k_norm_a, l0_q_norm_b, l0_k_norm_b,
              l0_rel_bias, l0_w_out, l0_norm2, l0_mlp_w1, l0_mlp_w2,
              l1_mod_w, l1_mod_b, l1_norm1, l1_w_in, l1_conv_w, l1_a_log, l1_dt_bias, l1_out_norm,
              l1_w_out, l1_norm2, l1_mlp_w1, l1_mlp_w2):
    common = ((l0_mod_w, l0_mod_b, l0_norm1, l0_norm2, l0_mlp_w1, l0_mlp_w2),
              (l1_mod_w, l1_mod_b, l1_norm1, l1_norm2, l1_mlp_w1, l1_mlp_w2))
    xp, xs = x_prompt, x_sample
    for layer in range(DEPTH):
        mod_w, mod_b, norm1, norm2, mlp_w1, mlp_w2 = common[layer]
        sh1_p, sc1_p, g1_p, sh2_p, sc2_p, g2_p = modulation(c_ctx[None, :], mod_w, mod_b)
        sh1_s, sc1_s, g1_s, sh2_s, sc2_s, g2_s = modulation(c, mod_w, mod_b)
        hp = rms_norm(xp, norm1) * (1 + sc1_p) + sh1_p
        hs = rms_norm(xs, norm1) * (1 + sc1_s) + sh1_s
        if layer % 2 == 0:
            mp, new_k, new_v = attn_mixer_context(hp, l0_w_in, l0_q_norm_a, l0_k_norm_a,
                                                  l0_q_norm_b, l0_k_norm_b, l0_w_out)
            ms = attn_mixer_latent(hs, cache_l0_k, cache_l0_v, l0_w_in, l0_q_norm_a, l0_k_norm_a,
                                   l0_q_norm_b, l0_k_norm_b, l0_rel_bias, l0_w_out)
        else:
            mp, new_s = delta_mixer_context(hp, l1_w_in, l1_conv_w, l1_a_log, l1_dt_bias, l1_out_norm, l1_w_out)
            ms = delta_mixer_latent(hs, state_l1, l1_w_in, l1_conv_w, l1_a_log, l1_dt_bias,
                                    l1_out_norm, l1_w_out)
        xp = xp + g1_p * mp
        xs = xs + g1_s * ms
        xp = xp + g2_p * squared_relu_mlp(rms_norm(xp, norm2) * (1 + sc2_p) + sh2_p, mlp_w1, mlp_w2)
        xs = xs + g2_s * squared_relu_mlp(rms_norm(xs, norm2) * (1 + sc2_s) + sh2_s, mlp_w1, mlp_w2)
    return (xp, xs, new_k, new_v, new_s.astype(x_prompt.dtype))
```

```python
import functools
import math

import jax
import jax.numpy as jnp
from jax import lax
from jax.experimental import pallas as pl
from jax.experimental.pallas import tpu as pltpu

F32 = jnp.float32
BF16 = jnp.bfloat16

EPS = 1e-6
N_MOD = 6
HEAD_DIM = 64
N_HEADS_A = 8
N_KV_A = 2
N_HEADS_B = 8
GRID_W = 64
WIN_R = 8
WIN_C = 16
ROPE_THETA = 10000.0
N_HEADS_C = 8
DK_C = 128
DV_C = 128
LANES = 128
DELTA_CHUNK = 128
NEG_BIG = -1e30

VMEM_LIMIT = 48 * 1024 * 1024


def _cparams(sem):
    return pltpu.CompilerParams(dimension_semantics=sem, vmem_limit_bytes=VMEM_LIMIT)


def _dot(a, b):
    return jnp.dot(a, b, preferred_element_type=F32)


def _dot_nt(a, b):
    return lax.dot_general(a, b, (((1,), (1,)), ((), ())), preferred_element_type=F32)


def _dot_tn(a, b):
    return lax.dot_general(a, b, (((0,), (0,)), ((), ())), preferred_element_type=F32)


def _split2(x):
    hi = x.astype(BF16)
    lo = (x - hi.astype(F32)).astype(BF16)
    return hi, lo


def _split3(x):
    p1 = x.astype(BF16)
    r1 = x - p1.astype(F32)
    p2 = r1.astype(BF16)
    p3 = (r1 - p2.astype(F32)).astype(BF16)
    return p1, p2, p3


def _dot3(a, b):
    ah, al = _split2(a)
    bh, bl = _split2(b)
    lhs = jnp.concatenate([ah, ah, al], axis=1)
    rhs = jnp.concatenate([bh, bl, bh], axis=0)
    return _dot(lhs, rhs)


def _dot_exact_rhs01(a_f32, b01):
    p1, p2, p3 = _split3(a_f32)
    lhs = jnp.concatenate([p1, p2, p3], axis=1)
    rhs = jnp.concatenate([b01, b01, b01], axis=0)
    return _dot(lhs, rhs)


def _dot_exact_lhs01(a01, b_f32):
    p1, p2, p3 = _split3(b_f32)
    lhs = jnp.concatenate([a01, a01, a01], axis=1)
    rhs = jnp.concatenate([p1, p2, p3], axis=0)
    return _dot(lhs, rhs)


def _rms_rows(x, gain):
    ms = jnp.mean(x * x, axis=-1, keepdims=True)
    return x * lax.rsqrt(ms + EPS) * gain


def _mod_kernel(c_ref, w_ref, b_ref, o_ref):
    c = c_ref[...]
    a = (c * jax.nn.sigmoid(c)).astype(BF16)
    o_ref[...] = _dot(a, w_ref[...].astype(BF16)) + b_ref[...]


def _modulation(cvec, w, b):
    rows, d = cvec.shape
    n = w.shape[1]
    tn = 1536
    return pl.pallas_call(
        _mod_kernel,
        grid=(n // tn,),
        in_specs=[pl.BlockSpec((rows, d), lambda j: (0, 0)),
                  pl.BlockSpec((d, tn), lambda j: (0, j)),
                  pl.BlockSpec((1, tn), lambda j: (0, j))],
        out_specs=pl.BlockSpec((rows, tn), lambda j: (0, j)),
        out_shape=jax.ShapeDtypeStruct((rows, n), F32),
        compiler_params=_cparams(("arbitrary",)),
        name="modulation",
    )(cvec, w, b.reshape(1, n))


def _proj_kernel(x_ref, mod_ref, g_ref, *rest, n_out):
    w_refs, o_refs = rest[:n_out], rest[n_out:]
    mod = mod_ref[0]
    h = _rms_rows(x_ref[0], g_ref[...]) * (1.0 + mod[1:2]) + mod[0:1]
    hb = h.astype(BF16)
    for w_ref, o_ref in zip(w_refs, o_refs):
        o_ref[0] = _dot(hb, w_ref[...]).astype(o_ref.dtype)


def _project(x, mod, gain, weights, tm):
    g, t, d = x.shape
    n_out = len(weights)
    in_specs = [pl.BlockSpec((1, tm, d), lambda i, j: (i, j, 0)),
                pl.BlockSpec((1, N_MOD, d), lambda i, j: (i, 0, 0)),
                pl.BlockSpec((1, d), lambda i, j: (0, 0))]
    in_specs += [pl.BlockSpec(w.shape, lambda i, j: (0, 0)) for w in weights]
    out_specs = [pl.BlockSpec((1, tm, w.shape[1]), lambda i, j: (i, j, 0)) for w in weights]
    out_shape = [jax.ShapeDtypeStruct((g, t, w.shape[1]), F32) for w in weights]
    return pl.pallas_call(
        functools.partial(_proj_kernel, n_out=n_out),
        grid=(g, t // tm),
        in_specs=in_specs, out_specs=out_specs, out_shape=out_shape,
        compiler_params=_cparams(("arbitrary", "arbitrary")),
        name="norm_project",
    )(x, mod, gain.reshape(1, d), *weights)


def _mlp_kernel(*refs, n_mix):
    x_ref, mod_ref, g_ref = refs[:3]
    o_refs = refs[3:3 + n_mix]
    wo_refs = refs[3 + n_mix:3 + 2 * n_mix]
    w1_ref, w2_ref, out_ref, x1_s, h_s, acc_s = refs[3 + 2 * n_mix:]
    f = pl.program_id(2)

    @pl.when(f == 0)
    def _():
        mod = mod_ref[0]
        mp = _dot(o_refs[0][0].astype(BF16), wo_refs[0][...])
        for o_ref, wo_ref in zip(o_refs[1:], wo_refs[1:]):
            mp += _dot(o_ref[0].astype(BF16), wo_ref[...])
        x1 = x_ref[0] + mod[2:3] * mp
        x1_s[...] = x1
        h = _rms_rows(x1, g_ref[...]) * (1.0 + mod[4:5]) + mod[3:4]
        h_s[...] = h.astype(BF16)
        acc_s[...] = jnp.zeros_like(acc_s)

    a = _dot(h_s[...], w1_ref[...])
    a = jnp.square(jnp.maximum(a, 0.0))
    acc_s[...] += _dot(a.astype(BF16), w2_ref[...])

    @pl.when(f == pl.num_programs(2) - 1)
    def _():
        out_ref[0] = x1_s[...] + mod_ref[0][5:6] * acc_s[...]


def _mix_mlp(x, mod, gain2, mixes, w_outs, w1, w2, tm, tf):
    g, t, d = x.shape
    dff = w1.shape[1]
    n_mix = len(mixes)
    in_specs = [pl.BlockSpec((1, tm, d), lambda i, j, f: (i, j, 0)),
                pl.BlockSpec((1, N_MOD, d), lambda i, j, f: (i, 0, 0)),
                pl.BlockSpec((1, d), lambda i, j, f: (0, 0))]
    in_specs += [pl.BlockSpec((1, tm, o.shape[2]), lambda i, j, f: (i, j, 0)) for o in mixes]
    in_specs += [pl.BlockSpec(w.shape, lambda i, j, f: (0, 0)) for w in w_outs]
    in_specs += [pl.BlockSpec((d, tf), lambda i, j, f: (0, f)),
                 pl.BlockSpec((tf, d), lambda i, j, f: (f, 0))]
    return pl.pallas_call(
        functools.partial(_mlp_kernel, n_mix=n_mix),
        grid=(g, t // tm, dff // tf),
        in_specs=in_specs,
        out_specs=pl.BlockSpec((1, tm, d), lambda i, j, f: (i, j, 0)),
        out_shape=jax.ShapeDtypeStruct((g, t, d), F32),
        scratch_shapes=[pltpu.VMEM((tm, d), F32), pltpu.VMEM((tm, d), BF16), pltpu.VMEM((tm, d), F32)],
        compiler_params=_cparams(("arbitrary", "arbitrary", "arbitrary")),
        name="mix_mlp",
    )(x, mod, gain2.reshape(1, d), *mixes, *w_outs, w1, w2)


def _pair_ones():
    r = lax.broadcasted_iota(jnp.int32, (LANES, LANES), 0) // HEAD_DIM
    c = lax.broadcasted_iota(jnp.int32, (LANES, LANES), 1) // HEAD_DIM
    return jnp.where(r == c, 1.0, 0.0).astype(BF16)


def _head_rms(xb, gain, pair_ones):
    hi, lo = _split2(xb * xb)
    ss = _dot(jnp.concatenate([hi, lo], axis=1), jnp.concatenate([pair_ones, pair_ones], axis=0))
    return xb * lax.rsqrt(ss * (1.0 / HEAD_DIM) + EPS) * gain


def _dup_half(x, g):
    lane = lax.broadcasted_iota(jnp.int32, x.shape, 1)
    sw = pltpu.roll(x, HEAD_DIM, 1)
    if g == 0:
        return jnp.where(lane < HEAD_DIM, x, sw)
    return jnp.where(lane < HEAD_DIM, sw, x)


def _softmax_parts(scores):
    m = jnp.max(scores[0], axis=-1, keepdims=True)
    for s in scores[1:]:
        m = jnp.maximum(m, jnp.max(s, axis=-1, keepdims=True))
    ps = [jnp.exp(s - m) for s in scores]
    l = jnp.sum(ps[0], axis=-1, keepdims=True)
    for p in ps[1:]:
        l += jnp.sum(p, axis=-1, keepdims=True)
    return ps, 1.0 / l


def _attend_pair(qb, ks, vs, biases=None):
    lane = lax.broadcasted_iota(jnp.int32, qb.shape, 1)
    outs = []
    for hh in range(2):
        sel = (lane < HEAD_DIM) if hh == 0 else (lane >= HEAD_DIM)
        qh = jnp.where(sel, qb, jnp.zeros_like(qb))
        scores = [_dot_nt(qh, k) for k in ks]
        if biases is not None:
            scores = [s if b is None else s + b for s, b in zip(scores, biases[hh])]
        ps, inv = _softmax_parts(scores)
        o = _dot(ps[0].astype(BF16), vs[0])
        for p, v in zip(ps[1:], vs[1:]):
            o += _dot(p.astype(BF16), v)
        outs.append(o * inv)
    lane_o = lax.broadcasted_iota(jnp.int32, outs[0].shape, 1)
    return jnp.where(lane_o < HEAD_DIM, outs[0], outs[1])


_QA0, _KA, _VA, _QB0, _KB0, _VB0 = 0, 4, 5, 6, 10, 14
_N_QKV_BLOCKS = 18


def _ctx_attn_kernel(qkv_ref, gain_ref, o_ref, nk_ref, nv_ref):
    ones = _pair_ones()

    def blk(j):
        return qkv_ref[0, :, j * LANES:(j + 1) * LANES]

    def gain(j):
        return gain_ref[:, j * LANES:(j + 1) * LANES]

    ka = _head_rms(blk(_KA), gain(_KA), ones)
    va = blk(_VA)
    nk_ref[0, :, 0:LANES] = ka
    nv_ref[0, :, 0:LANES] = va
    for j in range(4):
        g = j // 2
        qn = _head_rms(blk(_QA0 + j), gain(_QA0 + j), ones).astype(BF16)
        o = _attend_pair(qn, [_dup_half(ka, g).astype(BF16)], [_dup_half(va, g).astype(BF16)])
        o_ref[0, :, j * LANES:(j + 1) * LANES] = o.astype(o_ref.dtype)
    for j in range(4):
        kb = _head_rms(blk(_KB0 + j), gain(_KB0 + j), ones)
        vb = blk(_VB0 + j)
        nk_ref[0, :, (1 + j) * LANES:(2 + j) * LANES] = kb
        nv_ref[0, :, (1 + j) * LANES:(2 + j) * LANES] = vb
        qn = _head_rms(blk(_QB0 + j), gain(_QB0 + j), ones).astype(BF16)
        o = _attend_pair(qn, [kb.astype(BF16)], [vb.astype(BF16)])
        o_ref[0, :, (4 + j) * LANES:(5 + j) * LANES] = o.astype(o_ref.dtype)


def _ctx_attention(qkv, gain_row):
    b, t, n = qkv.shape
    n_cache = (N_KV_A + N_HEADS_B) * HEAD_DIM
    n_o = (N_HEADS_A + N_HEADS_B) * HEAD_DIM
    return pl.pallas_call(
        _ctx_attn_kernel,
        grid=(b,),
        in_specs=[pl.BlockSpec((1, t, n), lambda i: (i, 0, 0)),
                  pl.BlockSpec((1, n), lambda i: (0, 0))],
        out_specs=[pl.BlockSpec((1, t, n_o), lambda i: (i, 0, 0)),
                   pl.BlockSpec((1, t, n_cache), lambda i: (i, 0, 0)),
                   pl.BlockSpec((1, t, n_cache), lambda i: (i, 0, 0))],
        out_shape=[jax.ShapeDtypeStruct((b, t, n_o), BF16),
                   jax.ShapeDtypeStruct((b, t, n_cache), F32),
                   jax.ShapeDtypeStruct((b, t, n_cache), F32)],
        compiler_params=_cparams(("arbitrary",)),
        name="ctx_attention",
    )(qkv, gain_row)


def _rope(x, c, sa, sb):
    return x * c + pltpu.roll(x, LANES - 16, 1) * sa + pltpu.roll(x, 16, 1) * sb


def _lat_a_kernel(q_ref, k_ref, v_ref, ck_ref, cv_ref, gq_ref, gk_ref,
                  cq_ref, saq_ref, sbq_ref, ck_t_ref, sak_t_ref, sbk_t_ref,
                  o_ref, k2_s, v2_s):
    t = k_ref.shape[1]
    ones = _pair_ones()

    @pl.when(pl.program_id(1) == 0)
    def _():
        kn = _head_rms(k_ref[0], gk_ref[...], ones)
        kr = _rope(kn, ck_t_ref[...], sak_t_ref[...], sbk_t_ref[...])
        v = v_ref[0]
        ck = ck_ref[0]
        cv = cv_ref[0]
        for g in range(N_KV_A):
            k2_s[g, 0:t, :] = _dup_half(kr, g).astype(BF16)
            k2_s[g, t:, :] = _dup_half(ck, g).astype(BF16)
            v2_s[g, 0:t, :] = _dup_half(v, g).astype(BF16)
            v2_s[g, t:, :] = _dup_half(cv, g).astype(BF16)

    c, sa, sb = cq_ref[...], saq_ref[...], sbq_ref[...]
    for j in range(4):
        g = j // 2
        qn = _head_rms(q_ref[0, :, j * LANES:(j + 1) * LANES], gq_ref[:, j * LANES:(j + 1) * LANES], ones)
        qb = _rope(qn, c, sa, sb).astype(BF16)
        o = _attend_pair(qb, [k2_s[g]], [v2_s[g]])
        o_ref[0, :, j * LANES:(j + 1) * LANES] = o.astype(o_ref.dtype)


def _rope_tables(t):
    pos = jnp.arange(t, dtype=jnp.int32)
    rc = jnp.stack([pos // GRID_W, pos % GRID_W], axis=-1).astype(F32)
    axis_dim = HEAD_DIM // 2
    inv_freq = 1.0 / (ROPE_THETA ** (jnp.arange(0, axis_dim, 2, dtype=F32) / axis_dim))
    ang = rc[:, :, None] * inv_freq
    cos, sin = jnp.cos(ang), jnp.sin(ang)
    zero = jnp.zeros_like(sin)
    c = jnp.stack([cos, cos], axis=2).reshape(t, HEAD_DIM)
    sa = jnp.stack([-sin, zero], axis=2).reshape(t, HEAD_DIM)
    sb = jnp.stack([zero, sin], axis=2).reshape(t, HEAD_DIM)
    two = lambda a: jnp.concatenate([a, a], axis=-1)
    return two(c), two(sa), two(sb)


def _latent_attention_a(qkv, cache_k, cache_v, gain_row, tq):
    b, t, _ = qkv.shape
    p = cache_k.shape[1]
    c, sa, sb = _rope_tables(t)
    gq = gain_row[:, _QA0 * LANES:(_QA0 + 4) * LANES]
    gk = gain_row[:, _KA * LANES:(_KA + 1) * LANES]
    tab_q = pl.BlockSpec((tq, LANES), lambda i, j: (j, 0))
    tab_k = pl.BlockSpec((t, LANES), lambda i, j: (0, 0))
    return pl.pallas_call(
        _lat_a_kernel,
        grid=(b, t // tq),
        in_specs=[pl.BlockSpec((1, tq, 4 * LANES), lambda i, j: (i, j, 0)),
                  pl.BlockSpec((1, t, LANES), lambda i, j: (i, 0, _KA)),
                  pl.BlockSpec((1, t, LANES), lambda i, j: (i, 0, _VA)),
                  pl.BlockSpec((1, p, LANES), lambda i, j: (i, 0, 0)),
                  pl.BlockSpec((1, p, LANES), lambda i, j: (i, 0, 0)),
                  pl.BlockSpec((1, 4 * LANES), lambda i, j: (0, 0)),
                  pl.BlockSpec((1, LANES), lambda i, j: (0, 0)),
                  tab_q, tab_q, tab_q, tab_k, tab_k, tab_k],
        out_specs=pl.BlockSpec((1, tq, 4 * LANES), lambda i, j: (i, j, 0)),
        out_shape=jax.ShapeDtypeStruct((b, t, 4 * LANES), BF16),
        scratch_shapes=[pltpu.VMEM((N_KV_A, t + p, LANES), BF16), pltpu.VMEM((N_KV_A, t + p, LANES), BF16)],
        compiler_params=_cparams(("arbitrary", "arbitrary")),
        name="latent_attention_a",
    )(qkv, qkv, qkv, cache_k, cache_v, gq, gk, c, sa, sb, c, sa, sb)


def _nbr_bias_table(rel_bias, rows):
    wr = min(WIN_R, rows)
    cols = jnp.arange(GRID_W, dtype=jnp.int32)
    col_start = jnp.clip(cols - WIN_C // 2, 0, GRID_W - WIN_C)
    kc = jnp.arange(GRID_W, dtype=jnp.int32)
    in_win = (kc[None, :] >= col_start[:, None]) & (kc[None, :] < col_start[:, None] + WIN_C)
    col_off = jnp.clip(kc[None, :] - cols[:, None] + (WIN_C - 1), 0, 2 * WIN_C - 2)
    variants = []
    for var in range(wr):
        row_off = jnp.arange(wr, dtype=jnp.int32) - var + (WIN_R - 1)
        bias = rel_bias[:, row_off][:, :, col_off].astype(F32)
        bias = jnp.where(in_win[None, None], bias, NEG_BIG)
        variants.append(jnp.transpose(bias, (0, 2, 1, 3)).reshape(rel_bias.shape[0], GRID_W, wr * GRID_W))
    return jnp.stack(variants, axis=1)


def _nbr_kernel(q_ref, k_ref, v_ref, ck_ref, cv_ref, gq_ref, gk_ref, bias_ref, o_ref, qn_s, kn_s, vn_s):
    t = q_ref.shape[1]
    rows = t // GRID_W
    wr = min(WIN_R, rows)
    ones = _pair_ones()
    qn_s[...] = _head_rms(q_ref[0], gq_ref[0], ones).astype(BF16)
    kn_s[...] = _head_rms(k_ref[0], gk_ref[0], ones).astype(BF16)
    vn_s[...] = v_ref[0].astype(BF16)
    ck = ck_ref[0].astype(BF16)
    cv = cv_ref[0].astype(BF16)

    def row_block(r, carry):
        rs = jnp.clip(r - wr // 2, 0, rows - wr)
        var = r - rs
        q0 = pl.multiple_of(r * GRID_W, GRID_W)
        k0 = pl.multiple_of(rs * GRID_W, GRID_W)
        qb = qn_s[pl.ds(q0, GRID_W), :]
        kw = kn_s[pl.ds(k0, wr * GRID_W), :]
        vw = vn_s[pl.ds(k0, wr * GRID_W), :]
        biases = [[bias_ref[hh, var], None] for hh in range(2)]
        o = _attend_pair(qb, [kw, ck], [vw, cv], biases)
        o_ref[0, pl.ds(q0, GRID_W), :] = o.astype(o_ref.dtype)
        return carry

    lax.fori_loop(0, rows, row_block, 0)


def _latent_attention_b(qkv, cache_k, cache_v, gain_row, rel_bias):
    b, t, _ = qkv.shape
    p = cache_k.shape[1]
    rows = t // GRID_W
    wr = min(WIN_R, rows)
    table = _nbr_bias_table(rel_bias, rows)
    table = table.reshape(N_HEADS_B // 2, 2, wr, GRID_W, wr * GRID_W)
    gq = gain_row[:, _QB0 * LANES:(_QB0 + 4) * LANES].reshape(4, 1, LANES)
    gk = gain_row[:, _KB0 * LANES:(_KB0 + 4) * LANES].reshape(4, 1, LANES)
    return pl.pallas_call(
        _nbr_kernel,
        grid=(b, N_HEADS_B // 2),
        in_specs=[pl.BlockSpec((1, t, LANES), lambda i, j: (i, 0, _QB0 + j)),
                  pl.BlockSpec((1, t, LANES), lambda i, j: (i, 0, _KB0 + j)),
                  pl.BlockSpec((1, t, LANES), lambda i, j: (i, 0, _VB0 + j)),
                  pl.BlockSpec((1, p, LANES), lambda i, j: (i, 0, 1 + j)),
                  pl.BlockSpec((1, p, LANES), lambda i, j: (i, 0, 1 + j)),
                  pl.BlockSpec((1, 1, LANES), lambda i, j: (j, 0, 0)),
                  pl.BlockSpec((1, 1, LANES), lambda i, j: (j, 0, 0)),
                  pl.BlockSpec((None, 2, wr, GRID_W, wr * GRID_W), lambda i, j: (j, 0, 0, 0, 0))],
        out_specs=pl.BlockSpec((1, t, LANES), lambda i, j: (i, 0, j)),
        out_shape=jax.ShapeDtypeStruct((b, t, 4 * LANES), BF16),
        scratch_shapes=[pltpu.VMEM((t, LANES), BF16)] * 3,
        compiler_params=_cparams(("arbitrary", "arbitrary")),
        name="latent_attention_b",
    )(qkv, qkv, qkv, cache_k, cache_v, gq, gk, table)


def _delta_kernel(q_ref, k_ref, v_ref, cwq_ref, cwk_ref, cwv_ref, ab_ref, alog_ref, dtb_ref, z_ref, onorm_ref,
                  s0_ref, y_ref, sfin_ref,
                  pad_s, q_s, k_s, v_s, la_s, be_s, u_s, wq_s, kt_s, att_s, el_s, o_s):
    t = q_ref.shape[1]
    c = DELTA_CHUNK
    n = t // c
    h = pl.program_id(1)

    row_t = lax.broadcasted_iota(jnp.int32, (t, 1), 0)
    pad_s[0:8, :] = jnp.zeros((8, LANES), F32)
    pad_s[t + 8:t + 16, :] = jnp.zeros((8, LANES), F32)

    def conv_silu(x_ref, cw_ref):
        pad_s[8:t + 8, :] = x_ref[0]
        cw = cw_ref[...]
        y = pad_s[7:t + 7, :] * cw[0:1] + pad_s[8:t + 8, :] * cw[1:2] + pad_s[9:t + 9, :] * cw[2:3]
        return y * jax.nn.sigmoid(y)

    q = conv_silu(q_ref, cwq_ref)
    q_s[...] = q * lax.rsqrt(jnp.sum(q * q, axis=-1, keepdims=True) + EPS) * (DK_C ** -0.5)
    k = conv_silu(k_ref, cwk_ref)
    k_s[...] = k * lax.rsqrt(jnp.sum(k * k, axis=-1, keepdims=True) + EPS)
    v_s[...] = conv_silu(v_ref, cwv_ref)
    del row_t

    ab = ab_ref[0]
    xa = ab + dtb_ref[...]
    softplus = jnp.maximum(xa, 0.0) + jnp.log1p(jnp.exp(-jnp.abs(xa)))
    la_all = -jnp.exp(alog_ref[...]) * softplus
    be_all = jax.nn.sigmoid(ab)
    sel_r = lax.broadcasted_iota(jnp.int32, (LANES, LANES), 0)
    for d in range(2):
        pick_a = jnp.where(sel_r == d * N_HEADS_C + h, 1.0, 0.0).astype(BF16)
        pick_b = jnp.where(sel_r == 2 * N_HEADS_C + d * N_HEADS_C + h, 1.0, 0.0).astype(BF16)
        la_s[d] = _dot_exact_rhs01(la_all, pick_a)
        be_s[d] = _dot_exact_rhs01(be_all, pick_b)

    ii = lax.broadcasted_iota(jnp.int32, (c, c), 0)
    jj = lax.broadcasted_iota(jnp.int32, (c, c), 1)
    eye = jnp.where(ii == jj, 1.0, 0.0)
    ones_cc = jnp.ones((c, c), BF16)
    bd8 = (ii // 8) == (jj // 8)
    offs = []
    blk = 8
    while blk < c:
        offs.append(((ii // (2 * blk)) == (jj // (2 * blk))) & ((ii // blk) != (jj // blk)))
        blk *= 2

    def prep(ci, carry):
        r0 = pl.multiple_of(ci * c, c)
        qc = q_s[pl.ds(r0, c), :]
        kc = k_s[pl.ds(r0, c), :]
        vc = v_s[pl.ds(r0, c), :]
        qb, kb = qc.astype(BF16), kc.astype(BF16)
        kk = _dot_nt(kb, kb)
        qk = _dot_nt(qb, kb)
        for d in range(2):
            la = la_s[d, pl.ds(r0, c), :]
            be = be_s[d, pl.ds(r0, c), :]
            if d == 0:
                incl, strict, incl_t = ii >= jj, ii > jj, ii <= jj
            else:
                incl, strict, incl_t = ii <= jj, ii < jj, ii >= jj
            cum_mask = jnp.where(incl, 1.0, 0.0).astype(BF16)
            g_col = _dot_exact_lhs01(cum_mask, la)
            g_row = _dot_exact_lhs01(ones_cc, jnp.where(incl_t, la, 0.0))
            decay = jnp.where(incl, jnp.exp(jnp.where(incl, g_col - g_row, 0.0)), 0.0)
            a = jnp.where(strict, (kk * decay) * be, 0.0)
            p8 = jnp.where(bd8, -a, 0.0)
            pk2 = _dot3(p8, p8)
            acc = eye + p8
            r = _dot3(pk2, jnp.concatenate([pk2, acc], axis=1))
            acc = acc + r[:, c:]
            tmat = acc + _dot3(r[:, :c], acc)
            for off in offs:
                tmat = tmat - _dot3(tmat, _dot3(jnp.where(off, a, 0.0), tmat))
            eg = jnp.exp(g_col)
            rhs = jnp.concatenate([vc * be, kc * be * eg], axis=1)
            uw = _dot3(tmat, rhs)
            g_last = g_col[c - 1:c, :] if d == 0 else g_col[0:1, :]
            u_s[d, pl.ds(r0, c), :] = uw[:, :c]
            r2 = pl.multiple_of(ci * 2 * c, 2 * c)
            wq_s[d, pl.ds(r2, c), :] = uw[:, c:].astype(BF16)
            wq_s[d, pl.ds(r2 + c, c), :] = (qc * eg).astype(BF16)
            kt_s[d, pl.ds(r0, c), :] = (kc * jnp.exp(g_last - g_col)).astype(BF16)
            att_s[d, pl.ds(r0, c), :] = (qk * decay).astype(BF16)
            r8 = pl.multiple_of(ci * 8, 8)
            el_s[d, pl.ds(r8, 8), :] = jnp.broadcast_to(jnp.exp(g_last), (8, LANES))
        return carry

    lax.fori_loop(0, n, prep, 0)

    def scan_dir(d, ci, s):
        r0 = pl.multiple_of(ci * c, c)
        r2 = pl.multiple_of(ci * 2 * c, 2 * c)
        r8 = pl.multiple_of(ci * 8, 8)
        ws = _dot(wq_s[d, pl.ds(r2, 2 * c), :], s.astype(BF16))
        v_new = u_s[d, pl.ds(r0, c), :] - ws[:c]
        vb = v_new.astype(BF16)
        o = ws[c:] + _dot(att_s[d, pl.ds(r0, c), :], vb)
        s = s * el_s[d, pl.ds(r8, 1), :] + _dot_tn(kt_s[d, pl.ds(r0, c), :], vb)
        return o, s

    def scan(i, carry):
        s_f, s_b = carry
        o_f, s_f = scan_dir(0, i, s_f)
        o_s[0, pl.ds(pl.multiple_of(i * c, c), c), :] = o_f
        ib = n - 1 - i
        o_b, s_b = scan_dir(1, ib, s_b)
        o_s[1, pl.ds(pl.multiple_of(ib * c, c), c), :] = o_b
        return s_f, s_b

    s_f, s_b = lax.fori_loop(0, n, scan, (s0_ref[0, 0, 0], s0_ref[0, 1, 0]))
    sfin_ref[0, 0, 0] = s_f
    sfin_ref[0, 1, 0] = s_b

    o = o_s[0] + o_s[1]
    z = z_ref[0]
    y = _rms_rows(o, onorm_ref[...]) * (z * jax.nn.sigmoid(z))
    y_ref[0] = y.astype(y_ref.dtype)


def _delta_mixer(qkv, z, ab, conv_w, alog_row, dtb_row, out_norm, s0):
    b, t, _ = qkv.shape
    hh = N_HEADS_C
    c = DELTA_CHUNK
    n = t // c
    col = lambda off: pl.BlockSpec((1, t, LANES), lambda i, j, off=off: (i, 0, off + j))
    cw = lambda off: pl.BlockSpec((3, LANES), lambda i, j, off=off: (0, off + j))
    row = pl.BlockSpec((1, LANES), lambda i, j: (0, 0))
    st = pl.BlockSpec((1, 2, 1, DK_C, DV_C), lambda i, j: (i, 0, j, 0, 0))
    return pl.pallas_call(
        _delta_kernel,
        grid=(b, hh),
        in_specs=[col(0), col(hh), col(2 * hh), cw(0), cw(hh), cw(2 * hh),
                  pl.BlockSpec((1, t, LANES), lambda i, j: (i, 0, 0)), row, row,
                  pl.BlockSpec((1, t, LANES), lambda i, j: (i, 0, j)), row, st],
        out_specs=[pl.BlockSpec((1, t, LANES), lambda i, j: (i, 0, j)), st],
        out_shape=[jax.ShapeDtypeStruct((b, t, hh * DV_C), BF16),
                   jax.ShapeDtypeStruct((b, 2, hh, DK_C, DV_C), F32)],
        scratch_shapes=[pltpu.VMEM((t + 16, LANES), F32),
                        pltpu.VMEM((t, LANES), F32), pltpu.VMEM((t, LANES), F32), pltpu.VMEM((t, LANES), F32),
                        pltpu.VMEM((2, t, LANES), F32), pltpu.VMEM((2, t, LANES), F32),
                        pltpu.VMEM((2, t, LANES), F32), pltpu.VMEM((2, 2 * t, LANES), BF16),
                        pltpu.VMEM((2, t, LANES), BF16), pltpu.VMEM((2, t, LANES), BF16),
                        pltpu.VMEM((2, 8 * n, LANES), F32), pltpu.VMEM((2, t, LANES), F32)],
        compiler_params=_cparams(("arbitrary", "arbitrary")),
        name="delta_mixer",
    )(qkv, qkv, qkv, conv_w, conv_w, conv_w, ab, alog_row, dtb_row, z, out_norm.reshape(1, DV_C), s0)


def _pad_lanes(x, n):
    return jnp.pad(x, ((0, 0), (0, n - x.shape[1])))


def kernel(x_prompt, x_sample, c, cache_l0_k, cache_l0_v, state_l1, c_ctx, l0_mod_w, l0_mod_b, l0_norm1, l0_w_in, l0_q_norm_a, l0_k_norm_a, l0_q_norm_b, l0_k_norm_b, l0_rel_bias, l0_w_out, l0_norm2, l0_mlp_w1, l0_mlp_w2, l1_mod_w, l1_mod_b, l1_norm1, l1_w_in, l1_conv_w, l1_a_log, l1_dt_bias, l1_out_norm, l1_w_out, l1_norm2, l1_mlp_w1, l1_mlp_w2):
    bp, tp, d = x_prompt.shape
    bs, ts, _ = x_sample.shape
    n_cache = (N_KV_A + N_HEADS_B) * HEAD_DIM
    bf = lambda w: w.astype(BF16)

    n_rows = -(-(1 + bs) // 8) * 8
    cvec = jnp.concatenate([c_ctx[None, :], c, jnp.zeros((n_rows - 1 - bs, d), F32)], axis=0)
    mods = []
    for mw, mb in ((l0_mod_w, l0_mod_b), (l1_mod_w, l1_mod_b)):
        m = _modulation(cvec, mw, mb).reshape(n_rows, N_MOD, d)
        mods.append((m[0:1], m[1:1 + bs]))

    xp = x_prompt.reshape(1, bp * tp, d)
    xs = x_sample

    scale = HEAD_DIM ** -0.5
    ones_a = jnp.ones((N_KV_A * HEAD_DIM,), F32)
    ones_b = jnp.ones((N_HEADS_B * HEAD_DIM,), F32)
    gain_row = jnp.concatenate([jnp.tile(l0_q_norm_a, N_HEADS_A) * scale, jnp.tile(l0_k_norm_a, N_KV_A), ones_a,
                                jnp.tile(l0_q_norm_b, N_HEADS_B) * scale, jnp.tile(l0_k_norm_b, N_HEADS_B),
                                ones_b])[None, :]
    w_in0 = bf(l0_w_in)
    w_out0 = bf(l0_w_out)
    w1_0, w2_0 = bf(l0_mlp_w1), bf(l0_mlp_w2)
    half = N_HEADS_A * HEAD_DIM

    mod_p, mod_s = mods[0]
    (qkv_p,) = _project(xp, mod_p, l0_norm1, [w_in0], tm=512)
    o_p, new_k, new_v = _ctx_attention(qkv_p.reshape(bp, tp, -1), gain_row)
    xp = _mix_mlp(xp, mod_p, l0_norm2, [o_p.reshape(1, bp * tp, -1)], [w_out0], w1_0, w2_0, tm=512, tf=1024)

    (qkv_s,) = _project(xs, mod_s, l0_norm1, [w_in0], tm=512)
    ck = cache_l0_k.reshape(bs, -1, n_cache)
    cv = cache_l0_v.reshape(bs, -1, n_cache)
    o_a = _latent_attention_a(qkv_s, ck, cv, gain_row, tq=256)
    o_b = _latent_attention_b(qkv_s, ck, cv, gain_row, l0_rel_bias)
    xs = _mix_mlp(xs, mod_s, l0_norm2, [o_a, o_b], [w_out0[:half], w_out0[half:]], w1_0, w2_0, tm=512, tf=1024)

    n_qkv = N_HEADS_C * (2 * DK_C + DV_C)
    n_z = N_HEADS_C * DV_C
    w_in1 = bf(l1_w_in)
    w_pieces = [w_in1[:, :n_qkv], w_in1[:, n_qkv:n_qkv + n_z], _pad_lanes(w_in1[:, n_qkv + n_z:], LANES)]
    w_out1 = bf(l1_w_out)
    w1_1, w2_1 = bf(l1_mlp_w1), bf(l1_mlp_w2)
    alog_row = _pad_lanes(l1_a_log.reshape(1, -1), LANES)
    dtb_row = _pad_lanes(l1_dt_bias.reshape(1, -1), LANES)

    mod_p, mod_s = mods[1]
    qkv1_p, z_p, ab_p = _project(xp, mod_p, l1_norm1, w_pieces, tm=256)
    s0_p = jnp.zeros((bp, 2, N_HEADS_C, DK_C, DV_C), F32)
    y_p, new_s = _delta_mixer(qkv1_p.reshape(bp, tp, -1), z_p.reshape(bp, tp, -1), ab_p.reshape(bp, tp, -1),
                              l1_conv_w, alog_row, dtb_row, l1_out_norm, s0_p)
    xp = _mix_mlp(xp, mod_p, l1_norm2, [y_p.reshape(1, bp * tp, -1)], [w_out1], w1_1, w2_1, tm=512, tf=1024)

    qkv1_s, z_s, ab_s = _project(xs, mod_s, l1_norm1, w_pieces, tm=256)
    y_s, _ = _delta_mixer(qkv1_s, z_s, ab_s, l1_conv_w, alog_row, dtb_row, l1_out_norm, state_l1.astype(F32))
    xs = _mix_mlp(xs, mod_s, l1_norm2, [y_s], [w_out1], w1_1, w2_1, tm=512, tf=1024)

    return (xp.reshape(bp, tp, d), xs,
            new_k.reshape(bp, tp, N_KV_A + N_HEADS_B, HEAD_DIM), new_v.reshape(bp, tp, N_KV_A + N_HEADS_B, HEAD_DIM),
            new_s.astype(x_prompt.dtype))
```

```python
import functools
import math

import jax
import jax.numpy as jnp
from jax import lax
from jax.experimental import pallas as pl
from jax.experimental.pallas import tpu as pltpu

F32 = jnp.float32
BF16 = jnp.bfloat16

EPS = 1e-6
N_MOD = 6
HEAD_DIM = 64
N_HEADS_A = 8
N_KV_A = 2
N_HEADS_B = 8
GRID_W = 64
WIN_R = 8
WIN_C = 16
ROPE_THETA = 10000.0
N_HEADS_C = 8
DK_C = 128
DV_C = 128
LANES = 128
DELTA_CHUNK = 128
DELTA_PREP_GROUP = 4
NBR_ROW_GROUP = 4
NEG_BIG = -1e30

VMEM_LIMIT = 48 * 1024 * 1024


def _cparams(sem):
    return pltpu.CompilerParams(dimension_semantics=sem, vmem_limit_bytes=VMEM_LIMIT)


def _dot(a, b):
    return jnp.dot(a, b, preferred_element_type=F32)


def _dot_nt(a, b):
    return lax.dot_general(a, b, (((1,), (1,)), ((), ())), preferred_element_type=F32)


def _dot_tn(a, b):
    return lax.dot_general(a, b, (((0,), (0,)), ((), ())), preferred_element_type=F32)


def _split2(x):
    hi = x.astype(BF16)
    lo = (x - hi.astype(F32)).astype(BF16)
    return hi, lo


def _split3(x):
    p1 = x.astype(BF16)
    r1 = x - p1.astype(F32)
    p2 = r1.astype(BF16)
    p3 = (r1 - p2.astype(F32)).astype(BF16)
    return p1, p2, p3


def _dot3(a, b):
    ah, al = _split2(a)
    bh, bl = _split2(b)
    lhs = jnp.concatenate([ah, ah, al], axis=1)
    rhs = jnp.concatenate([bh, bl, bh], axis=0)
    return _dot(lhs, rhs)


def _dot_t(a, b):
    ah, al = _split2(a)
    bh = b.astype(BF16)
    return _dot(jnp.concatenate([ah, al], axis=1), jnp.concatenate([bh, bh], axis=0))


def _dot2_rhs01(a_f32, b01):
    hi, lo = _split2(a_f32)
    return _dot(jnp.concatenate([hi, lo], axis=1), jnp.concatenate([b01, b01], axis=0))


def _dot2_lhs01(a01, b_f32):
    hi, lo = _split2(b_f32)
    return _dot(jnp.concatenate([a01, a01], axis=1), jnp.concatenate([hi, lo], axis=0))


def _row_blocks(x, blk, first):
    return jnp.concatenate([x[i * blk:(i + 1) * blk] for i in range(0 if first else 1, x.shape[0] // blk, 2)], axis=0)


def _row_blocks_put(sel, rest, blk, first):
    pieces = []
    for i in range(rest.shape[0] // blk):
        if (i % 2 == 0) == first:
            pieces.append(sel[(i // 2) * blk:(i // 2 + 1) * blk])
        else:
            pieces.append(rest[i * blk:(i + 1) * blk])
    return jnp.concatenate(pieces, axis=0)


def _rms_rows(x, gain):
    ms = jnp.mean(x * x, axis=-1, keepdims=True)
    return x * lax.rsqrt(ms + EPS) * gain


def _mod_kernel(c_ref, w_ref, b_ref, o_ref):
    c = c_ref[...]
    a = (c * jax.nn.sigmoid(c)).astype(BF16)
    o_ref[...] = _dot(a, w_ref[...].astype(BF16)) + b_ref[...]


def _modulation(cvec, w, b):
    rows, d = cvec.shape
    n = w.shape[1]
    tn = 1536
    return pl.pallas_call(
        _mod_kernel,
        grid=(n // tn,),
        in_specs=[pl.BlockSpec((rows, d), lambda j: (0, 0)),
                  pl.BlockSpec((d, tn), lambda j: (0, j)),
                  pl.BlockSpec((1, tn), lambda j: (0, j))],
        out_specs=pl.BlockSpec((rows, tn), lambda j: (0, j)),
        out_shape=jax.ShapeDtypeStruct((rows, n), F32),
        compiler_params=_cparams(("arbitrary",)),
        name="modulation",
    )(cvec, w, b.reshape(1, n))


def _proj_kernel(x_ref, mod_ref, g_ref, *rest, n_out):
    w_refs, o_refs = rest[:n_out], rest[n_out:]
    mod = mod_ref[0]
    h = _rms_rows(x_ref[0], g_ref[...]) * (1.0 + mod[1:2]) + mod[0:1]
    hb = h.astype(BF16)
    for w_ref, o_ref in zip(w_refs, o_refs):
        o_ref[0] = _dot(hb, w_ref[...]).astype(o_ref.dtype)


def _project(x, mod, gain, weights, tm):
    g, t, d = x.shape
    n_out = len(weights)
    in_specs = [pl.BlockSpec((1, tm, d), lambda i, j: (i, j, 0)),
                pl.BlockSpec((1, N_MOD, d), lambda i, j: (i, 0, 0)),
                pl.BlockSpec((1, d), lambda i, j: (0, 0))]
    in_specs += [pl.BlockSpec(w.shape, lambda i, j: (0, 0)) for w in weights]
    out_specs = [pl.BlockSpec((1, tm, w.shape[1]), lambda i, j: (i, j, 0)) for w in weights]
    out_shape = [jax.ShapeDtypeStruct((g, t, w.shape[1]), F32) for w in weights]
    return pl.pallas_call(
        functools.partial(_proj_kernel, n_out=n_out),
        grid=(g, t // tm),
        in_specs=in_specs, out_specs=out_specs, out_shape=out_shape,
        compiler_params=_cparams(("arbitrary", "arbitrary")),
        name="norm_project",
    )(x, mod, gain.reshape(1, d), *weights)


def _mlp_kernel(*refs, n_mix):
    x_ref, mod_ref, g_ref = refs[:3]
    o_refs = refs[3:3 + n_mix]
    wo_refs = refs[3 + n_mix:3 + 2 * n_mix]
    w1_ref, w2_ref, out_ref, x1_s, h_s, acc_s = refs[3 + 2 * n_mix:]
    f = pl.program_id(2)

    @pl.when(f == 0)
    def _():
        mod = mod_ref[0]
        mp = _dot(o_refs[0][0].astype(BF16), wo_refs[0][...])
        for o_ref, wo_ref in zip(o_refs[1:], wo_refs[1:]):
            mp += _dot(o_ref[0].astype(BF16), wo_ref[...])
        x1 = x_ref[0] + mod[2:3] * mp
        x1_s[...] = x1
        h = _rms_rows(x1, g_ref[...]) * (1.0 + mod[4:5]) + mod[3:4]
        h_s[...] = h.astype(BF16)
        acc_s[...] = jnp.zeros_like(acc_s)

    a = _dot(h_s[...], w1_ref[...])
    a = jnp.square(jnp.maximum(a, 0.0))
    acc_s[...] += _dot(a.astype(BF16), w2_ref[...])

    @pl.when(f == pl.num_programs(2) - 1)
    def _():
        out_ref[0] = x1_s[...] + mod_ref[0][5:6] * acc_s[...]


def _mix_mlp(x, mod, gain2, mixes, w_outs, w1, w2, tm, tf):
    g, t, d = x.shape
    dff = w1.shape[1]
    n_mix = len(mixes)
    in_specs = [pl.BlockSpec((1, tm, d), lambda i, j, f: (i, j, 0)),
                pl.BlockSpec((1, N_MOD, d), lambda i, j, f: (i, 0, 0)),
                pl.BlockSpec((1, d), lambda i, j, f: (0, 0))]
    in_specs += [pl.BlockSpec((1, tm, o.shape[2]), lambda i, j, f: (i, j, 0)) for o in mixes]
    in_specs += [pl.BlockSpec(w.shape, lambda i, j, f: (0, 0)) for w in w_outs]
    in_specs += [pl.BlockSpec((d, tf), lambda i, j, f: (0, f)),
                 pl.BlockSpec((tf, d), lambda i, j, f: (f, 0))]
    return pl.pallas_call(
        functools.partial(_mlp_kernel, n_mix=n_mix),
        grid=(g, t // tm, dff // tf),
        in_specs=in_specs,
        out_specs=pl.BlockSpec((1, tm, d), lambda i, j, f: (i, j, 0)),
        out_shape=jax.ShapeDtypeStruct((g, t, d), F32),
        scratch_shapes=[pltpu.VMEM((tm, d), F32), pltpu.VMEM((tm, d), BF16), pltpu.VMEM((tm, d), F32)],
        compiler_params=_cparams(("arbitrary", "arbitrary", "arbitrary")),
        name="mix_mlp",
    )(x, mod, gain2.reshape(1, d), *mixes, *w_outs, w1, w2)


def _pair_ones():
    r = lax.broadcasted_iota(jnp.int32, (LANES, LANES), 0) // HEAD_DIM
    c = lax.broadcasted_iota(jnp.int32, (LANES, LANES), 1) // HEAD_DIM
    return jnp.where(r == c, 1.0, 0.0).astype(BF16)


def _head_rms(xb, gain, pair_ones):
    hi, lo = _split2(xb * xb)
    ss = _dot(jnp.concatenate([hi, lo], axis=1), jnp.concatenate([pair_ones, pair_ones], axis=0))
    return xb * lax.rsqrt(ss * (1.0 / HEAD_DIM) + EPS) * gain


def _dup_half(x, g):
    lane = lax.broadcasted_iota(jnp.int32, x.shape, 1)
    sw = pltpu.roll(x, HEAD_DIM, 1)
    if g == 0:
        return jnp.where(lane < HEAD_DIM, x, sw)
    return jnp.where(lane < HEAD_DIM, sw, x)


def _softmax_parts(scores):
    m = jnp.max(scores[0], axis=-1, keepdims=True)
    for s in scores[1:]:
        m = jnp.maximum(m, jnp.max(s, axis=-1, keepdims=True))
    ps = [jnp.exp(s - m) for s in scores]
    l = jnp.sum(ps[0], axis=-1, keepdims=True)
    for p in ps[1:]:
        l += jnp.sum(p, axis=-1, keepdims=True)
    return ps, 1.0 / l


def _attend_pairs(jobs):
    tasks = []
    for qb, ks, vs, biases in jobs:
        lane = lax.broadcasted_iota(jnp.int32, qb.shape, 1)
        for hh in range(2):
            sel = (lane < HEAD_DIM) if hh == 0 else (lane >= HEAD_DIM)
            tasks.append(dict(q=jnp.where(sel, qb, jnp.zeros_like(qb)), ks=ks, vs=vs,
                              bias=None if biases is None else biases[hh]))
    for tk in tasks:
        scores = [_dot_nt(tk["q"], k) for k in tk["ks"]]
        if tk["bias"] is not None:
            scores = [s if b is None else s + b for s, b in zip(scores, tk["bias"])]
        tk["s"] = scores
    for tk in tasks:
        tk["p"], tk["inv"] = _softmax_parts(tk["s"])
    for tk in tasks:
        o = _dot(tk["p"][0].astype(BF16), tk["vs"][0])
        for p, v in zip(tk["p"][1:], tk["vs"][1:]):
            o += _dot(p.astype(BF16), v)
        tk["o"] = o * tk["inv"]
    outs = []
    for j in range(len(jobs)):
        o0, o1 = tasks[2 * j]["o"], tasks[2 * j + 1]["o"]
        lane_o = lax.broadcasted_iota(jnp.int32, o0.shape, 1)
        outs.append(jnp.where(lane_o < HEAD_DIM, o0, o1))
    return outs


def _attend_pair(qb, ks, vs, biases=None):
    return _attend_pairs([(qb, ks, vs, biases)])[0]


_QA0, _KA, _VA, _QB0, _KB0, _VB0 = 0, 4, 5, 6, 10, 14
_N_QKV_BLOCKS = 18


def _ctx_attn_kernel(qkv_ref, gain_ref, o_ref, nk_ref, nv_ref):
    ones = _pair_ones()

    def blk(j):
        return qkv_ref[0, :, j * LANES:(j + 1) * LANES]

    def gain(j):
        return gain_ref[:, j * LANES:(j + 1) * LANES]

    ka = _head_rms(blk(_KA), gain(_KA), ones)
    va = blk(_VA)
    nk_ref[0, :, 0:LANES] = ka
    nv_ref[0, :, 0:LANES] = va
    for j in range(4):
        g = j // 2
        qn = _head_rms(blk(_QA0 + j), gain(_QA0 + j), ones).astype(BF16)
        o = _attend_pair(qn, [_dup_half(ka, g).astype(BF16)], [_dup_half(va, g).astype(BF16)])
        o_ref[0, :, j * LANES:(j + 1) * LANES] = o.astype(o_ref.dtype)
    for j in range(4):
        kb = _head_rms(blk(_KB0 + j), gain(_KB0 + j), ones)
        vb = blk(_VB0 + j)
        nk_ref[0, :, (1 + j) * LANES:(2 + j) * LANES] = kb
        nv_ref[0, :, (1 + j) * LANES:(2 + j) * LANES] = vb
        qn = _head_rms(blk(_QB0 + j), gain(_QB0 + j), ones).astype(BF16)
        o = _attend_pair(qn, [kb.astype(BF16)], [vb.astype(BF16)])
        o_ref[0, :, (4 + j) * LANES:(5 + j) * LANES] = o.astype(o_ref.dtype)


def _ctx_attention(qkv, gain_row):
    b, t, n = qkv.shape
    n_cache = (N_KV_A + N_HEADS_B) * HEAD_DIM
    n_o = (N_HEADS_A + N_HEADS_B) * HEAD_DIM
    return pl.pallas_call(
        _ctx_attn_kernel,
        grid=(b,),
        in_specs=[pl.BlockSpec((1, t, n), lambda i: (i, 0, 0)),
                  pl.BlockSpec((1, n), lambda i: (0, 0))],
        out_specs=[pl.BlockSpec((1, t, n_o), lambda i: (i, 0, 0)),
                   pl.BlockSpec((1, t, n_cache), lambda i: (i, 0, 0)),
                   pl.BlockSpec((1, t, n_cache), lambda i: (i, 0, 0))],
        out_shape=[jax.ShapeDtypeStruct((b, t, n_o), BF16),
                   jax.ShapeDtypeStruct((b, t, n_cache), F32),
                   jax.ShapeDtypeStruct((b, t, n_cache), F32)],
        compiler_params=_cparams(("arbitrary",)),
        name="ctx_attention",
    )(qkv, gain_row)


def _rope(x, c, sa, sb):
    return x * c + pltpu.roll(x, LANES - 16, 1) * sa + pltpu.roll(x, 16, 1) * sb


def _lat_a_kernel(q_ref, k_ref, v_ref, ck_ref, cv_ref, gq_ref, gk_ref,
                  cq_ref, saq_ref, sbq_ref, ck_t_ref, sak_t_ref, sbk_t_ref,
                  o_ref, k2_s, v2_s):
    t = k_ref.shape[1]
    ones = _pair_ones()

    @pl.when(pl.program_id(1) == 0)
    def _():
        kn = _head_rms(k_ref[0], gk_ref[...], ones)
        kr = _rope(kn, ck_t_ref[...], sak_t_ref[...], sbk_t_ref[...])
        v = v_ref[0]
        ck = ck_ref[0]
        cv = cv_ref[0]
        for g in range(N_KV_A):
            k2_s[g, 0:t, :] = _dup_half(kr, g).astype(BF16)
            k2_s[g, t:, :] = _dup_half(ck, g).astype(BF16)
            v2_s[g, 0:t, :] = _dup_half(v, g).astype(BF16)
            v2_s[g, t:, :] = _dup_half(cv, g).astype(BF16)

    c, sa, sb = cq_ref[...], saq_ref[...], sbq_ref[...]
    for j in range(4):
        g = j // 2
        qn = _head_rms(q_ref[0, :, j * LANES:(j + 1) * LANES], gq_ref[:, j * LANES:(j + 1) * LANES], ones)
        qb = _rope(qn, c, sa, sb).astype(BF16)
        o = _attend_pair(qb, [k2_s[g]], [v2_s[g]])
        o_ref[0, :, j * LANES:(j + 1) * LANES] = o.astype(o_ref.dtype)


def _rope_tables(t):
    pos = jnp.arange(t, dtype=jnp.int32)
    rc = jnp.stack([pos // GRID_W, pos % GRID_W], axis=-1).astype(F32)
    axis_dim = HEAD_DIM // 2
    inv_freq = 1.0 / (ROPE_THETA ** (jnp.arange(0, axis_dim, 2, dtype=F32) / axis_dim))
    ang = rc[:, :, None] * inv_freq
    cos, sin = jnp.cos(ang), jnp.sin(ang)
    zero = jnp.zeros_like(sin)
    c = jnp.stack([cos, cos], axis=2).reshape(t, HEAD_DIM)
    sa = jnp.stack([-sin, zero], axis=2).reshape(t, HEAD_DIM)
    sb = jnp.stack([zero, sin], axis=2).reshape(t, HEAD_DIM)
    two = lambda a: jnp.concatenate([a, a], axis=-1)
    return two(c), two(sa), two(sb)


def _latent_attention_a(qkv, cache_k, cache_v, gain_row, tq):
    b, t, _ = qkv.shape
    p = cache_k.shape[1]
    c, sa, sb = _rope_tables(t)
    gq = gain_row[:, _QA0 * LANES:(_QA0 + 4) * LANES]
    gk = gain_row[:, _KA * LANES:(_KA + 1) * LANES]
    tab_q = pl.BlockSpec((tq, LANES), lambda i, j: (j, 0))
    tab_k = pl.BlockSpec((t, LANES), lambda i, j: (0, 0))
    return pl.pallas_call(
        _lat_a_kernel,
        grid=(b, t // tq),
        in_specs=[pl.BlockSpec((1, tq, 4 * LANES), lambda i, j: (i, j, 0)),
                  pl.BlockSpec((1, t, LANES), lambda i, j: (i, 0, _KA)),
                  pl.BlockSpec((1, t, LANES), lambda i, j: (i, 0, _VA)),
                  pl.BlockSpec((1, p, LANES), lambda i, j: (i, 0, 0)),
                  pl.BlockSpec((1, p, LANES), lambda i, j: (i, 0, 0)),
                  pl.BlockSpec((1, 4 * LANES), lambda i, j: (0, 0)),
                  pl.BlockSpec((1, LANES), lambda i, j: (0, 0)),
                  tab_q, tab_q, tab_q, tab_k, tab_k, tab_k],
        out_specs=pl.BlockSpec((1, tq, 4 * LANES), lambda i, j: (i, j, 0)),
        out_shape=jax.ShapeDtypeStruct((b, t, 4 * LANES), BF16),
        scratch_shapes=[pltpu.VMEM((N_KV_A, t + p, LANES), BF16), pltpu.VMEM((N_KV_A, t + p, LANES), BF16)],
        compiler_params=_cparams(("arbitrary", "arbitrary")),
        name="latent_attention_a",
    )(qkv, qkv, qkv, cache_k, cache_v, gq, gk, c, sa, sb, c, sa, sb)


def _nbr_bias_table(rel_bias, rows):
    wr = min(WIN_R, rows)
    cols = jnp.arange(GRID_W, dtype=jnp.int32)
    col_start = jnp.clip(cols - WIN_C // 2, 0, GRID_W - WIN_C)
    kc = jnp.arange(GRID_W, dtype=jnp.int32)
    in_win = (kc[None, :] >= col_start[:, None]) & (kc[None, :] < col_start[:, None] + WIN_C)
    col_off = jnp.clip(kc[None, :] - cols[:, None] + (WIN_C - 1), 0, 2 * WIN_C - 2)
    row_off = jnp.arange(2 * wr - 1, dtype=jnp.int32) + (WIN_R - wr)
    base = jnp.where(in_win[None, None], rel_bias[:, row_off][:, :, col_off].astype(F32), NEG_BIG)
    return jnp.concatenate([base[:, :-1], base[:, 1:]], axis=-1)


def _nbr_kernel(q_ref, k_ref, v_ref, ck_ref, cv_ref, gq_ref, gk_ref, bias_ref, o_ref, qn_s, kn_s, vn_s):
    t = q_ref.shape[1]
    rows = t // GRID_W
    wr = min(WIN_R, rows)
    ones = _pair_ones()
    qn_s[...] = _head_rms(q_ref[0], gq_ref[0], ones).astype(BF16)
    kn_s[...] = _head_rms(k_ref[0], gk_ref[0], ones).astype(BF16)
    vn_s[...] = v_ref[0].astype(BF16)
    ck = ck_ref[0].astype(BF16)
    cv = cv_ref[0].astype(BF16)

    rb = math.gcd(rows, NBR_ROW_GROUP)

    def row_group(gi, carry):
        jobs, q0s = [], []
        for rr in range(rb):
            r = gi * rb + rr
            rs = jnp.clip(r - wr // 2, 0, rows - wr)
            var = r - rs
            q0 = pl.multiple_of(r * GRID_W, GRID_W)
            k0 = pl.multiple_of(rs * GRID_W, GRID_W)
            biases = [[jnp.concatenate([bias_ref[hh, wr - 1 - var + 2 * m] for m in range(wr // 2)], axis=-1), None]
                      for hh in range(2)]
            jobs.append((qn_s[pl.ds(q0, GRID_W), :], [kn_s[pl.ds(k0, wr * GRID_W), :], ck],
                         [vn_s[pl.ds(k0, wr * GRID_W), :], cv], biases))
            q0s.append(q0)
        for q0, o in zip(q0s, _attend_pairs(jobs)):
            o_ref[0, pl.ds(q0, GRID_W), :] = o.astype(o_ref.dtype)
        return carry

    lax.fori_loop(0, rows // rb, row_group, 0)


def _latent_attention_b(qkv, cache_k, cache_v, gain_row, rel_bias):
    b, t, _ = qkv.shape
    p = cache_k.shape[1]
    rows = t // GRID_W
    wr = min(WIN_R, rows)
    assert wr % 2 == 0
    n_tab = 2 * wr - 2
    table = _nbr_bias_table(rel_bias, rows).reshape(N_HEADS_B // 2, 2, n_tab, GRID_W, 2 * GRID_W)
    gq = gain_row[:, _QB0 * LANES:(_QB0 + 4) * LANES].reshape(4, 1, LANES)
    gk = gain_row[:, _KB0 * LANES:(_KB0 + 4) * LANES].reshape(4, 1, LANES)
    return pl.pallas_call(
        _nbr_kernel,
        grid=(b, N_HEADS_B // 2),
        in_specs=[pl.BlockSpec((1, t, LANES), lambda i, j: (i, 0, _QB0 + j)),
                  pl.BlockSpec((1, t, LANES), lambda i, j: (i, 0, _KB0 + j)),
                  pl.BlockSpec((1, t, LANES), lambda i, j: (i, 0, _VB0 + j)),
                  pl.BlockSpec((1, p, LANES), lambda i, j: (i, 0, 1 + j)),
                  pl.BlockSpec((1, p, LANES), lambda i, j: (i, 0, 1 + j)),
                  pl.BlockSpec((1, 1, LANES), lambda i, j: (j, 0, 0)),
                  pl.BlockSpec((1, 1, LANES), lambda i, j: (j, 0, 0)),
                  pl.BlockSpec((None, 2, n_tab, GRID_W, 2 * GRID_W), lambda i, j: (j, 0, 0, 0, 0))],
        out_specs=pl.BlockSpec((1, t, LANES), lambda i, j: (i, 0, j)),
        out_shape=jax.ShapeDtypeStruct((b, t, 4 * LANES), BF16),
        scratch_shapes=[pltpu.VMEM((t, LANES), BF16)] * 3,
        compiler_params=_cparams(("arbitrary", "arbitrary")),
        name="latent_attention_b",
    )(qkv, qkv, qkv, cache_k, cache_v, gq, gk, table)


def _delta_kernel(q_ref, k_ref, v_ref, cwq_ref, cwk_ref, cwv_ref, ab_ref, alog_ref, dtb_ref, z_ref, onorm_ref,
                  s0_ref, y_ref, sfin_ref,
                  pad_s, q_s, k_s, v_s, la_s, be_s, u_s, wq_s, kt_s, att_s, el_s, o_s):
    t = q_ref.shape[1]
    c = DELTA_CHUNK
    n = t // c
    h = pl.program_id(1)

    row_t = lax.broadcasted_iota(jnp.int32, (t, 1), 0)
    pad_s[0:8, :] = jnp.zeros((8, LANES), F32)
    pad_s[t + 8:t + 16, :] = jnp.zeros((8, LANES), F32)

    def conv_silu(x_ref, cw_ref):
        pad_s[8:t + 8, :] = x_ref[0]
        cw = cw_ref[...]
        y = pad_s[7:t + 7, :] * cw[0:1] + pad_s[8:t + 8, :] * cw[1:2] + pad_s[9:t + 9, :] * cw[2:3]
        return y * jax.nn.sigmoid(y)

    q = conv_silu(q_ref, cwq_ref)
    q_s[...] = q * lax.rsqrt(jnp.sum(q * q, axis=-1, keepdims=True) + EPS) * (DK_C ** -0.5)
    k = conv_silu(k_ref, cwk_ref)
    k_s[...] = k * lax.rsqrt(jnp.sum(k * k, axis=-1, keepdims=True) + EPS)
    v_s[...] = conv_silu(v_ref, cwv_ref)
    del row_t

    ab = ab_ref[0]
    xa = ab + dtb_ref[...]
    softplus = jnp.maximum(xa, 0.0) + jnp.log1p(jnp.exp(-jnp.abs(xa)))
    la_all = -jnp.exp(alog_ref[...]) * softplus
    be_all = jax.nn.sigmoid(ab)
    lane_t = lax.broadcasted_iota(jnp.int32, ab.shape, 1)
    gates = jnp.where(lane_t < 2 * N_HEADS_C, la_all, be_all)
    pick_r = lax.broadcasted_iota(jnp.int32, (LANES, 4 * LANES), 0)
    pick_c = lax.broadcasted_iota(jnp.int32, (LANES, 4 * LANES), 1)
    pick = jnp.where(pick_r == (pick_c // LANES) * N_HEADS_C + h, 1.0, 0.0).astype(BF16)
    picked = _dot2_rhs01(gates, pick)
    for d in range(2):
        la_s[d] = picked[:, d * LANES:(d + 1) * LANES]
        be_s[d] = picked[:, (2 + d) * LANES:(3 + d) * LANES]

    ii = lax.broadcasted_iota(jnp.int32, (c, c), 0)
    jj = lax.broadcasted_iota(jnp.int32, (c, c), 1)
    eye = jnp.where(ii == jj, 1.0, 0.0)
    ones_cc = jnp.ones((c, c), BF16)
    bd8 = (ii // 8) == (jj // 8)
    offs = []
    blk = 8
    while blk < c:
        offs.append((blk, ((ii // (2 * blk)) == (jj // (2 * blk))) & ((ii // blk) != (jj // blk))))
        blk *= 2

    cg = min(n, DELTA_PREP_GROUP)
    dir_masks = []
    for d in range(2):
        if d == 0:
            incl, strict, incl_t = ii >= jj, ii > jj, ii <= jj
        else:
            incl, strict, incl_t = ii <= jj, ii < jj, ii >= jj
        dir_masks.append((incl, strict, incl_t, jnp.where(incl, 1.0, 0.0).astype(BF16)))

    def prep(gi, carry):
        chains = []
        for cc in range(cg):
            ci = gi * cg + cc
            r0 = pl.multiple_of(ci * c, c)
            qc = q_s[pl.ds(r0, c), :]
            kc = k_s[pl.ds(r0, c), :]
            vc = v_s[pl.ds(r0, c), :]
            qb, kb = qc.astype(BF16), kc.astype(BF16)
            kk = _dot_nt(kb, kb)
            qk = _dot_nt(qb, kb)
            for d in range(2):
                chains.append(dict(ci=ci, r0=r0, d=d, qc=qc, kc=kc, vc=vc, kk=kk, qk=qk,
                                   la=la_s[d, pl.ds(r0, c), :], be=be_s[d, pl.ds(r0, c), :]))
        for ch in chains:
            incl, strict, incl_t, cum_mask = dir_masks[ch["d"]]
            ch["g_col"] = _dot2_lhs01(cum_mask, ch["la"])
            ch["g_row"] = _dot2_lhs01(ones_cc, jnp.where(incl_t, ch["la"], 0.0))
        for ch in chains:
            incl, strict, incl_t, cum_mask = dir_masks[ch["d"]]
            ch["decay"] = jnp.where(incl, jnp.exp(jnp.where(incl, ch["g_col"] - ch["g_row"], 0.0)), 0.0)
            ch["a"] = jnp.where(strict, (ch["kk"] * ch["decay"]) * ch["be"], 0.0)
            ch["p8"] = jnp.where(bd8, -ch["a"], 0.0)
        for ch in chains:
            ch["pk2"] = _dot_t(ch["p8"], ch["p8"])
        for ch in chains:
            ch["acc"] = eye + ch["p8"]
            ch["r"] = _dot_t(ch["pk2"], jnp.concatenate([ch["pk2"], ch["acc"]], axis=1))
        for ch in chains:
            ch["acc"] = ch["acc"] + ch["r"][:, c:]
            ch["t"] = ch["acc"] + _dot_t(ch["r"][:, :c], ch["acc"])
        for blk, off in offs:
            for ch in chains:
                first = ch["d"] == 1
                y = _dot_t(_row_blocks(jnp.where(off, ch["a"], 0.0), blk, first), ch["t"])
                ch["y"] = _row_blocks_put(y, jnp.zeros((c, c), F32), blk, first)
            for ch in chains:
                first = ch["d"] == 1
                t_sel = _row_blocks(ch["t"], blk, first)
                ch["t"] = _row_blocks_put(t_sel - _dot_t(t_sel, ch["y"]), ch["t"], blk, first)
        for ch in chains:
            ch["eg"] = jnp.exp(ch["g_col"])
            rhs = jnp.concatenate([ch["vc"] * ch["be"], ch["kc"] * ch["be"] * ch["eg"]], axis=1)
            ch["uw"] = _dot3(ch["t"], rhs)
        for ch in chains:
            d, r0, ci, g_col = ch["d"], ch["r0"], ch["ci"], ch["g_col"]
            g_last = g_col[c - 1:c, :] if d == 0 else g_col[0:1, :]
            u_s[d, pl.ds(r0, c), :] = ch["uw"][:, :c]
            r2 = pl.multiple_of(ci * 2 * c, 2 * c)
            wq_s[d, pl.ds(r2, c), :] = ch["uw"][:, c:].astype(BF16)
            wq_s[d, pl.ds(r2 + c, c), :] = (ch["qc"] * ch["eg"]).astype(BF16)
            kt_s[d, pl.ds(r0, c), :] = (ch["kc"] * jnp.exp(g_last - g_col)).astype(BF16)
            att_s[d, pl.ds(r0, c), :] = (ch["qk"] * ch["decay"]).astype(BF16)
            r8 = pl.multiple_of(ci * 8, 8)
            el_s[d, pl.ds(r8, 8), :] = jnp.broadcast_to(jnp.exp(g_last), (8, LANES))
        return carry

    lax.fori_loop(0, n // cg, prep, 0)

    def scan_dir(d, ci, s):
        r0 = pl.multiple_of(ci * c, c)
        r2 = pl.multiple_of(ci * 2 * c, 2 * c)
        r8 = pl.multiple_of(ci * 8, 8)
        ws = _dot(wq_s[d, pl.ds(r2, 2 * c), :], s.astype(BF16))
        v_new = u_s[d, pl.ds(r0, c), :] - ws[:c]
        vb = v_new.astype(BF16)
        o = ws[c:] + _dot(att_s[d, pl.ds(r0, c), :], vb)
        s = s * el_s[d, pl.ds(r8, 1), :] + _dot_tn(kt_s[d, pl.ds(r0, c), :], vb)
        return o, s

    def scan(i, carry):
        s_f, s_b = carry
        o_f, s_f = scan_dir(0, i, s_f)
        o_s[0, pl.ds(pl.multiple_of(i * c, c), c), :] = o_f
        ib = n - 1 - i
        o_b, s_b = scan_dir(1, ib, s_b)
        o_s[1, pl.ds(pl.multiple_of(ib * c, c), c), :] = o_b
        return s_f, s_b

    s_f, s_b = lax.fori_loop(0, n, scan, (s0_ref[0, 0, 0], s0_ref[0, 1, 0]))
    sfin_ref[0, 0, 0] = s_f
    sfin_ref[0, 1, 0] = s_b

    o = o_s[0] + o_s[1]
    z = z_ref[0]
    y = _rms_rows(o, onorm_ref[...]) * (z * jax.nn.sigmoid(z))
    y_ref[0] = y.astype(y_ref.dtype)


def _delta_mixer(qkv, z, ab, conv_w, alog_row, dtb_row, out_norm, s0):
    b, t, _ = qkv.shape
    hh = N_HEADS_C
    c = DELTA_CHUNK
    n = t // c
    col = lambda off: pl.BlockSpec((1, t, LANES), lambda i, j, off=off: (i, 0, off + j))
    cw = lambda off: pl.BlockSpec((3, LANES), lambda i, j, off=off: (0, off + j))
    row = pl.BlockSpec((1, LANES), lambda i, j: (0, 0))
    st = pl.BlockSpec((1, 2, 1, DK_C, DV_C), lambda i, j: (i, 0, j, 0, 0))
    return pl.pallas_call(
        _delta_kernel,
        grid=(b, hh),
        in_specs=[col(0), col(hh), col(2 * hh), cw(0), cw(hh), cw(2 * hh),
                  pl.BlockSpec((1, t, LANES), lambda i, j: (i, 0, 0)), row, row,
                  pl.BlockSpec((1, t, LANES), lambda i, j: (i, 0, j)), row, st],
        out_specs=[pl.BlockSpec((1, t, LANES), lambda i, j: (i, 0, j)), st],
        out_shape=[jax.ShapeDtypeStruct((b, t, hh * DV_C), BF16),
                   jax.ShapeDtypeStruct((b, 2, hh, DK_C, DV_C), F32)],
        scratch_shapes=[pltpu.VMEM((t + 16, LANES), F32),
                        pltpu.VMEM((t, LANES), F32), pltpu.VMEM((t, LANES), F32), pltpu.VMEM((t, LANES), F32),
                        pltpu.VMEM((2, t, LANES), F32), pltpu.VMEM((2, t, LANES), F32),
                        pltpu.VMEM((2, t, LANES), F32), pltpu.VMEM((2, 2 * t, LANES), BF16),
                        pltpu.VMEM((2, t, LANES), BF16), pltpu.VMEM((2, t, LANES), BF16),
                        pltpu.VMEM((2, 8 * n, LANES), F32), pltpu.VMEM((2, t, LANES), F32)],
        compiler_params=_cparams(("arbitrary", "arbitrary")),
        name="delta_mixer",
    )(qkv, qkv, qkv, conv_w, conv_w, conv_w, ab, alog_row, dtb_row, z, out_norm.reshape(1, DV_C), s0)


def _pad_lanes(x, n):
    return jnp.pad(x, ((0, 0), (0, n - x.shape[1])))


def kernel(x_prompt, x_sample, c, cache_l0_k, cache_l0_v, state_l1, c_ctx, l0_mod_w, l0_mod_b, l0_norm1, l0_w_in, l0_q_norm_a, l0_k_norm_a, l0_q_norm_b, l0_k_norm_b, l0_rel_bias, l0_w_out, l0_norm2, l0_mlp_w1, l0_mlp_w2, l1_mod_w, l1_mod_b, l1_norm1, l1_w_in, l1_conv_w, l1_a_log, l1_dt_bias, l1_out_norm, l1_w_out, l1_norm2, l1_mlp_w1, l1_mlp_w2):
    bp, tp, d = x_prompt.shape
    bs, ts, _ = x_sample.shape
    n_cache = (N_KV_A + N_HEADS_B) * HEAD_DIM
    bf = lambda w: w.astype(BF16)

    n_rows = -(-(1 + bs) // 8) * 8
    cvec = jnp.concatenate([c_ctx[None, :], c, jnp.zeros((n_rows - 1 - bs, d), F32)], axis=0)
    mods = []
    for mw, mb in ((l0_mod_w, l0_mod_b), (l1_mod_w, l1_mod_b)):
        m = _modulation(cvec, mw, mb).reshape(n_rows, N_MOD, d)
        mods.append((m[0:1], m[1:1 + bs]))

    xp = x_prompt.reshape(1, bp * tp, d)
    xs = x_sample

    scale = HEAD_DIM ** -0.5
    ones_a = jnp.ones((N_KV_A * HEAD_DIM,), F32)
    ones_b = jnp.ones((N_HEADS_B * HEAD_DIM,), F32)
    gain_row = jnp.concatenate([jnp.tile(l0_q_norm_a, N_HEADS_A) * scale, jnp.tile(l0_k_norm_a, N_KV_A), ones_a,
                                jnp.tile(l0_q_norm_b, N_HEADS_B) * scale, jnp.tile(l0_k_norm_b, N_HEADS_B),
                                ones_b])[None, :]
    w_in0 = bf(l0_w_in)
    w_out0 = bf(l0_w_out)
    w1_0, w2_0 = bf(l0_mlp_w1), bf(l0_mlp_w2)
    half = N_HEADS_A * HEAD_DIM

    mod_p, mod_s = mods[0]
    (qkv_p,) = _project(xp, mod_p, l0_norm1, [w_in0], tm=512)
    o_p, new_k, new_v = _ctx_attention(qkv_p.reshape(bp, tp, -1), gain_row)
    xp = _mix_mlp(xp, mod_p, l0_norm2, [o_p.reshape(1, bp * tp, -1)], [w_out0], w1_0, w2_0, tm=512, tf=1024)

    (qkv_s,) = _project(xs, mod_s, l0_norm1, [w_in0], tm=512)
    ck = cache_l0_k.reshape(bs, -1, n_cache)
    cv = cache_l0_v.reshape(bs, -1, n_cache)
    o_a = _latent_attention_a(qkv_s, ck, cv, gain_row, tq=256)
    o_b = _latent_attention_b(qkv_s, ck, cv, gain_row, l0_rel_bias)
    xs = _mix_mlp(xs, mod_s, l0_norm2, [o_a, o_b], [w_out0[:half], w_out0[half:]], w1_0, w2_0, tm=512, tf=1024)

    n_qkv = N_HEADS_C * (2 * DK_C + DV_C)
    n_z = N_HEADS_C * DV_C
    w_in1 = bf(l1_w_in)
    w_pieces = [w_in1[:, :n_qkv], w_in1[:, n_qkv:n_qkv + n_z], _pad_lanes(w_in1[:, n_qkv + n_z:], LANES)]
    w_out1 = bf(l1_w_out)
    w1_1, w2_1 = bf(l1_mlp_w1), bf(l1_mlp_w2)
    alog_row = _pad_lanes(l1_a_log.reshape(1, -1), LANES)
    dtb_row = _pad_lanes(l1_dt_bias.reshape(1, -1), LANES)

    mod_p, mod_s = mods[1]
    qkv1_p, z_p, ab_p = _project(xp, mod_p, l1_norm1, w_pieces, tm=256)
    s0_p = jnp.zeros((bp, 2, N_HEADS_C, DK_C, DV_C), F32)
    y_p, new_s = _delta_mixer(qkv1_p.reshape(bp, tp, -1), z_p.reshape(bp, tp, -1), ab_p.reshape(bp, tp, -1),
                              l1_conv_w, alog_row, dtb_row, l1_out_norm, s0_p)
    xp = _mix_mlp(xp, mod_p, l1_norm2, [y_p.reshape(1, bp * tp, -1)], [w_out1], w1_1, w2_1, tm=512, tf=1024)

    qkv1_s, z_s, ab_s = _project(xs, mod_s, l1_norm1, w_pieces, tm=256)
    y_s, _ = _delta_mixer(qkv1_s, z_s, ab_s, l1_conv_w, alog_row, dtb_row, l1_out_norm, state_l1.astype(F32))
    xs = _mix_mlp(xs, mod_s, l1_norm2, [y_s], [w_out1], w1_1, w2_1, tm=512, tf=1024)

    return (xp.reshape(bp, tp, d), xs,
            new_k.reshape(bp, tp, N_KV_A + N_HEADS_B, HEAD_DIM), new_v.reshape(bp, tp, N_KV_A + N_HEADS_B, HEAD_DIM),
            new_s.astype(x_prompt.dtype))
```

```python
import functools
import math

import jax
import jax.numpy as jnp
from jax import lax
from jax.experimental import pallas as pl
from jax.experimental.pallas import tpu as pltpu

F32 = jnp.float32
BF16 = jnp.bfloat16

EPS = 1e-6
N_MOD = 6
HEAD_DIM = 64
N_HEADS_A = 8
N_KV_A = 2
N_HEADS_B = 8
GRID_W = 64
WIN_R = 8
WIN_C = 16
ROPE_THETA = 10000.0
N_HEADS_C = 8
DK_C = 128
DV_C = 128
LANES = 128
DELTA_CHUNK = 128
DELTA_PREP_GROUP = 4
DELTA_PREP_CHAINS = 16
DELTA_ROW_BLOCK = 256
DELTA_HEADS_PROMPT = 8
DELTA_HEADS_SAMPLE = 2
NBR_ROW_GROUP = 8
LAT_A_JOBS = 2
DELTA_VMEM_LIMIT = 56 * 1024 * 1024
NEG_BIG = -1e30
LOG2_E = math.log2(math.e)

VMEM_LIMIT = 48 * 1024 * 1024


def _cparams(sem):
    return pltpu.CompilerParams(dimension_semantics=sem, vmem_limit_bytes=VMEM_LIMIT)


def _dot(a, b):
    return jnp.dot(a, b, preferred_element_type=F32)


def _dot_nt(a, b):
    return lax.dot_general(a, b, (((1,), (1,)), ((), ())), preferred_element_type=F32)


def _dot_tn(a, b):
    return lax.dot_general(a, b, (((0,), (0,)), ((), ())), preferred_element_type=F32)


def _split2(x):
    hi = x.astype(BF16)
    lo = (x - hi.astype(F32)).astype(BF16)
    return hi, lo


def _split3(x):
    p1 = x.astype(BF16)
    r1 = x - p1.astype(F32)
    p2 = r1.astype(BF16)
    p3 = (r1 - p2.astype(F32)).astype(BF16)
    return p1, p2, p3


def _dot3(a, b):
    ah, al = _split2(a)
    bh, bl = _split2(b)
    lhs = jnp.concatenate([ah, ah, al], axis=1)
    rhs = jnp.concatenate([bh, bl, bh], axis=0)
    return _dot(lhs, rhs)


def _dot_t(a, b):
    ah, al = _split2(a)
    bh = b.astype(BF16)
    return _dot(jnp.concatenate([ah, al], axis=1), jnp.concatenate([bh, bh], axis=0))


def _dot2_rhs01(a_f32, b01):
    hi, lo = _split2(a_f32)
    return _dot(jnp.concatenate([hi, lo], axis=1), jnp.concatenate([b01, b01], axis=0))


def _dot2_lhs01(a01, b_f32):
    hi, lo = _split2(b_f32)
    return _dot(jnp.concatenate([a01, a01], axis=1), jnp.concatenate([hi, lo], axis=0))


def _row_blocks(x, blk, first):
    return jnp.concatenate([x[i * blk:(i + 1) * blk] for i in range(0 if first else 1, x.shape[0] // blk, 2)], axis=0)


def _row_blocks_put(sel, rest, blk, first):
    pieces = []
    for i in range(rest.shape[0] // blk):
        if (i % 2 == 0) == first:
            pieces.append(sel[(i // 2) * blk:(i // 2 + 1) * blk])
        else:
            pieces.append(rest[i * blk:(i + 1) * blk])
    return jnp.concatenate(pieces, axis=0)


def _rms_rows(x, gain):
    ms = jnp.mean(x * x, axis=-1, keepdims=True)
    return x * lax.rsqrt(ms + EPS) * gain


def _mod_kernel(c_ref, w_ref, b_ref, o_ref):
    c = c_ref[...]
    a = (c * jax.nn.sigmoid(c)).astype(BF16)
    o_ref[...] = _dot(a, w_ref[...].astype(BF16)) + b_ref[...]


def _modulation(cvec, w, b):
    rows, d = cvec.shape
    n = w.shape[1]
    tn = 1536
    return pl.pallas_call(
        _mod_kernel,
        grid=(n // tn,),
        in_specs=[pl.BlockSpec((rows, d), lambda j: (0, 0)),
                  pl.BlockSpec((d, tn), lambda j: (0, j)),
                  pl.BlockSpec((1, tn), lambda j: (0, j))],
        out_specs=pl.BlockSpec((rows, tn), lambda j: (0, j)),
        out_shape=jax.ShapeDtypeStruct((rows, n), F32),
        compiler_params=_cparams(("arbitrary",)),
        name="modulation",
    )(cvec, w, b.reshape(1, n))


def _proj_kernel(x_ref, mod_ref, g_ref, *rest, n_out):
    w_refs, o_refs = rest[:n_out], rest[n_out:]
    mod = mod_ref[0]
    h = _rms_rows(x_ref[0], g_ref[...]) * (1.0 + mod[1:2]) + mod[0:1]
    hb = h.astype(BF16)
    for w_ref, o_ref in zip(w_refs, o_refs):
        o_ref[0] = _dot(hb, w_ref[...]).astype(o_ref.dtype)


def _project(x, mod, gain, weights, tm):
    g, t, d = x.shape
    n_out = len(weights)
    in_specs = [pl.BlockSpec((1, tm, d), lambda i, j: (i, j, 0)),
                pl.BlockSpec((1, N_MOD, d), lambda i, j: (i, 0, 0)),
                pl.BlockSpec((1, d), lambda i, j: (0, 0))]
    in_specs += [pl.BlockSpec(w.shape, lambda i, j: (0, 0)) for w in weights]
    out_specs = [pl.BlockSpec((1, tm, w.shape[1]), lambda i, j: (i, j, 0)) for w in weights]
    out_shape = [jax.ShapeDtypeStruct((g, t, w.shape[1]), F32) for w in weights]
    return pl.pallas_call(
        functools.partial(_proj_kernel, n_out=n_out),
        grid=(g, t // tm),
        in_specs=in_specs, out_specs=out_specs, out_shape=out_shape,
        compiler_params=_cparams(("arbitrary", "arbitrary")),
        name="norm_project",
    )(x, mod, gain.reshape(1, d), *weights)


def _mlp_kernel(*refs, n_mix):
    x_ref, mod_ref, g_ref = refs[:3]
    o_refs = refs[3:3 + n_mix]
    wo_refs = refs[3 + n_mix:3 + 2 * n_mix]
    w1_ref, w2_ref, out_ref, x1_s, h_s, acc_s = refs[3 + 2 * n_mix:]
    f = pl.program_id(2)

    @pl.when(f == 0)
    def _():
        mod = mod_ref[0]
        mp = _dot(o_refs[0][0].astype(BF16), wo_refs[0][...])
        for o_ref, wo_ref in zip(o_refs[1:], wo_refs[1:]):
            mp += _dot(o_ref[0].astype(BF16), wo_ref[...])
        x1 = x_ref[0] + mod[2:3] * mp
        x1_s[...] = x1
        h = _rms_rows(x1, g_ref[...]) * (1.0 + mod[4:5]) + mod[3:4]
        h_s[...] = h.astype(BF16)
        acc_s[...] = jnp.zeros_like(acc_s)

    a = _dot(h_s[...], w1_ref[...])
    a = jnp.square(jnp.maximum(a, 0.0))
    acc_s[...] += _dot(a.astype(BF16), w2_ref[...])

    @pl.when(f == pl.num_programs(2) - 1)
    def _():
        out_ref[0] = x1_s[...] + mod_ref[0][5:6] * acc_s[...]


def _mix_mlp(x, mod, gain2, mixes, w_outs, w1, w2, tm, tf):
    g, t, d = x.shape
    dff = w1.shape[1]
    n_mix = len(mixes)
    in_specs = [pl.BlockSpec((1, tm, d), lambda i, j, f: (i, j, 0)),
                pl.BlockSpec((1, N_MOD, d), lambda i, j, f: (i, 0, 0)),
                pl.BlockSpec((1, d), lambda i, j, f: (0, 0))]
    in_specs += [pl.BlockSpec((1, tm, o.shape[2]), lambda i, j, f: (i, j, 0)) for o in mixes]
    in_specs += [pl.BlockSpec(w.shape, lambda i, j, f: (0, 0)) for w in w_outs]
    in_specs += [pl.BlockSpec((d, tf), lambda i, j, f: (0, f)),
                 pl.BlockSpec((tf, d), lambda i, j, f: (f, 0))]
    return pl.pallas_call(
        functools.partial(_mlp_kernel, n_mix=n_mix),
        grid=(g, t // tm, dff // tf),
        in_specs=in_specs,
        out_specs=pl.BlockSpec((1, tm, d), lambda i, j, f: (i, j, 0)),
        out_shape=jax.ShapeDtypeStruct((g, t, d), F32),
        scratch_shapes=[pltpu.VMEM((tm, d), F32), pltpu.VMEM((tm, d), BF16), pltpu.VMEM((tm, d), F32)],
        compiler_params=_cparams(("arbitrary", "arbitrary", "arbitrary")),
        name="mix_mlp",
    )(x, mod, gain2.reshape(1, d), *mixes, *w_outs, w1, w2)


def _pair_ones():
    r = lax.broadcasted_iota(jnp.int32, (LANES, LANES), 0) // HEAD_DIM
    c = lax.broadcasted_iota(jnp.int32, (LANES, LANES), 1) // HEAD_DIM
    return jnp.where(r == c, 1.0, 0.0).astype(BF16)


def _head_rms(xb, gain, pair_ones):
    hi, lo = _split2(xb * xb)
    ss = _dot(jnp.concatenate([hi, lo], axis=1), jnp.concatenate([pair_ones, pair_ones], axis=0))
    return xb * lax.rsqrt(ss * (1.0 / HEAD_DIM) + EPS) * gain


def _dup_half(x, g):
    lane = lax.broadcasted_iota(jnp.int32, x.shape, 1)
    sw = pltpu.roll(x, HEAD_DIM, 1)
    if g == 0:
        return jnp.where(lane < HEAD_DIM, x, sw)
    return jnp.where(lane < HEAD_DIM, sw, x)


def _softmax_parts(scores):
    m = jnp.max(scores[0], axis=-1, keepdims=True)
    for s in scores[1:]:
        m = jnp.maximum(m, jnp.max(s, axis=-1, keepdims=True))
    ps = [jnp.exp2(s - m) for s in scores]
    l = jnp.sum(ps[0], axis=-1, keepdims=True)
    for p in ps[1:]:
        l += jnp.sum(p, axis=-1, keepdims=True)
    return ps, 1.0 / l


def _attend_pairs(jobs):
    tasks = []
    for qb, ks, vs, biases in jobs:
        lane = lax.broadcasted_iota(jnp.int32, qb.shape, 1)
        for hh in range(2):
            sel = (lane < HEAD_DIM) if hh == 0 else (lane >= HEAD_DIM)
            tasks.append(dict(q=jnp.where(sel, qb, jnp.zeros_like(qb)), ks=ks, vs=vs,
                              bias=None if biases is None else biases[hh]))
    for tk in tasks:
        scores = [_dot_nt(tk["q"], k) for k in tk["ks"]]
        if tk["bias"] is not None:
            scores = [s if b is None else s + b for s, b in zip(scores, tk["bias"])]
        tk["s"] = scores
    for tk in tasks:
        tk["p"], tk["inv"] = _softmax_parts(tk["s"])
    for tk in tasks:
        o = _dot(tk["p"][0].astype(BF16), tk["vs"][0])
        for p, v in zip(tk["p"][1:], tk["vs"][1:]):
            o += _dot(p.astype(BF16), v)
        tk["o"] = o * tk["inv"]
    outs = []
    for j in range(len(jobs)):
        o0, o1 = tasks[2 * j]["o"], tasks[2 * j + 1]["o"]
        lane_o = lax.broadcasted_iota(jnp.int32, o0.shape, 1)
        outs.append(jnp.where(lane_o < HEAD_DIM, o0, o1))
    return outs


def _attend_pair(qb, ks, vs, biases=None):
    return _attend_pairs([(qb, ks, vs, biases)])[0]


_QA0, _KA, _VA, _QB0, _KB0, _VB0 = 0, 4, 5, 6, 10, 14
_N_QKV_BLOCKS = 18


def _ctx_attn_kernel(qkv_ref, gain_ref, o_ref, nk_ref, nv_ref):
    ones = _pair_ones()

    def blk(j):
        return qkv_ref[0, :, j * LANES:(j + 1) * LANES]

    def gain(j):
        return gain_ref[:, j * LANES:(j + 1) * LANES]

    ka = _head_rms(blk(_KA), gain(_KA), ones)
    va = blk(_VA)
    nk_ref[0, :, 0:LANES] = ka
    nv_ref[0, :, 0:LANES] = va
    k2 = [_dup_half(ka, g).astype(BF16) for g in range(N_KV_A)]
    v2 = [_dup_half(va, g).astype(BF16) for g in range(N_KV_A)]
    jobs = []
    for j in range(4):
        qn = _head_rms(blk(_QA0 + j), gain(_QA0 + j), ones).astype(BF16)
        jobs.append((qn, [k2[j // 2]], [v2[j // 2]], None))
    for j, o in enumerate(_attend_pairs(jobs)):
        o_ref[0, :, j * LANES:(j + 1) * LANES] = o.astype(o_ref.dtype)
    jobs = []
    for j in range(4):
        kb = _head_rms(blk(_KB0 + j), gain(_KB0 + j), ones)
        vb = blk(_VB0 + j)
        nk_ref[0, :, (1 + j) * LANES:(2 + j) * LANES] = kb
        nv_ref[0, :, (1 + j) * LANES:(2 + j) * LANES] = vb
        qn = _head_rms(blk(_QB0 + j), gain(_QB0 + j), ones).astype(BF16)
        jobs.append((qn, [kb.astype(BF16)], [vb.astype(BF16)], None))
    for j, o in enumerate(_attend_pairs(jobs)):
        o_ref[0, :, (4 + j) * LANES:(5 + j) * LANES] = o.astype(o_ref.dtype)


def _ctx_attention(qkv, gain_row):
    b, t, n = qkv.shape
    n_cache = (N_KV_A + N_HEADS_B) * HEAD_DIM
    n_o = (N_HEADS_A + N_HEADS_B) * HEAD_DIM
    return pl.pallas_call(
        _ctx_attn_kernel,
        grid=(b,),
        in_specs=[pl.BlockSpec((1, t, n), lambda i: (i, 0, 0)),
                  pl.BlockSpec((1, n), lambda i: (0, 0))],
        out_specs=[pl.BlockSpec((1, t, n_o), lambda i: (i, 0, 0)),
                   pl.BlockSpec((1, t, n_cache), lambda i: (i, 0, 0)),
                   pl.BlockSpec((1, t, n_cache), lambda i: (i, 0, 0))],
        out_shape=[jax.ShapeDtypeStruct((b, t, n_o), BF16),
                   jax.ShapeDtypeStruct((b, t, n_cache), F32),
                   jax.ShapeDtypeStruct((b, t, n_cache), F32)],
        compiler_params=_cparams(("arbitrary",)),
        name="ctx_attention",
    )(qkv, gain_row)


def _rope(x, c, sa, sb):
    return x * c + pltpu.roll(x, LANES - 16, 1) * sa + pltpu.roll(x, 16, 1) * sb


def _lat_a_kernel(q_ref, k_ref, v_ref, ck_ref, cv_ref, gq_ref, gk_ref,
                  cq_ref, saq_ref, sbq_ref, ck_t_ref, sak_t_ref, sbk_t_ref,
                  o_ref, k2_s, v2_s):
    t = k_ref.shape[1]
    ones = _pair_ones()

    @pl.when(pl.program_id(1) == 0)
    def _():
        kn = _head_rms(k_ref[0], gk_ref[...], ones)
        kr = _rope(kn, ck_t_ref[...], sak_t_ref[...], sbk_t_ref[...])
        v = v_ref[0]
        ck = ck_ref[0]
        cv = cv_ref[0]
        for g in range(N_KV_A):
            k2_s[g, 0:t, :] = _dup_half(kr, g).astype(BF16)
            k2_s[g, t:, :] = _dup_half(ck, g).astype(BF16)
            v2_s[g, 0:t, :] = _dup_half(v, g).astype(BF16)
            v2_s[g, t:, :] = _dup_half(cv, g).astype(BF16)

    c, sa, sb = cq_ref[...], saq_ref[...], sbq_ref[...]
    for j0 in range(0, 4, LAT_A_JOBS):
        jobs = []
        for j in range(j0, j0 + LAT_A_JOBS):
            qn = _head_rms(q_ref[0, :, j * LANES:(j + 1) * LANES], gq_ref[:, j * LANES:(j + 1) * LANES], ones)
            jobs.append((_rope(qn, c, sa, sb).astype(BF16), [k2_s[j // 2]], [v2_s[j // 2]], None))
        for j, o in zip(range(j0, j0 + LAT_A_JOBS), _attend_pairs(jobs)):
            o_ref[0, :, j * LANES:(j + 1) * LANES] = o.astype(o_ref.dtype)


def _rope_tables(t):
    pos = jnp.arange(t, dtype=jnp.int32)
    rc = jnp.stack([pos // GRID_W, pos % GRID_W], axis=-1).astype(F32)
    axis_dim = HEAD_DIM // 2
    inv_freq = 1.0 / (ROPE_THETA ** (jnp.arange(0, axis_dim, 2, dtype=F32) / axis_dim))
    ang = rc[:, :, None] * inv_freq
    cos, sin = jnp.cos(ang), jnp.sin(ang)
    zero = jnp.zeros_like(sin)
    c = jnp.stack([cos, cos], axis=2).reshape(t, HEAD_DIM)
    sa = jnp.stack([-sin, zero], axis=2).reshape(t, HEAD_DIM)
    sb = jnp.stack([zero, sin], axis=2).reshape(t, HEAD_DIM)
    two = lambda a: jnp.concatenate([a, a], axis=-1)
    return two(c), two(sa), two(sb)


def _latent_attention_a(qkv, cache_k, cache_v, gain_row, tq):
    b, t, _ = qkv.shape
    p = cache_k.shape[1]
    c, sa, sb = _rope_tables(t)
    gq = gain_row[:, _QA0 * LANES:(_QA0 + 4) * LANES]
    gk = gain_row[:, _KA * LANES:(_KA + 1) * LANES]
    tab_q = pl.BlockSpec((tq, LANES), lambda i, j: (j, 0))
    tab_k = pl.BlockSpec((t, LANES), lambda i, j: (0, 0))
    return pl.pallas_call(
        _lat_a_kernel,
        grid=(b, t // tq),
        in_specs=[pl.BlockSpec((1, tq, 4 * LANES), lambda i, j: (i, j, 0)),
                  pl.BlockSpec((1, t, LANES), lambda i, j: (i, 0, _KA)),
                  pl.BlockSpec((1, t, LANES), lambda i, j: (i, 0, _VA)),
                  pl.BlockSpec((1, p, LANES), lambda i, j: (i, 0, 0)),
                  pl.BlockSpec((1, p, LANES), lambda i, j: (i, 0, 0)),
                  pl.BlockSpec((1, 4 * LANES), lambda i, j: (0, 0)),
                  pl.BlockSpec((1, LANES), lambda i, j: (0, 0)),
                  tab_q, tab_q, tab_q, tab_k, tab_k, tab_k],
        out_specs=pl.BlockSpec((1, tq, 4 * LANES), lambda i, j: (i, j, 0)),
        out_shape=jax.ShapeDtypeStruct((b, t, 4 * LANES), BF16),
        scratch_shapes=[pltpu.VMEM((N_KV_A, t + p, LANES), BF16), pltpu.VMEM((N_KV_A, t + p, LANES), BF16)],
        compiler_params=_cparams(("arbitrary", "arbitrary")),
        name="latent_attention_a",
    )(qkv, qkv, qkv, cache_k, cache_v, gq, gk, c, sa, sb, c, sa, sb)


def _nbr_bias_table(rel_bias, rows):
    wr = min(WIN_R, rows)
    cols = jnp.arange(GRID_W, dtype=jnp.int32)
    col_start = jnp.clip(cols - WIN_C // 2, 0, GRID_W - WIN_C)
    kc = jnp.arange(GRID_W, dtype=jnp.int32)
    in_win = (kc[None, :] >= col_start[:, None]) & (kc[None, :] < col_start[:, None] + WIN_C)
    col_off = jnp.clip(kc[None, :] - cols[:, None] + (WIN_C - 1), 0, 2 * WIN_C - 2)
    row_off = jnp.arange(2 * wr - 1, dtype=jnp.int32) + (WIN_R - wr)
    base = jnp.where(in_win[None, None], rel_bias[:, row_off][:, :, col_off].astype(F32) * LOG2_E,
                     NEG_BIG)
    return jnp.concatenate([base[:, :-1], base[:, 1:]], axis=-1)


def _nbr_kernel(q_ref, k_ref, v_ref, ck_ref, cv_ref, gq_ref, gk_ref, bias_ref, o_ref, qn_s, kn_s, vn_s):
    t = q_ref.shape[1]
    rows = t // GRID_W
    wr = min(WIN_R, rows)
    ones = _pair_ones()
    qn_s[...] = _head_rms(q_ref[0], gq_ref[0], ones).astype(BF16)
    kn_s[...] = _head_rms(k_ref[0], gk_ref[0], ones).astype(BF16)
    vn_s[...] = v_ref[0].astype(BF16)
    ck = ck_ref[0].astype(BF16)
    cv = cv_ref[0].astype(BF16)

    rb = math.gcd(rows, NBR_ROW_GROUP)

    def row_group(gi, carry):
        jobs, q0s = [], []
        for rr in range(rb):
            r = gi * rb + rr
            rs = jnp.clip(r - wr // 2, 0, rows - wr)
            var = r - rs
            q0 = pl.multiple_of(r * GRID_W, GRID_W)
            k0 = pl.multiple_of(rs * GRID_W, GRID_W)
            biases = [[jnp.concatenate([bias_ref[hh, wr - 1 - var + 2 * m] for m in range(wr // 2)], axis=-1), None]
                      for hh in range(2)]
            jobs.append((qn_s[pl.ds(q0, GRID_W), :], [kn_s[pl.ds(k0, wr * GRID_W), :], ck],
                         [vn_s[pl.ds(k0, wr * GRID_W), :], cv], biases))
            q0s.append(q0)
        for q0, o in zip(q0s, _attend_pairs(jobs)):
            o_ref[0, pl.ds(q0, GRID_W), :] = o.astype(o_ref.dtype)
        return carry

    lax.fori_loop(0, rows // rb, row_group, 0)


def _latent_attention_b(qkv, cache_k, cache_v, gain_row, rel_bias):
    b, t, _ = qkv.shape
    p = cache_k.shape[1]
    rows = t // GRID_W
    wr = min(WIN_R, rows)
    assert wr % 2 == 0
    n_tab = 2 * wr - 2
    table = _nbr_bias_table(rel_bias, rows).reshape(N_HEADS_B // 2, 2, n_tab, GRID_W, 2 * GRID_W)
    gq = gain_row[:, _QB0 * LANES:(_QB0 + 4) * LANES].reshape(4, 1, LANES)
    gk = gain_row[:, _KB0 * LANES:(_KB0 + 4) * LANES].reshape(4, 1, LANES)
    return pl.pallas_call(
        _nbr_kernel,
        grid=(b, N_HEADS_B // 2),
        in_specs=[pl.BlockSpec((1, t, LANES), lambda i, j: (i, 0, _QB0 + j)),
                  pl.BlockSpec((1, t, LANES), lambda i, j: (i, 0, _KB0 + j)),
                  pl.BlockSpec((1, t, LANES), lambda i, j: (i, 0, _VB0 + j)),
                  pl.BlockSpec((1, p, LANES), lambda i, j: (i, 0, 1 + j)),
                  pl.BlockSpec((1, p, LANES), lambda i, j: (i, 0, 1 + j)),
                  pl.BlockSpec((1, 1, LANES), lambda i, j: (j, 0, 0)),
                  pl.BlockSpec((1, 1, LANES), lambda i, j: (j, 0, 0)),
                  pl.BlockSpec((None, 2, n_tab, GRID_W, 2 * GRID_W), lambda i, j: (j, 0, 0, 0, 0))],
        out_specs=pl.BlockSpec((1, t, LANES), lambda i, j: (i, 0, j)),
        out_shape=jax.ShapeDtypeStruct((b, t, 4 * LANES), BF16),
        scratch_shapes=[pltpu.VMEM((t, LANES), BF16)] * 3,
        compiler_params=_cparams(("arbitrary", "arbitrary")),
        name="latent_attention_b",
    )(qkv, qkv, qkv, cache_k, cache_v, gq, gk, table)


def _delta_kernel_v2(q_ref, k_ref, v_ref, cwq_ref, cwk_ref, cwv_ref, ab_ref, alog_ref, dtb_ref, z_ref, onorm_ref,
                  s0_ref, y_ref, sfin_ref,
                  pad_s, q_s, k_s, v_s, la_s, be_s, u_s, wq_s, kt_s, att_s, el_s, o_s):
    t = q_ref.shape[1]
    c = DELTA_CHUNK
    n = t // c
    h = pl.program_id(1)

    row_t = lax.broadcasted_iota(jnp.int32, (t, 1), 0)
    pad_s[0:8, :] = jnp.zeros((8, LANES), F32)
    pad_s[t + 8:t + 16, :] = jnp.zeros((8, LANES), F32)

    def conv_silu(x_ref, cw_ref):
        pad_s[8:t + 8, :] = x_ref[0]
        cw = cw_ref[...]
        y = pad_s[7:t + 7, :] * cw[0:1] + pad_s[8:t + 8, :] * cw[1:2] + pad_s[9:t + 9, :] * cw[2:3]
        return y * jax.nn.sigmoid(y)

    q = conv_silu(q_ref, cwq_ref)
    q_s[...] = q * lax.rsqrt(jnp.sum(q * q, axis=-1, keepdims=True) + EPS) * (DK_C ** -0.5)
    k = conv_silu(k_ref, cwk_ref)
    k_s[...] = k * lax.rsqrt(jnp.sum(k * k, axis=-1, keepdims=True) + EPS)
    v_s[...] = conv_silu(v_ref, cwv_ref)
    del row_t

    ab = ab_ref[0]
    xa = ab + dtb_ref[...]
    softplus = jnp.maximum(xa, 0.0) + jnp.log1p(jnp.exp(-jnp.abs(xa)))
    la_all = -jnp.exp(alog_ref[...]) * softplus
    be_all = jax.nn.sigmoid(ab)
    lane_t = lax.broadcasted_iota(jnp.int32, ab.shape, 1)
    gates = jnp.where(lane_t < 2 * N_HEADS_C, la_all, be_all)
    pick_r = lax.broadcasted_iota(jnp.int32, (LANES, 4 * LANES), 0)
    pick_c = lax.broadcasted_iota(jnp.int32, (LANES, 4 * LANES), 1)
    pick = jnp.where(pick_r == (pick_c // LANES) * N_HEADS_C + h, 1.0, 0.0).astype(BF16)
    picked = _dot2_rhs01(gates, pick)
    for d in range(2):
        la_s[d] = picked[:, d * LANES:(d + 1) * LANES]
        be_s[d] = picked[:, (2 + d) * LANES:(3 + d) * LANES]

    ii = lax.broadcasted_iota(jnp.int32, (c, c), 0)
    jj = lax.broadcasted_iota(jnp.int32, (c, c), 1)
    eye = jnp.where(ii == jj, 1.0, 0.0)
    ones_cc = jnp.ones((c, c), BF16)
    bd8 = (ii // 8) == (jj // 8)
    offs = []
    blk = 8
    while blk < c:
        offs.append((blk, ((ii // (2 * blk)) == (jj // (2 * blk))) & ((ii // blk) != (jj // blk))))
        blk *= 2

    cg = min(n, DELTA_PREP_GROUP)
    dir_masks = []
    for d in range(2):
        if d == 0:
            incl, strict, incl_t = ii >= jj, ii > jj, ii <= jj
        else:
            incl, strict, incl_t = ii <= jj, ii < jj, ii >= jj
        dir_masks.append((incl, strict, incl_t, jnp.where(incl, 1.0, 0.0).astype(BF16)))

    def prep(gi, carry):
        chains = []
        for cc in range(cg):
            ci = gi * cg + cc
            r0 = pl.multiple_of(ci * c, c)
            qc = q_s[pl.ds(r0, c), :]
            kc = k_s[pl.ds(r0, c), :]
            vc = v_s[pl.ds(r0, c), :]
            qb, kb = qc.astype(BF16), kc.astype(BF16)
            kk = _dot_nt(kb, kb)
            qk = _dot_nt(qb, kb)
            for d in range(2):
                chains.append(dict(ci=ci, r0=r0, d=d, qc=qc, kc=kc, vc=vc, kk=kk, qk=qk,
                                   la=la_s[d, pl.ds(r0, c), :], be=be_s[d, pl.ds(r0, c), :]))
        for ch in chains:
            incl, strict, incl_t, cum_mask = dir_masks[ch["d"]]
            ch["g_col"] = _dot2_lhs01(cum_mask, ch["la"])
            ch["g_row"] = _dot2_lhs01(ones_cc, jnp.where(incl_t, ch["la"], 0.0))
        for ch in chains:
            incl, strict, incl_t, cum_mask = dir_masks[ch["d"]]
            ch["decay"] = jnp.where(incl, jnp.exp(jnp.where(incl, ch["g_col"] - ch["g_row"], 0.0)), 0.0)
            ch["a"] = jnp.where(strict, (ch["kk"] * ch["decay"]) * ch["be"], 0.0)
            ch["p8"] = jnp.where(bd8, -ch["a"], 0.0)
        for ch in chains:
            ch["pk2"] = _dot_t(ch["p8"], ch["p8"])
        for ch in chains:
            ch["acc"] = eye + ch["p8"]
            ch["r"] = _dot_t(ch["pk2"], jnp.concatenate([ch["pk2"], ch["acc"]], axis=1))
        for ch in chains:
            ch["acc"] = ch["acc"] + ch["r"][:, c:]
            ch["t"] = ch["acc"] + _dot_t(ch["r"][:, :c], ch["acc"])
        for blk, off in offs:
            for ch in chains:
                first = ch["d"] == 1
                y = _dot_t(_row_blocks(jnp.where(off, ch["a"], 0.0), blk, first), ch["t"])
                ch["y"] = _row_blocks_put(y, jnp.zeros((c, c), F32), blk, first)
            for ch in chains:
                first = ch["d"] == 1
                t_sel = _row_blocks(ch["t"], blk, first)
                ch["t"] = _row_blocks_put(t_sel - _dot_t(t_sel, ch["y"]), ch["t"], blk, first)
        for ch in chains:
            ch["eg"] = jnp.exp(ch["g_col"])
            rhs = jnp.concatenate([ch["vc"] * ch["be"], ch["kc"] * ch["be"] * ch["eg"]], axis=1)
            ch["uw"] = _dot3(ch["t"], rhs)
        for ch in chains:
            d, r0, ci, g_col = ch["d"], ch["r0"], ch["ci"], ch["g_col"]
            g_last = g_col[c - 1:c, :] if d == 0 else g_col[0:1, :]
            u_s[d, pl.ds(r0, c), :] = ch["uw"][:, :c]
            r2 = pl.multiple_of(ci * 2 * c, 2 * c)
            wq_s[d, pl.ds(r2, c), :] = ch["uw"][:, c:].astype(BF16)
            wq_s[d, pl.ds(r2 + c, c), :] = (ch["qc"] * ch["eg"]).astype(BF16)
            kt_s[d, pl.ds(r0, c), :] = (ch["kc"] * jnp.exp(g_last - g_col)).astype(BF16)
            att_s[d, pl.ds(r0, c), :] = (ch["qk"] * ch["decay"]).astype(BF16)
            r8 = pl.multiple_of(ci * 8, 8)
            el_s[d, pl.ds(r8, 8), :] = jnp.broadcast_to(jnp.exp(g_last), (8, LANES))
        return carry

    lax.fori_loop(0, n // cg, prep, 0)

    def scan_dir(d, ci, s):
        r0 = pl.multiple_of(ci * c, c)
        r2 = pl.multiple_of(ci * 2 * c, 2 * c)
        r8 = pl.multiple_of(ci * 8, 8)
        ws = _dot(wq_s[d, pl.ds(r2, 2 * c), :], s.astype(BF16))
        v_new = u_s[d, pl.ds(r0, c), :] - ws[:c]
        vb = v_new.astype(BF16)
        o = ws[c:] + _dot(att_s[d, pl.ds(r0, c), :], vb)
        s = s * el_s[d, pl.ds(r8, 1), :] + _dot_tn(kt_s[d, pl.ds(r0, c), :], vb)
        return o, s

    def scan(i, carry):
        s_f, s_b = carry
        o_f, s_f = scan_dir(0, i, s_f)
        o_s[0, pl.ds(pl.multiple_of(i * c, c), c), :] = o_f
        ib = n - 1 - i
        o_b, s_b = scan_dir(1, ib, s_b)
        o_s[1, pl.ds(pl.multiple_of(ib * c, c), c), :] = o_b
        return s_f, s_b

    s_f, s_b = lax.fori_loop(0, n, scan, (s0_ref[0, 0, 0], s0_ref[0, 1, 0]))
    sfin_ref[0, 0, 0] = s_f
    sfin_ref[0, 1, 0] = s_b

    o = o_s[0] + o_s[1]
    z = z_ref[0]
    y = _rms_rows(o, onorm_ref[...]) * (z * jax.nn.sigmoid(z))
    y_ref[0] = y.astype(y_ref.dtype)


def _delta_mixer_v2(qkv, z, ab, conv_w, alog_row, dtb_row, out_norm, s0):
    b, t, _ = qkv.shape
    hh = N_HEADS_C
    c = DELTA_CHUNK
    n = t // c
    col = lambda off: pl.BlockSpec((1, t, LANES), lambda i, j, off=off: (i, 0, off + j))
    cw = lambda off: pl.BlockSpec((3, LANES), lambda i, j, off=off: (0, off + j))
    row = pl.BlockSpec((1, LANES), lambda i, j: (0, 0))
    st = pl.BlockSpec((1, 2, 1, DK_C, DV_C), lambda i, j: (i, 0, j, 0, 0))
    return pl.pallas_call(
        _delta_kernel,
        grid=(b, hh),
        in_specs=[col(0), col(hh), col(2 * hh), cw(0), cw(hh), cw(2 * hh),
                  pl.BlockSpec((1, t, LANES), lambda i, j: (i, 0, 0)), row, row,
                  pl.BlockSpec((1, t, LANES), lambda i, j: (i, 0, j)), row, st],
        out_specs=[pl.BlockSpec((1, t, LANES), lambda i, j: (i, 0, j)), st],
        out_shape=[jax.ShapeDtypeStruct((b, t, hh * DV_C), BF16),
                   jax.ShapeDtypeStruct((b, 2, hh, DK_C, DV_C), F32)],
        scratch_shapes=[pltpu.VMEM((t + 16, LANES), F32),
                        pltpu.VMEM((t, LANES), F32), pltpu.VMEM((t, LANES), F32), pltpu.VMEM((t, LANES), F32),
                        pltpu.VMEM((2, t, LANES), F32), pltpu.VMEM((2, t, LANES), F32),
                        pltpu.VMEM((2, t, LANES), F32), pltpu.VMEM((2, 2 * t, LANES), BF16),
                        pltpu.VMEM((2, t, LANES), BF16), pltpu.VMEM((2, t, LANES), BF16),
                        pltpu.VMEM((2, 8 * n, LANES), F32), pltpu.VMEM((2, t, LANES), F32)],
        compiler_params=_cparams(("arbitrary", "arbitrary")),
        name="delta_mixer",
    )(qkv, qkv, qkv, conv_w, conv_w, conv_w, ab, alog_row, dtb_row, z, out_norm.reshape(1, DV_C), s0)


def _delta_kernel(*refs, hb, hg, cg, has_s0, emit_state):
    (q_ref, k_ref, v_ref, cwq_ref, cwk_ref, cwv_ref, ab_ref, alog_ref, dtb_ref, z_ref, onorm_ref), rest = refs[:11], refs[11:]
    s0_ref = None
    if has_s0:
        s0_ref, rest = rest[0], rest[1:]
    y_ref, rest = rest[0], rest[1:]
    sfin_ref = None
    if emit_state:
        sfin_ref, rest = rest[0], rest[1:]
    pad_s, q_s, k_s, v_s, gates_s, u_s, wq_s, kt_s, att_s, el_s, o_s, s_s = rest
    t = q_ref.shape[1]
    c = DELTA_CHUNK
    n = t // c
    h0 = pl.program_id(1) * hb

    pad_s[0:8, :] = jnp.zeros((8, LANES), F32)
    pad_s[t + 8:t + 16, :] = jnp.zeros((8, LANES), F32)

    rbs = min(t, DELTA_ROW_BLOCK)
    nrb = t // rbs

    def conv_silu_norm(x_ref, cw_ref, hd, dst, scale):
        for i in range(nrb):
            pad_s[8 + i * rbs:8 + (i + 1) * rbs, :] = x_ref[0, i * rbs:(i + 1) * rbs, hd * LANES:(hd + 1) * LANES]
        cw = cw_ref[:, hd * LANES:(hd + 1) * LANES]

        def blk(i, carry):
            r = pl.multiple_of(i * rbs, rbs)
            y = (pad_s[pl.ds(r + 7, rbs), :] * cw[0:1] + pad_s[pl.ds(r + 8, rbs), :] * cw[1:2]
                 + pad_s[pl.ds(r + 9, rbs), :] * cw[2:3])
            y = y * jax.nn.sigmoid(y)
            if scale is not None:
                y = y * lax.rsqrt(jnp.sum(y * y, axis=-1, keepdims=True) + EPS) * scale
            dst[hd, pl.ds(r, rbs), :] = y
            return carry

        lax.fori_loop(0, nrb, blk, 0)

    for hd in range(hb):
        conv_silu_norm(q_ref, cwq_ref, hd, q_s, DK_C ** -0.5)
        conv_silu_norm(k_ref, cwk_ref, hd, k_s, 1.0)
        conv_silu_norm(v_ref, cwv_ref, hd, v_s, None)

    def gate_blk(i, carry):
        r = pl.multiple_of(i * rbs, rbs)
        ab = ab_ref[0, pl.ds(r, rbs), :]
        xa = ab + dtb_ref[...]
        softplus = jnp.maximum(xa, 0.0) + jnp.log1p(jnp.exp(-jnp.abs(xa)))
        lane_t = lax.broadcasted_iota(jnp.int32, ab.shape, 1)
        gates_s[pl.ds(r, rbs), :] = jnp.where(lane_t < 2 * N_HEADS_C, -jnp.exp(alog_ref[...]) * softplus,
                                              jax.nn.sigmoid(ab))
        return carry

    lax.fori_loop(0, nrb, gate_blk, 0)

    ii = lax.broadcasted_iota(jnp.int32, (c, c), 0)
    jj = lax.broadcasted_iota(jnp.int32, (c, c), 1)
    eye = jnp.where(ii == jj, 1.0, 0.0)
    ones_cc = jnp.ones((c, c), BF16)
    bd8 = (ii // 8) == (jj // 8)
    offs = []
    blk = 8
    while blk < c:
        offs.append((blk, ((ii // (2 * blk)) == (jj // (2 * blk))) & ((ii // blk) != (jj // blk))))
        blk *= 2
    dir_masks = []
    for d in range(2):
        if d == 0:
            incl, strict, incl_t = ii >= jj, ii > jj, ii <= jj
        else:
            incl, strict, incl_t = ii <= jj, ii < jj, ii >= jj
        dir_masks.append((incl, strict, incl_t, jnp.where(incl, 1.0, 0.0).astype(BF16)))
    pick_r = lax.broadcasted_iota(jnp.int32, (LANES, 4 * LANES), 0)
    pick_c = lax.broadcasted_iota(jnp.int32, (LANES, 4 * LANES), 1)

    n_cgroups = n // cg

    def prep(gi, carry):
        hg0 = (gi // n_cgroups) * hg
        c0 = (gi % n_cgroups) * cg
        chains = []
        for hh in range(hg):
            hd = hg0 + hh
            pick = jnp.where(pick_r == (pick_c // LANES) * N_HEADS_C + h0 + hd, 1.0, 0.0).astype(BF16)
            for cc in range(cg):
                ci = c0 + cc
                r0 = pl.multiple_of(ci * c, c)
                qc = q_s[hd, pl.ds(r0, c), :]
                kc = k_s[hd, pl.ds(r0, c), :]
                vc = v_s[hd, pl.ds(r0, c), :]
                qb, kb = qc.astype(BF16), kc.astype(BF16)
                kk = _dot_nt(kb, kb)
                qk = _dot_nt(qb, kb)
                picked = _dot2_rhs01(gates_s[pl.ds(r0, c), :], pick)
                for d in range(2):
                    chains.append(dict(hd=hd, ci=ci, r0=r0, d=d, qc=qc, kc=kc, vc=vc, kk=kk, qk=qk,
                                       la=picked[:, d * LANES:(d + 1) * LANES],
                                       be=picked[:, (2 + d) * LANES:(3 + d) * LANES]))
        for ch in chains:
            incl, strict, incl_t, cum_mask = dir_masks[ch["d"]]
            ch["g_col"] = _dot2_lhs01(cum_mask, ch["la"])
            ch["g_row"] = _dot2_lhs01(ones_cc, jnp.where(incl_t, ch["la"], 0.0))
        for ch in chains:
            incl, strict, incl_t, cum_mask = dir_masks[ch["d"]]
            ch["decay"] = jnp.where(incl, jnp.exp(jnp.where(incl, ch["g_col"] - ch["g_row"], 0.0)), 0.0)
            ch["a"] = jnp.where(strict, (ch["kk"] * ch["decay"]) * ch["be"], 0.0)
            ch["p8"] = jnp.where(bd8, -ch["a"], 0.0)
        for ch in chains:
            ch["pk2"] = _dot_t(ch["p8"], ch["p8"])
        for ch in chains:
            ch["acc"] = eye + ch["p8"]
            ch["r"] = _dot_t(ch["pk2"], jnp.concatenate([ch["pk2"], ch["acc"]], axis=1))
        for ch in chains:
            ch["acc"] = ch["acc"] + ch["r"][:, c:]
            ch["t"] = ch["acc"] + _dot_t(ch["r"][:, :c], ch["acc"])
        for blk, off in offs:
            for ch in chains:
                first = ch["d"] == 1
                y = _dot_t(_row_blocks(jnp.where(off, ch["a"], 0.0), blk, first), ch["t"])
                ch["y"] = _row_blocks_put(y, jnp.zeros((c, c), F32), blk, first)
            for ch in chains:
                first = ch["d"] == 1
                t_sel = _row_blocks(ch["t"], blk, first)
                ch["t"] = _row_blocks_put(t_sel - _dot_t(t_sel, ch["y"]), ch["t"], blk, first)
        for ch in chains:
            ch["eg"] = jnp.exp(ch["g_col"])
            rhs = jnp.concatenate([ch["vc"] * ch["be"], ch["kc"] * ch["be"] * ch["eg"]], axis=1)
            ch["uw"] = _dot3(ch["t"], rhs)
        for ch in chains:
            d, r0, ci, g_col = ch["d"], ch["r0"], ch["ci"], ch["g_col"]
            hx = ch["hd"] * 2 + d
            g_last = g_col[c - 1:c, :] if d == 0 else g_col[0:1, :]
            u_s[hx, pl.ds(r0, c), :] = ch["uw"][:, :c]
            r2 = pl.multiple_of(ci * 2 * c, 2 * c)
            wq_s[hx, pl.ds(r2, c), :] = ch["uw"][:, c:].astype(BF16)
            wq_s[hx, pl.ds(r2 + c, c), :] = (ch["qc"] * ch["eg"]).astype(BF16)
            kt_s[hx, pl.ds(r0, c), :] = (ch["kc"] * jnp.exp(g_last - g_col)).astype(BF16)
            att_s[hx, pl.ds(r0, c), :] = (ch["qk"] * ch["decay"]).astype(BF16)
            r8 = pl.multiple_of(ci * 8, 8)
            el_s[hx, pl.ds(r8, 8), :] = jnp.broadcast_to(jnp.exp(g_last), (8, LANES))
        return carry

    lax.fori_loop(0, (hb // hg) * n_cgroups, prep, 0)

    for hx in range(2 * hb):
        if has_s0:
            s_s[hx] = s0_ref[0, hx % 2, hx // 2]
        else:
            s_s[hx] = jnp.zeros((DK_C, DV_C), F32)

    def scan(i, carry):
        items = []
        for hx in range(2 * hb):
            ci = i if hx % 2 == 0 else n - 1 - i
            items.append(dict(hx=hx, r0=pl.multiple_of(ci * c, c), r2=pl.multiple_of(ci * 2 * c, 2 * c),
                              r8=pl.multiple_of(ci * 8, 8), s=s_s[hx]))
        for it in items:
            it["ws"] = _dot(wq_s[it["hx"], pl.ds(it["r2"], 2 * c), :], it["s"].astype(BF16))
        for it in items:
            it["vb"] = (u_s[it["hx"], pl.ds(it["r0"], c), :] - it["ws"][:c]).astype(BF16)
        for it in items:
            hx = it["hx"]
            o_s[hx, pl.ds(it["r0"], c), :] = it["ws"][c:] + _dot(att_s[hx, pl.ds(it["r0"], c), :], it["vb"])
            s_s[hx] = (it["s"] * el_s[hx, pl.ds(it["r8"], 1), :]
                       + _dot_tn(kt_s[hx, pl.ds(it["r0"], c), :], it["vb"]))
        return carry

    lax.fori_loop(0, n, scan, 0)
    if emit_state:
        for hx in range(2 * hb):
            sfin_ref[0, hx % 2, hx // 2] = s_s[hx]

    for hd in range(hb):
        def out_blk(i, carry, hd=hd):
            r = pl.multiple_of(i * rbs, rbs)
            o = o_s[2 * hd, pl.ds(r, rbs), :] + o_s[2 * hd + 1, pl.ds(r, rbs), :]
            z = z_ref[0, pl.ds(r, rbs), hd * LANES:(hd + 1) * LANES]
            y = _rms_rows(o, onorm_ref[...]) * (z * jax.nn.sigmoid(z))
            y_ref[0, pl.ds(r, rbs), hd * LANES:(hd + 1) * LANES] = y.astype(y_ref.dtype)
            return carry

        lax.fori_loop(0, nrb, out_blk, 0)


def _delta_mixer(qkv, z, ab, conv_w, alog_row, dtb_row, out_norm, s0, hb, emit_state):
    b, t, _ = qkv.shape
    nh = N_HEADS_C
    c = DELTA_CHUNK
    n = t // c
    cg = min(n, 4)
    hg = min(hb, max(1, DELTA_PREP_CHAINS // (2 * cg)))
    assert hb % hg == 0 and n % cg == 0 and nh % hb == 0
    w = hb * LANES
    has_s0 = s0 is not None
    col = lambda off: pl.BlockSpec((1, t, w), lambda i, j, off=off: (i, 0, off + j))
    cw = lambda off: pl.BlockSpec((3, w), lambda i, j, off=off: (0, off + j))
    row = pl.BlockSpec((1, LANES), lambda i, j: (0, 0))
    st = pl.BlockSpec((1, 2, hb, DK_C, DV_C), lambda i, j: (i, 0, j, 0, 0))
    nb = nh // hb
    in_specs = [col(0), col(nb), col(2 * nb), cw(0), cw(nb), cw(2 * nb),
                pl.BlockSpec((1, t, LANES), lambda i, j: (i, 0, 0)), row, row, col(0), row]
    args = [qkv, qkv, qkv, conv_w, conv_w, conv_w, ab, alog_row, dtb_row, z, out_norm.reshape(1, DV_C)]
    if has_s0:
        in_specs.append(st)
        args.append(s0)
    out_specs = [col(0)]
    out_shape = [jax.ShapeDtypeStruct((b, t, nh * DV_C), BF16)]
    if emit_state:
        out_specs.append(st)
        out_shape.append(jax.ShapeDtypeStruct((b, 2, nh, DK_C, DV_C), F32))
    hd2 = 2 * hb
    scratch = [pltpu.VMEM((t + 16, LANES), F32),
               pltpu.VMEM((hb, t, LANES), F32), pltpu.VMEM((hb, t, LANES), F32), pltpu.VMEM((hb, t, LANES), F32),
               pltpu.VMEM((t, LANES), F32),
               pltpu.VMEM((hd2, t, LANES), F32), pltpu.VMEM((hd2, 2 * t, LANES), BF16),
               pltpu.VMEM((hd2, t, LANES), BF16), pltpu.VMEM((hd2, t, LANES), BF16),
               pltpu.VMEM((hd2, 8 * n, LANES), F32), pltpu.VMEM((hd2, t, LANES), F32),
               pltpu.VMEM((hd2, DK_C, DV_C), F32)]
    return pl.pallas_call(
        functools.partial(_delta_kernel, hb=hb, hg=hg, cg=cg, has_s0=has_s0, emit_state=emit_state),
        grid=(b, nb),
        in_specs=in_specs, out_specs=out_specs, out_shape=out_shape, scratch_shapes=scratch,
        compiler_params=pltpu.CompilerParams(dimension_semantics=("arbitrary", "arbitrary"),
                                             vmem_limit_bytes=DELTA_VMEM_LIMIT),
        name="delta_mixer",
    )(*args)


def _pad_lanes(x, n):
    return jnp.pad(x, ((0, 0), (0, n - x.shape[1])))


def kernel(x_prompt, x_sample, c, cache_l0_k, cache_l0_v, state_l1, c_ctx, l0_mod_w, l0_mod_b, l0_norm1, l0_w_in, l0_q_norm_a, l0_k_norm_a, l0_q_norm_b, l0_k_norm_b, l0_rel_bias, l0_w_out, l0_norm2, l0_mlp_w1, l0_mlp_w2, l1_mod_w, l1_mod_b, l1_norm1, l1_w_in, l1_conv_w, l1_a_log, l1_dt_bias, l1_out_norm, l1_w_out, l1_norm2, l1_mlp_w1, l1_mlp_w2):
    bp, tp, d = x_prompt.shape
    bs, ts, _ = x_sample.shape
    n_cache = (N_KV_A + N_HEADS_B) * HEAD_DIM
    bf = lambda w: w.astype(BF16)

    n_rows = -(-(1 + bs) // 8) * 8
    cvec = jnp.concatenate([c_ctx[None, :], c, jnp.zeros((n_rows - 1 - bs, d), F32)], axis=0)
    mods = []
    for mw, mb in ((l0_mod_w, l0_mod_b), (l1_mod_w, l1_mod_b)):
        m = _modulation(cvec, mw, mb).reshape(n_rows, N_MOD, d)
        mods.append((m[0:1], m[1:1 + bs]))

    xp = x_prompt.reshape(1, bp * tp, d)
    xs = x_sample

    scale = HEAD_DIM ** -0.5 * LOG2_E
    ones_a = jnp.ones((N_KV_A * HEAD_DIM,), F32)
    ones_b = jnp.ones((N_HEADS_B * HEAD_DIM,), F32)
    gain_row = jnp.concatenate([jnp.tile(l0_q_norm_a, N_HEADS_A) * scale, jnp.tile(l0_k_norm_a, N_KV_A), ones_a,
                                jnp.tile(l0_q_norm_b, N_HEADS_B) * scale, jnp.tile(l0_k_norm_b, N_HEADS_B),
                                ones_b])[None, :]
    w_in0 = bf(l0_w_in)
    w_out0 = bf(l0_w_out)
    w1_0, w2_0 = bf(l0_mlp_w1), bf(l0_mlp_w2)
    half = N_HEADS_A * HEAD_DIM

    mod_p, mod_s = mods[0]
    (qkv_p,) = _project(xp, mod_p, l0_norm1, [w_in0], tm=512)
    o_p, new_k, new_v = _ctx_attention(qkv_p.reshape(bp, tp, -1), gain_row)
    xp = _mix_mlp(xp, mod_p, l0_norm2, [o_p.reshape(1, bp * tp, -1)], [w_out0], w1_0, w2_0, tm=512, tf=1024)

    (qkv_s,) = _project(xs, mod_s, l0_norm1, [w_in0], tm=512)
    ck = cache_l0_k.reshape(bs, -1, n_cache)
    cv = cache_l0_v.reshape(bs, -1, n_cache)
    o_a = _latent_attention_a(qkv_s, ck, cv, gain_row, tq=256)
    o_b = _latent_attention_b(qkv_s, ck, cv, gain_row, l0_rel_bias)
    xs = _mix_mlp(xs, mod_s, l0_norm2, [o_a, o_b], [w_out0[:half], w_out0[half:]], w1_0, w2_0, tm=512, tf=1024)

    n_qkv = N_HEADS_C * (2 * DK_C + DV_C)
    n_z = N_HEADS_C * DV_C
    w_in1 = bf(l1_w_in)
    w_pieces = [w_in1[:, :n_qkv], w_in1[:, n_qkv:n_qkv + n_z], _pad_lanes(w_in1[:, n_qkv + n_z:], LANES)]
    w_out1 = bf(l1_w_out)
    w1_1, w2_1 = bf(l1_mlp_w1), bf(l1_mlp_w2)
    alog_row = _pad_lanes(l1_a_log.reshape(1, -1), LANES)
    dtb_row = _pad_lanes(l1_dt_bias.reshape(1, -1), LANES)

    mod_p, mod_s = mods[1]
    qkv1_p, z_p, ab_p = _project(xp, mod_p, l1_norm1, w_pieces, tm=256)
    y_p, new_s = _delta_mixer(qkv1_p.reshape(bp, tp, -1), z_p.reshape(bp, tp, -1), ab_p.reshape(bp, tp, -1),
                              l1_conv_w, alog_row, dtb_row, l1_out_norm, None, hb=DELTA_HEADS_PROMPT, emit_state=True)
    xp = _mix_mlp(xp, mod_p, l1_norm2, [y_p.reshape(1, bp * tp, -1)], [w_out1], w1_1, w2_1, tm=512, tf=1024)

    qkv1_s, z_s, ab_s = _project(xs, mod_s, l1_norm1, w_pieces, tm=256)
    (y_s,) = _delta_mixer(qkv1_s, z_s, ab_s, l1_conv_w, alog_row, dtb_row, l1_out_norm, state_l1.astype(F32),
                          hb=DELTA_HEADS_SAMPLE, emit_state=False)
    xs = _mix_mlp(xs, mod_s, l1_norm2, [y_s], [w_out1], w1_1, w2_1, tm=512, tf=1024)

    return (xp.reshape(bp, tp, d), xs,
            new_k.reshape(bp, tp, N_KV_A + N_HEADS_B, HEAD_DIM), new_v.reshape(bp, tp, N_KV_A + N_HEADS_B, HEAD_DIM),
            new_s.astype(x_prompt.dtype))
```

```python
import functools
import math

import jax
import jax.numpy as jnp
from jax import lax
from jax.experimental import pallas as pl
from jax.experimental.pallas import tpu as pltpu

F32 = jnp.float32
BF16 = jnp.bfloat16

EPS = 1e-6
N_MOD = 6
HEAD_DIM = 64
N_HEADS_A = 8
N_KV_A = 2
N_HEADS_B = 8
GRID_W = 64
WIN_R = 8
WIN_C = 16
ROPE_THETA = 10000.0
N_HEADS_C = 8
DK_C = 128
DV_C = 128
LANES = 128
DELTA_CHUNK = 128
DELTA_PREP_GROUP = 4
DELTA_PREP_CHAINS = 16
DELTA_ROW_BLOCK = 256
DELTA_HEADS_PROMPT = 8
DELTA_HEADS_SAMPLE = 2
NBR_ROW_GROUP = 8
LAT_A_JOBS = 2
DELTA_VMEM_LIMIT = 56 * 1024 * 1024
NEG_BIG = -1e30
LOG2_E = math.log2(math.e)

VMEM_LIMIT = 48 * 1024 * 1024


def _cparams(sem):
    return pltpu.CompilerParams(dimension_semantics=sem, vmem_limit_bytes=VMEM_LIMIT)


def _dot(a, b):
    return jnp.dot(a, b, preferred_element_type=F32)


def _dot_nt(a, b):
    return lax.dot_general(a, b, (((1,), (1,)), ((), ())), preferred_element_type=F32)


def _dot_tn(a, b):
    return lax.dot_general(a, b, (((0,), (0,)), ((), ())), preferred_element_type=F32)


def _split2(x):
    hi = x.astype(BF16)
    lo = (x - hi.astype(F32)).astype(BF16)
    return hi, lo


def _split3(x):
    p1 = x.astype(BF16)
    r1 = x - p1.astype(F32)
    p2 = r1.astype(BF16)
    p3 = (r1 - p2.astype(F32)).astype(BF16)
    return p1, p2, p3


def _dot3(a, b):
    ah, al = _split2(a)
    bh, bl = _split2(b)
    lhs = jnp.concatenate([ah, ah, al], axis=1)
    rhs = jnp.concatenate([bh, bl, bh], axis=0)
    return _dot(lhs, rhs)


def _dot_t(a, b):
    ah, al = _split2(a)
    bh = b.astype(BF16)
    return _dot(jnp.concatenate([ah, al], axis=1), jnp.concatenate([bh, bh], axis=0))


def _dot2_rhs01(a_f32, b01):
    hi, lo = _split2(a_f32)
    return _dot(jnp.concatenate([hi, lo], axis=1), jnp.concatenate([b01, b01], axis=0))


def _dot2_lhs01(a01, b_f32):
    hi, lo = _split2(b_f32)
    return _dot(jnp.concatenate([a01, a01], axis=1), jnp.concatenate([hi, lo], axis=0))


def _dot2_lhs_bf16(a_f32, b_f32):
    ab = a_f32.astype(BF16)
    hi, lo = _split2(b_f32)
    return _dot(jnp.concatenate([ab, ab], axis=1), jnp.concatenate([hi, lo], axis=0))


def _pair_dots(lhs, rhs):
    outs = []
    for i in range(0, len(lhs), 2):
        r0, r1 = rhs[i].astype(BF16), rhs[i + 1].astype(BF16)
        z = jnp.zeros_like(r0)
        rr = jnp.concatenate([jnp.concatenate([r0, z], axis=1), jnp.concatenate([z, r1], axis=1)], axis=0)
        x = _dot(jnp.concatenate([lhs[i].astype(BF16), lhs[i + 1].astype(BF16)], axis=1), rr)
        outs += [x[:, :LANES], x[:, LANES:]]
    return outs


def _row_blocks(x, blk, first):
    return jnp.concatenate([x[i * blk:(i + 1) * blk] for i in range(0 if first else 1, x.shape[0] // blk, 2)], axis=0)


def _row_blocks_put(sel, rest, blk, first):
    pieces = []
    for i in range(rest.shape[0] // blk):
        if (i % 2 == 0) == first:
            pieces.append(sel[(i // 2) * blk:(i // 2 + 1) * blk])
        else:
            pieces.append(rest[i * blk:(i + 1) * blk])
    return jnp.concatenate(pieces, axis=0)


def _rms_rows(x, gain):
    ms = jnp.mean(x * x, axis=-1, keepdims=True)
    return x * lax.rsqrt(ms + EPS) * gain


def _mod_kernel(c_ref, w_ref, b_ref, o_ref):
    c = c_ref[...]
    a = (c * jax.nn.sigmoid(c)).astype(BF16)
    o_ref[...] = _dot(a, w_ref[...].astype(BF16)) + b_ref[...]


def _modulation(cvec, w, b):
    rows, d = cvec.shape
    n = w.shape[1]
    tn = 1536
    return pl.pallas_call(
        _mod_kernel,
        grid=(n // tn,),
        in_specs=[pl.BlockSpec((rows, d), lambda j: (0, 0)),
                  pl.BlockSpec((d, tn), lambda j: (0, j)),
                  pl.BlockSpec((1, tn), lambda j: (0, j))],
        out_specs=pl.BlockSpec((rows, tn), lambda j: (0, j)),
        out_shape=jax.ShapeDtypeStruct((rows, n), F32),
        compiler_params=_cparams(("arbitrary",)),
        name="modulation",
    )(cvec, w, b.reshape(1, n))


def _proj_kernel(x_ref, mod_ref, g_ref, *rest, n_out):
    w_refs, o_refs = rest[:n_out], rest[n_out:]
    mod = mod_ref[0]
    h = _rms_rows(x_ref[0], g_ref[...]) * (1.0 + mod[1:2]) + mod[0:1]
    hb = h.astype(BF16)
    for w_ref, o_ref in zip(w_refs, o_refs):
        o_ref[0] = _dot(hb, w_ref[...]).astype(o_ref.dtype)


def _project(x, mod, gain, weights, tm):
    g, t, d = x.shape
    n_out = len(weights)
    in_specs = [pl.BlockSpec((1, tm, d), lambda i, j: (i, j, 0)),
                pl.BlockSpec((1, N_MOD, d), lambda i, j: (i, 0, 0)),
                pl.BlockSpec((1, d), lambda i, j: (0, 0))]
    in_specs += [pl.BlockSpec(w.shape, lambda i, j: (0, 0)) for w in weights]
    out_specs = [pl.BlockSpec((1, tm, w.shape[1]), lambda i, j: (i, j, 0)) for w in weights]
    out_shape = [jax.ShapeDtypeStruct((g, t, w.shape[1]), F32) for w in weights]
    return pl.pallas_call(
        functools.partial(_proj_kernel, n_out=n_out),
        grid=(g, t // tm),
        in_specs=in_specs, out_specs=out_specs, out_shape=out_shape,
        compiler_params=_cparams(("arbitrary", "arbitrary")),
        name="norm_project",
    )(x, mod, gain.reshape(1, d), *weights)


def _mlp_kernel(*refs, n_mix):
    x_ref, mod_ref, g_ref = refs[:3]
    o_refs = refs[3:3 + n_mix]
    wo_refs = refs[3 + n_mix:3 + 2 * n_mix]
    w1_ref, w2_ref, out_ref, x1_s, h_s, acc_s = refs[3 + 2 * n_mix:]
    f = pl.program_id(2)

    @pl.when(f == 0)
    def _():
        mod = mod_ref[0]
        mp = _dot(o_refs[0][0].astype(BF16), wo_refs[0][...])
        for o_ref, wo_ref in zip(o_refs[1:], wo_refs[1:]):
            mp += _dot(o_ref[0].astype(BF16), wo_ref[...])
        x1 = x_ref[0] + mod[2:3] * mp
        x1_s[...] = x1
        h = _rms_rows(x1, g_ref[...]) * (1.0 + mod[4:5]) + mod[3:4]
        h_s[...] = h.astype(BF16)
        acc_s[...] = jnp.zeros_like(acc_s)

    a = _dot(h_s[...], w1_ref[...])
    a = jnp.square(jnp.maximum(a, 0.0))
    acc_s[...] += _dot(a.astype(BF16), w2_ref[...])

    @pl.when(f == pl.num_programs(2) - 1)
    def _():
        out_ref[0] = x1_s[...] + mod_ref[0][5:6] * acc_s[...]


def _mix_mlp(x, mod, gain2, mixes, w_outs, w1, w2, tm, tf):
    g, t, d = x.shape
    dff = w1.shape[1]
    n_mix = len(mixes)
    in_specs = [pl.BlockSpec((1, tm, d), lambda i, j, f: (i, j, 0)),
                pl.BlockSpec((1, N_MOD, d), lambda i, j, f: (i, 0, 0)),
                pl.BlockSpec((1, d), lambda i, j, f: (0, 0))]
    in_specs += [pl.BlockSpec((1, tm, o.shape[2]), lambda i, j, f: (i, j, 0)) for o in mixes]
    in_specs += [pl.BlockSpec(w.shape, lambda i, j, f: (0, 0)) for w in w_outs]
    in_specs += [pl.BlockSpec((d, tf), lambda i, j, f: (0, f)),
                 pl.BlockSpec((tf, d), lambda i, j, f: (f, 0))]
    return pl.pallas_call(
        functools.partial(_mlp_kernel, n_mix=n_mix),
        grid=(g, t // tm, dff // tf),
        in_specs=in_specs,
        out_specs=pl.BlockSpec((1, tm, d), lambda i, j, f: (i, j, 0)),
        out_shape=jax.ShapeDtypeStruct((g, t, d), F32),
        scratch_shapes=[pltpu.VMEM((tm, d), F32), pltpu.VMEM((tm, d), BF16), pltpu.VMEM((tm, d), F32)],
        compiler_params=_cparams(("arbitrary", "arbitrary", "arbitrary")),
        name="mix_mlp",
    )(x, mod, gain2.reshape(1, d), *mixes, *w_outs, w1, w2)


def _pair_ones():
    r = lax.broadcasted_iota(jnp.int32, (LANES, LANES), 0) // HEAD_DIM
    c = lax.broadcasted_iota(jnp.int32, (LANES, LANES), 1) // HEAD_DIM
    return jnp.where(r == c, 1.0, 0.0).astype(BF16)


def _head_rms(xb, gain, pair_ones):
    hi, lo = _split2(xb * xb)
    ss = _dot(jnp.concatenate([hi, lo], axis=1), jnp.concatenate([pair_ones, pair_ones], axis=0))
    return xb * lax.rsqrt(ss * (1.0 / HEAD_DIM) + EPS) * gain


def _dup_half(x, g):
    lane = lax.broadcasted_iota(jnp.int32, x.shape, 1)
    sw = pltpu.roll(x, HEAD_DIM, 1)
    if g == 0:
        return jnp.where(lane < HEAD_DIM, x, sw)
    return jnp.where(lane < HEAD_DIM, sw, x)


def _softmax_parts(scores):
    m = jnp.max(scores[0], axis=-1, keepdims=True)
    for s in scores[1:]:
        m = jnp.maximum(m, jnp.max(s, axis=-1, keepdims=True))
    ps = [jnp.exp2(s - m) for s in scores]
    l = jnp.sum(ps[0], axis=-1, keepdims=True)
    for p in ps[1:]:
        l += jnp.sum(p, axis=-1, keepdims=True)
    return ps, 1.0 / l


def _attend_pairs(jobs):
    tasks = []
    for qb, ks, vs, biases in jobs:
        lane = lax.broadcasted_iota(jnp.int32, qb.shape, 1)
        for hh in range(2):
            sel = (lane < HEAD_DIM) if hh == 0 else (lane >= HEAD_DIM)
            tasks.append(dict(q=jnp.where(sel, qb, jnp.zeros_like(qb)), ks=ks, vs=vs,
                              bias=None if biases is None else biases[hh]))
    for tk in tasks:
        scores = [_dot_nt(tk["q"], k) for k in tk["ks"]]
        if tk["bias"] is not None:
            scores = [s if b is None else s + b for s, b in zip(scores, tk["bias"])]
        tk["s"] = scores
    for tk in tasks:
        tk["p"], tk["inv"] = _softmax_parts(tk["s"])
    for tk in tasks:
        o = _dot(tk["p"][0].astype(BF16), tk["vs"][0])
        for p, v in zip(tk["p"][1:], tk["vs"][1:]):
            o += _dot(p.astype(BF16), v)
        tk["o"] = o * tk["inv"]
    outs = []
    for j in range(len(jobs)):
        o0, o1 = tasks[2 * j]["o"], tasks[2 * j + 1]["o"]
        lane_o = lax.broadcasted_iota(jnp.int32, o0.shape, 1)
        outs.append(jnp.where(lane_o < HEAD_DIM, o0, o1))
    return outs


def _attend_pair(qb, ks, vs, biases=None):
    return _attend_pairs([(qb, ks, vs, biases)])[0]


_QA0, _KA, _VA, _QB0, _KB0, _VB0 = 0, 4, 5, 6, 10, 14
_N_QKV_BLOCKS = 18


def _ctx_attn_kernel(qkv_ref, gain_ref, o_ref, nk_ref, nv_ref):
    ones = _pair_ones()

    def blk(j):
        return qkv_ref[0, :, j * LANES:(j + 1) * LANES]

    def gain(j):
        return gain_ref[:, j * LANES:(j + 1) * LANES]

    ka = _head_rms(blk(_KA), gain(_KA), ones)
    va = blk(_VA)
    nk_ref[0, :, 0:LANES] = ka
    nv_ref[0, :, 0:LANES] = va
    k2 = [_dup_half(ka, g).astype(BF16) for g in range(N_KV_A)]
    v2 = [_dup_half(va, g).astype(BF16) for g in range(N_KV_A)]
    jobs = []
    for j in range(4):
        qn = _head_rms(blk(_QA0 + j), gain(_QA0 + j), ones).astype(BF16)
        jobs.append((qn, [k2[j // 2]], [v2[j // 2]], None))
    for j, o in enumerate(_attend_pairs(jobs)):
        o_ref[0, :, j * LANES:(j + 1) * LANES] = o.astype(o_ref.dtype)
    jobs = []
    for j in range(4):
        kb = _head_rms(blk(_KB0 + j), gain(_KB0 + j), ones)
        vb = blk(_VB0 + j)
        nk_ref[0, :, (1 + j) * LANES:(2 + j) * LANES] = kb
        nv_ref[0, :, (1 + j) * LANES:(2 + j) * LANES] = vb
        qn = _head_rms(blk(_QB0 + j), gain(_QB0 + j), ones).astype(BF16)
        jobs.append((qn, [kb.astype(BF16)], [vb.astype(BF16)], None))
    for j, o in enumerate(_attend_pairs(jobs)):
        o_ref[0, :, (4 + j) * LANES:(5 + j) * LANES] = o.astype(o_ref.dtype)


def _ctx_attention(qkv, gain_row):
    b, t, n = qkv.shape
    n_cache = (N_KV_A + N_HEADS_B) * HEAD_DIM
    n_o = (N_HEADS_A + N_HEADS_B) * HEAD_DIM
    return pl.pallas_call(
        _ctx_attn_kernel,
        grid=(b,),
        in_specs=[pl.BlockSpec((1, t, n), lambda i: (i, 0, 0)),
                  pl.BlockSpec((1, n), lambda i: (0, 0))],
        out_specs=[pl.BlockSpec((1, t, n_o), lambda i: (i, 0, 0)),
                   pl.BlockSpec((1, t, n_cache), lambda i: (i, 0, 0)),
                   pl.BlockSpec((1, t, n_cache), lambda i: (i, 0, 0))],
        out_shape=[jax.ShapeDtypeStruct((b, t, n_o), BF16),
                   jax.ShapeDtypeStruct((b, t, n_cache), F32),
                   jax.ShapeDtypeStruct((b, t, n_cache), F32)],
        compiler_params=_cparams(("arbitrary",)),
        name="ctx_attention",
    )(qkv, gain_row)


def _rope(x, c, sa, sb):
    return x * c + pltpu.roll(x, LANES - 16, 1) * sa + pltpu.roll(x, 16, 1) * sb


def _lat_a_kernel(q_ref, k_ref, v_ref, ck_ref, cv_ref, gq_ref, gk_ref,
                  cq_ref, saq_ref, sbq_ref, ck_t_ref, sak_t_ref, sbk_t_ref,
                  o_ref, k2_s, v2_s):
    t = k_ref.shape[1]
    ones = _pair_ones()

    @pl.when(pl.program_id(1) == 0)
    def _():
        kn = _head_rms(k_ref[0], gk_ref[...], ones)
        kr = _rope(kn, ck_t_ref[...], sak_t_ref[...], sbk_t_ref[...])
        v = v_ref[0]
        ck = ck_ref[0]
        cv = cv_ref[0]
        for g in range(N_KV_A):
            k2_s[g, 0:t, :] = _dup_half(kr, g).astype(BF16)
            k2_s[g, t:, :] = _dup_half(ck, g).astype(BF16)
            v2_s[g, 0:t, :] = _dup_half(v, g).astype(BF16)
            v2_s[g, t:, :] = _dup_half(cv, g).astype(BF16)

    c, sa, sb = cq_ref[...], saq_ref[...], sbq_ref[...]
    for j0 in range(0, 4, LAT_A_JOBS):
        jobs = []
        for j in range(j0, j0 + LAT_A_JOBS):
            qn = _head_rms(q_ref[0, :, j * LANES:(j + 1) * LANES], gq_ref[:, j * LANES:(j + 1) * LANES], ones)
            jobs.append((_rope(qn, c, sa, sb).astype(BF16), [k2_s[j // 2]], [v2_s[j // 2]], None))
        for j, o in zip(range(j0, j0 + LAT_A_JOBS), _attend_pairs(jobs)):
            o_ref[0, :, j * LANES:(j + 1) * LANES] = o.astype(o_ref.dtype)


def _rope_tables(t):
    pos = jnp.arange(t, dtype=jnp.int32)
    rc = jnp.stack([pos // GRID_W, pos % GRID_W], axis=-1).astype(F32)
    axis_dim = HEAD_DIM // 2
    inv_freq = 1.0 / (ROPE_THETA ** (jnp.arange(0, axis_dim, 2, dtype=F32) / axis_dim))
    ang = rc[:, :, None] * inv_freq
    cos, sin = jnp.cos(ang), jnp.sin(ang)
    zero = jnp.zeros_like(sin)
    c = jnp.stack([cos, cos], axis=2).reshape(t, HEAD_DIM)
    sa = jnp.stack([-sin, zero], axis=2).reshape(t, HEAD_DIM)
    sb = jnp.stack([zero, sin], axis=2).reshape(t, HEAD_DIM)
    two = lambda a: jnp.concatenate([a, a], axis=-1)
    return two(c), two(sa), two(sb)


def _latent_attention_a(qkv, cache_k, cache_v, gain_row, tq):
    b, t, _ = qkv.shape
    p = cache_k.shape[1]
    c, sa, sb = _rope_tables(t)
    gq = gain_row[:, _QA0 * LANES:(_QA0 + 4) * LANES]
    gk = gain_row[:, _KA * LANES:(_KA + 1) * LANES]
    tab_q = pl.BlockSpec((tq, LANES), lambda i, j: (j, 0))
    tab_k = pl.BlockSpec((t, LANES), lambda i, j: (0, 0))
    return pl.pallas_call(
        _lat_a_kernel,
        grid=(b, t // tq),
        in_specs=[pl.BlockSpec((1, tq, 4 * LANES), lambda i, j: (i, j, 0)),
                  pl.BlockSpec((1, t, LANES), lambda i, j: (i, 0, _KA)),
                  pl.BlockSpec((1, t, LANES), lambda i, j: (i, 0, _VA)),
                  pl.BlockSpec((1, p, LANES), lambda i, j: (i, 0, 0)),
                  pl.BlockSpec((1, p, LANES), lambda i, j: (i, 0, 0)),
                  pl.BlockSpec((1, 4 * LANES), lambda i, j: (0, 0)),
                  pl.BlockSpec((1, LANES), lambda i, j: (0, 0)),
                  tab_q, tab_q, tab_q, tab_k, tab_k, tab_k],
        out_specs=pl.BlockSpec((1, tq, 4 * LANES), lambda i, j: (i, j, 0)),
        out_shape=jax.ShapeDtypeStruct((b, t, 4 * LANES), BF16),
        scratch_shapes=[pltpu.VMEM((N_KV_A, t + p, LANES), BF16), pltpu.VMEM((N_KV_A, t + p, LANES), BF16)],
        compiler_params=_cparams(("arbitrary", "arbitrary")),
        name="latent_attention_a",
    )(qkv, qkv, qkv, cache_k, cache_v, gq, gk, c, sa, sb, c, sa, sb)


def _nbr_bias_table(rel_bias, rows):
    wr = min(WIN_R, rows)
    cols = jnp.arange(GRID_W, dtype=jnp.int32)
    col_start = jnp.clip(cols - WIN_C // 2, 0, GRID_W - WIN_C)
    kc = jnp.arange(GRID_W, dtype=jnp.int32)
    in_win = (kc[None, :] >= col_start[:, None]) & (kc[None, :] < col_start[:, None] + WIN_C)
    nh, n_ro = rel_bias.shape[0], 2 * wr - 1
    rb = rel_bias[:, WIN_R - wr:WIN_R + wr - 1, :].astype(F32) * LOG2_E
    span = 2 * GRID_W
    lead = GRID_W - WIN_C
    v = jnp.pad(rb, ((0, 0), (0, 0), (lead, span - lead - (2 * WIN_C - 1))))
    flat = jnp.tile(v, (1, 1, GRID_W))
    toep = flat[..., GRID_W - 1:GRID_W - 1 + GRID_W * (span - 1)].reshape(nh, n_ro, GRID_W, span - 1)[..., :GRID_W]
    base = jnp.where(in_win[None, None], toep, NEG_BIG)
    return jnp.concatenate([base[:, :-1], base[:, 1:]], axis=-1)


def _nbr_kernel(q_ref, k_ref, v_ref, ck_ref, cv_ref, gq_ref, gk_ref, bias_ref, o_ref, qn_s, kn_s, vn_s):
    t = q_ref.shape[1]
    rows = t // GRID_W
    wr = min(WIN_R, rows)
    ones = _pair_ones()
    qn_s[...] = _head_rms(q_ref[0], gq_ref[0], ones).astype(BF16)
    kn_s[...] = _head_rms(k_ref[0], gk_ref[0], ones).astype(BF16)
    vn_s[...] = v_ref[0].astype(BF16)
    ck = ck_ref[0].astype(BF16)
    cv = cv_ref[0].astype(BF16)

    rb = math.gcd(rows, NBR_ROW_GROUP)

    def row_group(gi, carry):
        jobs, q0s = [], []
        for rr in range(rb):
            r = gi * rb + rr
            rs = jnp.clip(r - wr // 2, 0, rows - wr)
            var = r - rs
            q0 = pl.multiple_of(r * GRID_W, GRID_W)
            k0 = pl.multiple_of(rs * GRID_W, GRID_W)
            biases = [[jnp.concatenate([bias_ref[hh, wr - 1 - var + 2 * m] for m in range(wr // 2)], axis=-1), None]
                      for hh in range(2)]
            jobs.append((qn_s[pl.ds(q0, GRID_W), :], [kn_s[pl.ds(k0, wr * GRID_W), :], ck],
                         [vn_s[pl.ds(k0, wr * GRID_W), :], cv], biases))
            q0s.append(q0)
        for q0, o in zip(q0s, _attend_pairs(jobs)):
            o_ref[0, pl.ds(q0, GRID_W), :] = o.astype(o_ref.dtype)
        return carry

    lax.fori_loop(0, rows // rb, row_group, 0)


def _latent_attention_b(qkv, cache_k, cache_v, gain_row, rel_bias):
    b, t, _ = qkv.shape
    p = cache_k.shape[1]
    rows = t // GRID_W
    wr = min(WIN_R, rows)
    assert wr % 2 == 0
    n_tab = 2 * wr - 2
    table = _nbr_bias_table(rel_bias, rows).reshape(N_HEADS_B // 2, 2, n_tab, GRID_W, 2 * GRID_W)
    gq = gain_row[:, _QB0 * LANES:(_QB0 + 4) * LANES].reshape(4, 1, LANES)
    gk = gain_row[:, _KB0 * LANES:(_KB0 + 4) * LANES].reshape(4, 1, LANES)
    return pl.pallas_call(
        _nbr_kernel,
        grid=(b, N_HEADS_B // 2),
        in_specs=[pl.BlockSpec((1, t, LANES), lambda i, j: (i, 0, _QB0 + j)),
                  pl.BlockSpec((1, t, LANES), lambda i, j: (i, 0, _KB0 + j)),
                  pl.BlockSpec((1, t, LANES), lambda i, j: (i, 0, _VB0 + j)),
                  pl.BlockSpec((1, p, LANES), lambda i, j: (i, 0, 1 + j)),
                  pl.BlockSpec((1, p, LANES), lambda i, j: (i, 0, 1 + j)),
                  pl.BlockSpec((1, 1, LANES), lambda i, j: (j, 0, 0)),
                  pl.BlockSpec((1, 1, LANES), lambda i, j: (j, 0, 0)),
                  pl.BlockSpec((None, 2, n_tab, GRID_W, 2 * GRID_W), lambda i, j: (j, 0, 0, 0, 0))],
        out_specs=pl.BlockSpec((1, t, LANES), lambda i, j: (i, 0, j)),
        out_shape=jax.ShapeDtypeStruct((b, t, 4 * LANES), BF16),
        scratch_shapes=[pltpu.VMEM((t, LANES), BF16)] * 3,
        compiler_params=_cparams(("arbitrary", "arbitrary")),
        name="latent_attention_b",
    )(qkv, qkv, qkv, cache_k, cache_v, gq, gk, table)


def _delta_kernel_v2(q_ref, k_ref, v_ref, cwq_ref, cwk_ref, cwv_ref, ab_ref, alog_ref, dtb_ref, z_ref, onorm_ref,
                  s0_ref, y_ref, sfin_ref,
                  pad_s, q_s, k_s, v_s, la_s, be_s, u_s, wq_s, kt_s, att_s, el_s, o_s):
    t = q_ref.shape[1]
    c = DELTA_CHUNK
    n = t // c
    h = pl.program_id(1)

    row_t = lax.broadcasted_iota(jnp.int32, (t, 1), 0)
    pad_s[0:8, :] = jnp.zeros((8, LANES), F32)
    pad_s[t + 8:t + 16, :] = jnp.zeros((8, LANES), F32)

    def conv_silu(x_ref, cw_ref):
        pad_s[8:t + 8, :] = x_ref[0]
        cw = cw_ref[...]
        y = pad_s[7:t + 7, :] * cw[0:1] + pad_s[8:t + 8, :] * cw[1:2] + pad_s[9:t + 9, :] * cw[2:3]
        return y * jax.nn.sigmoid(y)

    q = conv_silu(q_ref, cwq_ref)
    q_s[...] = q * lax.rsqrt(jnp.sum(q * q, axis=-1, keepdims=True) + EPS) * (DK_C ** -0.5)
    k = conv_silu(k_ref, cwk_ref)
    k_s[...] = k * lax.rsqrt(jnp.sum(k * k, axis=-1, keepdims=True) + EPS)
    v_s[...] = conv_silu(v_ref, cwv_ref)
    del row_t

    ab = ab_ref[0]
    xa = ab + dtb_ref[...]
    softplus = jnp.maximum(xa, 0.0) + jnp.log1p(jnp.exp(-jnp.abs(xa)))
    la_all = -jnp.exp(alog_ref[...]) * softplus
    be_all = jax.nn.sigmoid(ab)
    lane_t = lax.broadcasted_iota(jnp.int32, ab.shape, 1)
    gates = jnp.where(lane_t < 2 * N_HEADS_C, la_all, be_all)
    pick_r = lax.broadcasted_iota(jnp.int32, (LANES, 4 * LANES), 0)
    pick_c = lax.broadcasted_iota(jnp.int32, (LANES, 4 * LANES), 1)
    pick = jnp.where(pick_r == (pick_c // LANES) * N_HEADS_C + h, 1.0, 0.0).astype(BF16)
    picked = _dot2_rhs01(gates, pick)
    for d in range(2):
        la_s[d] = picked[:, d * LANES:(d + 1) * LANES]
        be_s[d] = picked[:, (2 + d) * LANES:(3 + d) * LANES]

    ii = lax.broadcasted_iota(jnp.int32, (c, c), 0)
    jj = lax.broadcasted_iota(jnp.int32, (c, c), 1)
    eye = jnp.where(ii == jj, 1.0, 0.0)
    ones_cc = jnp.ones((c, c), BF16)
    bd8 = (ii // 8) == (jj // 8)
    offs = []
    blk = 8
    while blk < c:
        offs.append((blk, ((ii // (2 * blk)) == (jj // (2 * blk))) & ((ii // blk) != (jj // blk))))
        blk *= 2

    cg = min(n, DELTA_PREP_GROUP)
    dir_masks = []
    for d in range(2):
        if d == 0:
            incl, strict, incl_t = ii >= jj, ii > jj, ii <= jj
        else:
            incl, strict, incl_t = ii <= jj, ii < jj, ii >= jj
        dir_masks.append((incl, strict, incl_t, jnp.where(incl, 1.0, 0.0).astype(BF16)))

    def prep(gi, carry):
        chains = []
        for cc in range(cg):
            ci = gi * cg + cc
            r0 = pl.multiple_of(ci * c, c)
            qc = q_s[pl.ds(r0, c), :]
            kc = k_s[pl.ds(r0, c), :]
            vc = v_s[pl.ds(r0, c), :]
            qb, kb = qc.astype(BF16), kc.astype(BF16)
            kk = _dot_nt(kb, kb)
            qk = _dot_nt(qb, kb)
            for d in range(2):
                chains.append(dict(ci=ci, r0=r0, d=d, qc=qc, kc=kc, vc=vc, kk=kk, qk=qk,
                                   la=la_s[d, pl.ds(r0, c), :], be=be_s[d, pl.ds(r0, c), :]))
        for ch in chains:
            incl, strict, incl_t, cum_mask = dir_masks[ch["d"]]
            ch["g_col"] = _dot2_lhs01(cum_mask, ch["la"])
            ch["g_row"] = _dot2_lhs01(ones_cc, jnp.where(incl_t, ch["la"], 0.0))
        for ch in chains:
            incl, strict, incl_t, cum_mask = dir_masks[ch["d"]]
            ch["decay"] = jnp.where(incl, jnp.exp(jnp.where(incl, ch["g_col"] - ch["g_row"], 0.0)), 0.0)
            ch["a"] = jnp.where(strict, (ch["kk"] * ch["decay"]) * ch["be"], 0.0)
            ch["p8"] = jnp.where(bd8, -ch["a"], 0.0)
        for ch in chains:
            ch["pk2"] = _dot_t(ch["p8"], ch["p8"])
        for ch in chains:
            ch["acc"] = eye + ch["p8"]
            ch["r"] = _dot_t(ch["pk2"], jnp.concatenate([ch["pk2"], ch["acc"]], axis=1))
        for ch in chains:
            ch["acc"] = ch["acc"] + ch["r"][:, c:]
            ch["t"] = ch["acc"] + _dot_t(ch["r"][:, :c], ch["acc"])
        for blk, off in offs:
            for ch in chains:
                first = ch["d"] == 1
                y = _dot_t(_row_blocks(jnp.where(off, ch["a"], 0.0), blk, first), ch["t"])
                ch["y"] = _row_blocks_put(y, jnp.zeros((c, c), F32), blk, first)
            for ch in chains:
                first = ch["d"] == 1
                t_sel = _row_blocks(ch["t"], blk, first)
                ch["t"] = _row_blocks_put(t_sel - _dot_t(t_sel, ch["y"]), ch["t"], blk, first)
        for ch in chains:
            ch["eg"] = jnp.exp(ch["g_col"])
            rhs = jnp.concatenate([ch["vc"] * ch["be"], ch["kc"] * ch["be"] * ch["eg"]], axis=1)
            ch["uw"] = _dot3(ch["t"], rhs)
        for ch in chains:
            d, r0, ci, g_col = ch["d"], ch["r0"], ch["ci"], ch["g_col"]
            g_last = g_col[c - 1:c, :] if d == 0 else g_col[0:1, :]
            u_s[d, pl.ds(r0, c), :] = ch["uw"][:, :c]
            r2 = pl.multiple_of(ci * 2 * c, 2 * c)
            wq_s[d, pl.ds(r2, c), :] = ch["uw"][:, c:].astype(BF16)
            wq_s[d, pl.ds(r2 + c, c), :] = (ch["qc"] * ch["eg"]).astype(BF16)
            kt_s[d, pl.ds(r0, c), :] = (ch["kc"] * jnp.exp(g_last - g_col)).astype(BF16)
            att_s[d, pl.ds(r0, c), :] = (ch["qk"] * ch["decay"]).astype(BF16)
            r8 = pl.multiple_of(ci * 8, 8)
            el_s[d, pl.ds(r8, 8), :] = jnp.broadcast_to(jnp.exp(g_last), (8, LANES))
        return carry

    lax.fori_loop(0, n // cg, prep, 0)

    def scan_dir(d, ci, s):
        r0 = pl.multiple_of(ci * c, c)
        r2 = pl.multiple_of(ci * 2 * c, 2 * c)
        r8 = pl.multiple_of(ci * 8, 8)
        ws = _dot(wq_s[d, pl.ds(r2, 2 * c), :], s.astype(BF16))
        v_new = u_s[d, pl.ds(r0, c), :] - ws[:c]
        vb = v_new.astype(BF16)
        o = ws[c:] + _dot(att_s[d, pl.ds(r0, c), :], vb)
        s = s * el_s[d, pl.ds(r8, 1), :] + _dot_tn(kt_s[d, pl.ds(r0, c), :], vb)
        return o, s

    def scan(i, carry):
        s_f, s_b = carry
        o_f, s_f = scan_dir(0, i, s_f)
        o_s[0, pl.ds(pl.multiple_of(i * c, c), c), :] = o_f
        ib = n - 1 - i
        o_b, s_b = scan_dir(1, ib, s_b)
        o_s[1, pl.ds(pl.multiple_of(ib * c, c), c), :] = o_b
        return s_f, s_b

    s_f, s_b = lax.fori_loop(0, n, scan, (s0_ref[0, 0, 0], s0_ref[0, 1, 0]))
    sfin_ref[0, 0, 0] = s_f
    sfin_ref[0, 1, 0] = s_b

    o = o_s[0] + o_s[1]
    z = z_ref[0]
    y = _rms_rows(o, onorm_ref[...]) * (z * jax.nn.sigmoid(z))
    y_ref[0] = y.astype(y_ref.dtype)


def _delta_mixer_v2(qkv, z, ab, conv_w, alog_row, dtb_row, out_norm, s0):
    b, t, _ = qkv.shape
    hh = N_HEADS_C
    c = DELTA_CHUNK
    n = t // c
    col = lambda off: pl.BlockSpec((1, t, LANES), lambda i, j, off=off: (i, 0, off + j))
    cw = lambda off: pl.BlockSpec((3, LANES), lambda i, j, off=off: (0, off + j))
    row = pl.BlockSpec((1, LANES), lambda i, j: (0, 0))
    st = pl.BlockSpec((1, 2, 1, DK_C, DV_C), lambda i, j: (i, 0, j, 0, 0))
    return pl.pallas_call(
        _delta_kernel,
        grid=(b, hh),
        in_specs=[col(0), col(hh), col(2 * hh), cw(0), cw(hh), cw(2 * hh),
                  pl.BlockSpec((1, t, LANES), lambda i, j: (i, 0, 0)), row, row,
                  pl.BlockSpec((1, t, LANES), lambda i, j: (i, 0, j)), row, st],
        out_specs=[pl.BlockSpec((1, t, LANES), lambda i, j: (i, 0, j)), st],
        out_shape=[jax.ShapeDtypeStruct((b, t, hh * DV_C), BF16),
                   jax.ShapeDtypeStruct((b, 2, hh, DK_C, DV_C), F32)],
        scratch_shapes=[pltpu.VMEM((t + 16, LANES), F32),
                        pltpu.VMEM((t, LANES), F32), pltpu.VMEM((t, LANES), F32), pltpu.VMEM((t, LANES), F32),
                        pltpu.VMEM((2, t, LANES), F32), pltpu.VMEM((2, t, LANES), F32),
                        pltpu.VMEM((2, t, LANES), F32), pltpu.VMEM((2, 2 * t, LANES), BF16),
                        pltpu.VMEM((2, t, LANES), BF16), pltpu.VMEM((2, t, LANES), BF16),
                        pltpu.VMEM((2, 8 * n, LANES), F32), pltpu.VMEM((2, t, LANES), F32)],
        compiler_params=_cparams(("arbitrary", "arbitrary")),
        name="delta_mixer",
    )(qkv, qkv, qkv, conv_w, conv_w, conv_w, ab, alog_row, dtb_row, z, out_norm.reshape(1, DV_C), s0)


def _delta_kernel(*refs, hb, hg, cg, has_s0, emit_state):
    (q_ref, k_ref, v_ref, cwq_ref, cwk_ref, cwv_ref, ab_ref, alog_ref, dtb_ref, z_ref, onorm_ref), rest = refs[:11], refs[11:]
    s0_ref = None
    if has_s0:
        s0_ref, rest = rest[0], rest[1:]
    y_ref, rest = rest[0], rest[1:]
    sfin_ref = None
    if emit_state:
        sfin_ref, rest = rest[0], rest[1:]
    pad_s, q_s, k_s, v_s, gates_s, u_s, wq_s, kt_s, att_s, el_s, o_s, s_s = rest
    t = q_ref.shape[1]
    c = DELTA_CHUNK
    n = t // c
    h0 = pl.program_id(1) * hb

    pad_s[0:8, :] = jnp.zeros((8, LANES), F32)
    pad_s[t + 8:t + 16, :] = jnp.zeros((8, LANES), F32)

    rbs = min(t, DELTA_ROW_BLOCK)
    nrb = t // rbs

    def conv_silu_norm(x_ref, cw_ref, hd, dst, scale):
        for i in range(nrb):
            pad_s[8 + i * rbs:8 + (i + 1) * rbs, :] = x_ref[0, i * rbs:(i + 1) * rbs, hd * LANES:(hd + 1) * LANES]
        cw = cw_ref[:, hd * LANES:(hd + 1) * LANES]

        def blk(i, carry):
            r = pl.multiple_of(i * rbs, rbs)
            y = (pad_s[pl.ds(r + 7, rbs), :] * cw[0:1] + pad_s[pl.ds(r + 8, rbs), :] * cw[1:2]
                 + pad_s[pl.ds(r + 9, rbs), :] * cw[2:3])
            y = y * jax.nn.sigmoid(y)
            if scale is not None:
                y = y * lax.rsqrt(jnp.sum(y * y, axis=-1, keepdims=True) + EPS) * scale
            dst[hd, pl.ds(r, rbs), :] = y
            return carry

        lax.fori_loop(0, nrb, blk, 0, unroll=min(2, nrb))

    for hd in range(hb):
        conv_silu_norm(q_ref, cwq_ref, hd, q_s, DK_C ** -0.5)
        conv_silu_norm(k_ref, cwk_ref, hd, k_s, 1.0)
        conv_silu_norm(v_ref, cwv_ref, hd, v_s, None)

    def gate_blk(i, carry):
        r = pl.multiple_of(i * rbs, rbs)
        ab = ab_ref[0, pl.ds(r, rbs), :]
        xa = ab + dtb_ref[...]
        softplus = jnp.maximum(xa, 0.0) + jnp.log1p(jnp.exp(-jnp.abs(xa)))
        lane_t = lax.broadcasted_iota(jnp.int32, ab.shape, 1)
        gates_s[pl.ds(r, rbs), :] = jnp.where(lane_t < 2 * N_HEADS_C, -jnp.exp(alog_ref[...]) * softplus,
                                              jax.nn.sigmoid(ab))
        return carry

    lax.fori_loop(0, nrb, gate_blk, 0, unroll=min(2, nrb))

    ii = lax.broadcasted_iota(jnp.int32, (c, c), 0)
    jj = lax.broadcasted_iota(jnp.int32, (c, c), 1)
    eye = jnp.where(ii == jj, 1.0, 0.0)
    ones_cc = jnp.ones((c, c), BF16)
    bd8 = (ii // 8) == (jj // 8)
    offs = []
    blk = 8
    while blk < c:
        offs.append((blk, ((ii // (2 * blk)) == (jj // (2 * blk))) & ((ii // blk) != (jj // blk))))
        blk *= 2
    dir_masks = []
    for d in range(2):
        if d == 0:
            incl, strict, incl_t = ii >= jj, ii > jj, ii <= jj
        else:
            incl, strict, incl_t = ii <= jj, ii < jj, ii >= jj
        dir_masks.append((incl, strict, incl_t, jnp.where(incl, 1.0, 0.0).astype(BF16)))
    pick_r = lax.broadcasted_iota(jnp.int32, (LANES, 4 * LANES), 0)
    pick_c = lax.broadcasted_iota(jnp.int32, (LANES, 4 * LANES), 1)

    n_cgroups = n // cg

    def prep(gi, carry):
        hg0 = (gi // n_cgroups) * hg
        c0 = (gi % n_cgroups) * cg
        chains = []
        for hh in range(hg):
            hd = hg0 + hh
            pick = jnp.where(pick_r == (pick_c // LANES) * N_HEADS_C + h0 + hd, 1.0, 0.0).astype(BF16)
            for cc in range(cg):
                ci = c0 + cc
                r0 = pl.multiple_of(ci * c, c)
                qc = q_s[hd, pl.ds(r0, c), :]
                kc = k_s[hd, pl.ds(r0, c), :]
                vc = v_s[hd, pl.ds(r0, c), :]
                qb, kb = qc.astype(BF16), kc.astype(BF16)
                kk = _dot_nt(kb, kb)
                qk = _dot_nt(qb, kb)
                picked = _dot2_rhs01(gates_s[pl.ds(r0, c), :], pick)
                for d in range(2):
                    chains.append(dict(hd=hd, ci=ci, r0=r0, d=d, qc=qc, kc=kc, vc=vc, kk=kk, qk=qk,
                                       la=picked[:, d * LANES:(d + 1) * LANES],
                                       be=picked[:, (2 + d) * LANES:(3 + d) * LANES]))
        for ch in chains:
            incl, strict, incl_t, cum_mask = dir_masks[ch["d"]]
            ch["g_col"] = _dot2_lhs01(cum_mask, ch["la"])
            ch["g_row"] = _dot2_lhs01(ones_cc, jnp.where(incl_t, ch["la"], 0.0))
        for ch in chains:
            incl, strict, incl_t, cum_mask = dir_masks[ch["d"]]
            ch["decay"] = jnp.where(incl, jnp.exp(jnp.where(incl, ch["g_col"] - ch["g_row"], 0.0)), 0.0)
            ch["a"] = jnp.where(strict, (ch["kk"] * ch["decay"]) * ch["be"], 0.0)
            ch["p8"] = jnp.where(bd8, -ch["a"], 0.0)
        for ch, pk2 in zip(chains, _pair_dots([ch["p8"] for ch in chains], [ch["p8"] for ch in chains])):
            ch["pk2"] = pk2
        for ch in chains:
            ch["acc"] = eye + ch["p8"]
            ch["r"] = _dot(ch["pk2"].astype(BF16), jnp.concatenate([ch["pk2"], ch["acc"]], axis=1).astype(BF16))
        for ch in chains:
            ch["acc"] = ch["acc"] + ch["r"][:, c:]
        for ch, x in zip(chains, _pair_dots([ch["r"][:, :c] for ch in chains], [ch["acc"] for ch in chains])):
            ch["t"] = ch["acc"] + x
        for blk, off in offs:
            a_sel = [_row_blocks(jnp.where(off, ch["a"], 0.0), blk, ch["d"] == 1) for ch in chains]
            for ch, y in zip(chains, _pair_dots(a_sel, [ch["t"] for ch in chains])):
                ch["y"] = _row_blocks_put(y, jnp.zeros((c, c), F32), blk, ch["d"] == 1)
            t_sel = [_row_blocks(ch["t"], blk, ch["d"] == 1) for ch in chains]
            for ch, ts, x in zip(chains, t_sel, _pair_dots(t_sel, [ch["y"] for ch in chains])):
                ch["t"] = _row_blocks_put(ts - x, ch["t"], blk, ch["d"] == 1)
        for ch in chains:
            ch["eg"] = jnp.exp(ch["g_col"])
            rhs = jnp.concatenate([ch["vc"] * ch["be"], ch["kc"] * ch["be"] * ch["eg"]], axis=1)
            ch["uw"] = _dot2_lhs_bf16(ch["t"], rhs)
        for ch in chains:
            d, r0, ci, g_col = ch["d"], ch["r0"], ch["ci"], ch["g_col"]
            hx = ch["hd"] * 2 + d
            g_last = g_col[c - 1:c, :] if d == 0 else g_col[0:1, :]
            u_s[hx, pl.ds(r0, c), :] = ch["uw"][:, :c]
            r2 = pl.multiple_of(ci * 2 * c, 2 * c)
            wq_s[hx, pl.ds(r2, c), :] = ch["uw"][:, c:].astype(BF16)
            wq_s[hx, pl.ds(r2 + c, c), :] = (ch["qc"] * ch["eg"]).astype(BF16)
            kt_s[hx, pl.ds(r0, c), :] = (ch["kc"] * jnp.exp(g_last - g_col)).astype(BF16)
            att_s[hx, pl.ds(r0, c), :] = (ch["qk"] * ch["decay"]).astype(BF16)
            r8 = pl.multiple_of(ci * 8, 8)
            el_s[hx, pl.ds(r8, 8), :] = jnp.broadcast_to(jnp.exp(g_last), (8, LANES))
        return carry

    lax.fori_loop(0, (hb // hg) * n_cgroups, prep, 0)

    for hx in range(2 * hb):
        if has_s0:
            s_s[hx] = s0_ref[0, hx % 2, hx // 2]
        else:
            s_s[hx] = jnp.zeros((DK_C, DV_C), F32)

    def scan(i, carry):
        items = []
        for hx in range(2 * hb):
            ci = i if hx % 2 == 0 else n - 1 - i
            items.append(dict(hx=hx, r0=pl.multiple_of(ci * c, c), r2=pl.multiple_of(ci * 2 * c, 2 * c),
                              r8=pl.multiple_of(ci * 8, 8), s=s_s[hx]))
        for it in items:
            it["ws"] = _dot(wq_s[it["hx"], pl.ds(it["r2"], 2 * c), :], it["s"].astype(BF16))
        for it in items:
            it["vb"] = (u_s[it["hx"], pl.ds(it["r0"], c), :] - it["ws"][:c]).astype(BF16)
        for it in items:
            hx = it["hx"]
            o_s[hx, pl.ds(it["r0"], c), :] = it["ws"][c:] + _dot(att_s[hx, pl.ds(it["r0"], c), :], it["vb"])
            s_s[hx] = (it["s"] * el_s[hx, pl.ds(it["r8"], 1), :]
                       + _dot_tn(kt_s[hx, pl.ds(it["r0"], c), :], it["vb"]))
        return carry

    lax.fori_loop(0, n, scan, 0)
    if emit_state:
        for hx in range(2 * hb):
            sfin_ref[0, hx % 2, hx // 2] = s_s[hx]

    for hd in range(hb):
        def out_blk(i, carry, hd=hd):
            r = pl.multiple_of(i * rbs, rbs)
            o = o_s[2 * hd, pl.ds(r, rbs), :] + o_s[2 * hd + 1, pl.ds(r, rbs), :]
            z = z_ref[0, pl.ds(r, rbs), hd * LANES:(hd + 1) * LANES]
            y = _rms_rows(o, onorm_ref[...]) * (z * jax.nn.sigmoid(z))
            y_ref[0, pl.ds(r, rbs), hd * LANES:(hd + 1) * LANES] = y.astype(y_ref.dtype)
            return carry

        lax.fori_loop(0, nrb, out_blk, 0, unroll=min(2, nrb))


def _delta_mixer(qkv, z, ab, conv_w, alog_row, dtb_row, out_norm, s0, hb, emit_state):
    b, t, _ = qkv.shape
    nh = N_HEADS_C
    c = DELTA_CHUNK
    n = t // c
    cg = min(n, 4)
    hg = min(hb, max(1, DELTA_PREP_CHAINS // (2 * cg)))
    assert hb % hg == 0 and n % cg == 0 and nh % hb == 0
    w = hb * LANES
    has_s0 = s0 is not None
    col = lambda off: pl.BlockSpec((1, t, w), lambda i, j, off=off: (i, 0, off + j))
    cw = lambda off: pl.BlockSpec((3, w), lambda i, j, off=off: (0, off + j))
    row = pl.BlockSpec((1, LANES), lambda i, j: (0, 0))
    st = pl.BlockSpec((1, 2, hb, DK_C, DV_C), lambda i, j: (i, 0, j, 0, 0))
    nb = nh // hb
    in_specs = [col(0), col(nb), col(2 * nb), cw(0), cw(nb), cw(2 * nb),
                pl.BlockSpec((1, t, LANES), lambda i, j: (i, 0, 0)), row, row, col(0), row]
    args = [qkv, qkv, qkv, conv_w, conv_w, conv_w, ab, alog_row, dtb_row, z, out_norm.reshape(1, DV_C)]
    if has_s0:
        in_specs.append(st)
        args.append(s0)
    out_specs = [col(0)]
    out_shape = [jax.ShapeDtypeStruct((b, t, nh * DV_C), BF16)]
    if emit_state:
        out_specs.append(st)
        out_shape.append(jax.ShapeDtypeStruct((b, 2, nh, DK_C, DV_C), F32))
    hd2 = 2 * hb
    scratch = [pltpu.VMEM((t + 16, LANES), F32),
               pltpu.VMEM((hb, t, LANES), F32), pltpu.VMEM((hb, t, LANES), F32), pltpu.VMEM((hb, t, LANES), F32),
               pltpu.VMEM((t, LANES), F32),
               pltpu.VMEM((hd2, t, LANES), F32), pltpu.VMEM((hd2, 2 * t, LANES), BF16),
               pltpu.VMEM((hd2, t, LANES), BF16), pltpu.VMEM((hd2, t, LANES), BF16),
               pltpu.VMEM((hd2, 8 * n, LANES), F32), pltpu.VMEM((hd2, t, LANES), F32),
               pltpu.VMEM((hd2, DK_C, DV_C), F32)]
    return pl.pallas_call(
        functools.partial(_delta_kernel, hb=hb, hg=hg, cg=cg, has_s0=has_s0, emit_state=emit_state),
        grid=(b, nb),
        in_specs=in_specs, out_specs=out_specs, out_shape=out_shape, scratch_shapes=scratch,
        compiler_params=pltpu.CompilerParams(dimension_semantics=("arbitrary", "arbitrary"),
                                             vmem_limit_bytes=DELTA_VMEM_LIMIT),
        name="delta_mixer",
    )(*args)


def _pad_lanes(x, n):
    return jnp.pad(x, ((0, 0), (0, n - x.shape[1])))


def kernel(x_prompt, x_sample, c, cache_l0_k, cache_l0_v, state_l1, c_ctx, l0_mod_w, l0_mod_b, l0_norm1, l0_w_in, l0_q_norm_a, l0_k_norm_a, l0_q_norm_b, l0_k_norm_b, l0_rel_bias, l0_w_out, l0_norm2, l0_mlp_w1, l0_mlp_w2, l1_mod_w, l1_mod_b, l1_norm1, l1_w_in, l1_conv_w, l1_a_log, l1_dt_bias, l1_out_norm, l1_w_out, l1_norm2, l1_mlp_w1, l1_mlp_w2):
    bp, tp, d = x_prompt.shape
    bs, ts, _ = x_sample.shape
    n_cache = (N_KV_A + N_HEADS_B) * HEAD_DIM
    bf = lambda w: w.astype(BF16)

    n_rows = -(-(1 + bs) // 8) * 8
    cvec = jnp.concatenate([c_ctx[None, :], c, jnp.zeros((n_rows - 1 - bs, d), F32)], axis=0)
    mods = []
    for mw, mb in ((l0_mod_w, l0_mod_b), (l1_mod_w, l1_mod_b)):
        m = _modulation(cvec, mw, mb).reshape(n_rows, N_MOD, d)
        mods.append((m[0:1], m[1:1 + bs]))

    xp = x_prompt.reshape(1, bp * tp, d)
    xs = x_sample

    scale = HEAD_DIM ** -0.5 * LOG2_E
    ones_a = jnp.ones((N_KV_A * HEAD_DIM,), F32)
    ones_b = jnp.ones((N_HEADS_B * HEAD_DIM,), F32)
    gain_row = jnp.concatenate([jnp.tile(l0_q_norm_a, N_HEADS_A) * scale, jnp.tile(l0_k_norm_a, N_KV_A), ones_a,
                                jnp.tile(l0_q_norm_b, N_HEADS_B) * scale, jnp.tile(l0_k_norm_b, N_HEADS_B),
                                ones_b])[None, :]
    w_in0 = bf(l0_w_in)
    w_out0 = bf(l0_w_out)
    w1_0, w2_0 = bf(l0_mlp_w1), bf(l0_mlp_w2)
    half = N_HEADS_A * HEAD_DIM

    mod_p, mod_s = mods[0]
    (qkv_p,) = _project(xp, mod_p, l0_norm1, [w_in0], tm=512)
    o_p, new_k, new_v = _ctx_attention(qkv_p.reshape(bp, tp, -1), gain_row)
    xp = _mix_mlp(xp, mod_p, l0_norm2, [o_p.reshape(1, bp * tp, -1)], [w_out0], w1_0, w2_0, tm=512, tf=1024)

    (qkv_s,) = _project(xs, mod_s, l0_norm1, [w_in0], tm=512)
    ck = cache_l0_k.reshape(bs, -1, n_cache)
    cv = cache_l0_v.reshape(bs, -1, n_cache)
    o_a = _latent_attention_a(qkv_s, ck, cv, gain_row, tq=256)
    o_b = _latent_attention_b(qkv_s, ck, cv, gain_row, l0_rel_bias)
    xs = _mix_mlp(xs, mod_s, l0_norm2, [o_a, o_b], [bf(l0_w_out[:half]), bf(l0_w_out[half:])], w1_0, w2_0,
                  tm=512, tf=1024)

    n_qkv = N_HEADS_C * (2 * DK_C + DV_C)
    n_z = N_HEADS_C * DV_C
    w_pieces = [bf(l1_w_in[:, :n_qkv]), bf(l1_w_in[:, n_qkv:n_qkv + n_z]),
                bf(_pad_lanes(l1_w_in[:, n_qkv + n_z:], LANES))]
    w_out1 = bf(l1_w_out)
    w1_1, w2_1 = bf(l1_mlp_w1), bf(l1_mlp_w2)
    alog_row = _pad_lanes(l1_a_log.reshape(1, -1), LANES)
    dtb_row = _pad_lanes(l1_dt_bias.reshape(1, -1), LANES)

    mod_p, mod_s = mods[1]
    qkv1_p, z_p, ab_p = _project(xp, mod_p, l1_norm1, w_pieces, tm=256)
    y_p, new_s = _delta_mixer(qkv1_p.reshape(bp, tp, -1), z_p.reshape(bp, tp, -1), ab_p.reshape(bp, tp, -1),
                              l1_conv_w, alog_row, dtb_row, l1_out_norm, None, hb=DELTA_HEADS_PROMPT, emit_state=True)
    xp = _mix_mlp(xp, mod_p, l1_norm2, [y_p.reshape(1, bp * tp, -1)], [w_out1], w1_1, w2_1, tm=512, tf=1024)

    qkv1_s, z_s, ab_s = _project(xs, mod_s, l1_norm1, w_pieces, tm=256)
    (y_s,) = _delta_mixer(qkv1_s, z_s, ab_s, l1_conv_w, alog_row, dtb_row, l1_out_norm, state_l1.astype(F32),
                          hb=DELTA_HEADS_SAMPLE, emit_state=False)
    xs = _mix_mlp(xs, mod_s, l1_norm2, [y_s], [w_out1], w1_1, w2_1, tm=512, tf=1024)

    return (xp.reshape(bp, tp, d), xs,
            new_k.reshape(bp, tp, N_KV_A + N_HEADS_B, HEAD_DIM), new_v.reshape(bp, tp, N_KV_A + N_HEADS_B, HEAD_DIM),
            new_s.astype(x_prompt.dtype))
```

```python
import functools
import math

import jax
import jax.numpy as jnp
from jax import lax
from jax.experimental import pallas as pl
from jax.experimental.pallas import tpu as pltpu

F32 = jnp.float32
BF16 = jnp.bfloat16

EPS = 1e-6
N_MOD = 6
HEAD_DIM = 64
N_HEADS_A = 8
N_KV_A = 2
N_HEADS_B = 8
GRID_W = 64
WIN_R = 8
WIN_C = 16
ROPE_THETA = 10000.0
N_HEADS_C = 8
DK_C = 128
DV_C = 128
NEG_BIG = -1e30
LOG2_E = math.log2(math.e)

LANES = 128
VMEM_LIMIT = 48 * 1024 * 1024
DELTA_VMEM_LIMIT = 56 * 1024 * 1024
PROJ_ROWS_L0 = 1024
PROJ_ROWS_L1 = 512
MLP_ROWS = 1024
MLP_FF_TILE = 1024
LAT_A_QUERY_TILE = 512
LAT_A_JOBS = 2
NBR_ROW_GROUP = 16
DELTA_CHUNK = LANES
DELTA_PREP_CHAINS = 16
DELTA_ROW_BLOCK = 256
DELTA_HEADS_PROMPT = 8
DELTA_HEADS_SAMPLE = 2


def _cparams(sem, limit=VMEM_LIMIT):
    return pltpu.CompilerParams(dimension_semantics=sem, vmem_limit_bytes=limit)


def _dot(a, b):
    return jnp.dot(a, b, preferred_element_type=F32)


def _dot_nt(a, b):
    return lax.dot_general(a, b, (((1,), (1,)), ((), ())), preferred_element_type=F32)


def _dot_tn(a, b):
    return lax.dot_general(a, b, (((0,), (0,)), ((), ())), preferred_element_type=F32)


def _split2(x):
    hi = x.astype(BF16)
    lo = (x - hi.astype(F32)).astype(BF16)
    return hi, lo


def _dot2_rhs01(a_f32, b01):
    hi, lo = _split2(a_f32)
    return _dot(jnp.concatenate([hi, lo], axis=1), jnp.concatenate([b01, b01], axis=0))


def _dot2_lhs01(a01, b_f32):
    hi, lo = _split2(b_f32)
    return _dot(jnp.concatenate([a01, a01], axis=1), jnp.concatenate([hi, lo], axis=0))


def _dot2_lhs_bf16(a_f32, b_f32):
    ab = a_f32.astype(BF16)
    hi, lo = _split2(b_f32)
    return _dot(jnp.concatenate([ab, ab], axis=1), jnp.concatenate([hi, lo], axis=0))


def _pair_dots(lhs, rhs):
    outs = []
    for i in range(0, len(lhs), 2):
        r0, r1 = rhs[i].astype(BF16), rhs[i + 1].astype(BF16)
        z = jnp.zeros_like(r0)
        rr = jnp.concatenate([jnp.concatenate([r0, z], axis=1), jnp.concatenate([z, r1], axis=1)], axis=0)
        x = _dot(jnp.concatenate([lhs[i].astype(BF16), lhs[i + 1].astype(BF16)], axis=1), rr)
        outs += [x[:, :LANES], x[:, LANES:]]
    return outs


def _row_blocks(x, blk, first):
    return jnp.concatenate([x[i * blk:(i + 1) * blk] for i in range(0 if first else 1, x.shape[0] // blk, 2)], axis=0)


def _row_blocks_put(sel, rest, blk, first):
    pieces = []
    for i in range(rest.shape[0] // blk):
        if (i % 2 == 0) == first:
            pieces.append(sel[(i // 2) * blk:(i // 2 + 1) * blk])
        else:
            pieces.append(rest[i * blk:(i + 1) * blk])
    return jnp.concatenate(pieces, axis=0)


def _rms_rows(x, gain):
    ms = jnp.mean(x * x, axis=-1, keepdims=True)
    return x * lax.rsqrt(ms + EPS) * gain


def _mod_kernel(c_ref, w_ref, b_ref, o_ref):
    c = c_ref[...]
    a = (c * jax.nn.sigmoid(c)).astype(BF16)
    o_ref[...] = _dot(a, w_ref[...].astype(BF16)) + b_ref[...]


def _modulation(cvec, w, b):
    rows, d = cvec.shape
    n = w.shape[1]
    tn = n // 4
    return pl.pallas_call(
        _mod_kernel,
        grid=(n // tn,),
        in_specs=[pl.BlockSpec((rows, d), lambda j: (0, 0)),
                  pl.BlockSpec((d, tn), lambda j: (0, j)),
                  pl.BlockSpec((1, tn), lambda j: (0, j))],
        out_specs=pl.BlockSpec((rows, tn), lambda j: (0, j)),
        out_shape=jax.ShapeDtypeStruct((rows, n), F32),
        compiler_params=_cparams(("arbitrary",)),
        name="modulation",
    )(cvec, w, b.reshape(1, n))


def _proj_kernel(x_ref, mod_ref, g_ref, *rest, n_out):
    w_refs, o_refs = rest[:n_out], rest[n_out:]
    mod = mod_ref[0]
    h = _rms_rows(x_ref[0], g_ref[...]) * (1.0 + mod[1:2]) + mod[0:1]
    hb = h.astype(BF16)
    for w_ref, o_ref in zip(w_refs, o_refs):
        o_ref[0] = _dot(hb, w_ref[...]).astype(o_ref.dtype)


def _project(x, mod, gain, weights, tm):
    g, t, d = x.shape
    n_out = len(weights)
    in_specs = [pl.BlockSpec((1, tm, d), lambda i, j: (i, j, 0)),
                pl.BlockSpec((1, N_MOD, d), lambda i, j: (i, 0, 0)),
                pl.BlockSpec((1, d), lambda i, j: (0, 0))]
    in_specs += [pl.BlockSpec(w.shape, lambda i, j: (0, 0)) for w in weights]
    out_specs = [pl.BlockSpec((1, tm, w.shape[1]), lambda i, j: (i, j, 0)) for w in weights]
    out_shape = [jax.ShapeDtypeStruct((g, t, w.shape[1]), F32) for w in weights]
    return pl.pallas_call(
        functools.partial(_proj_kernel, n_out=n_out),
        grid=(g, t // tm),
        in_specs=in_specs, out_specs=out_specs, out_shape=out_shape,
        compiler_params=_cparams(("arbitrary", "arbitrary")),
        name="norm_project",
    )(x, mod, gain.reshape(1, d), *weights)


def _mlp_kernel(*refs, n_mix):
    x_ref, mod_ref, g_ref = refs[:3]
    o_refs = refs[3:3 + n_mix]
    wo_refs = refs[3 + n_mix:3 + 2 * n_mix]
    w1_ref, w2_ref, out_ref, x1_s, h_s, acc_s = refs[3 + 2 * n_mix:]
    f = pl.program_id(2)

    @pl.when(f == 0)
    def _():
        mod = mod_ref[0]
        mp = _dot(o_refs[0][0].astype(BF16), wo_refs[0][...])
        for o_ref, wo_ref in zip(o_refs[1:], wo_refs[1:]):
            mp += _dot(o_ref[0].astype(BF16), wo_ref[...])
        x1 = x_ref[0] + mod[2:3] * mp
        x1_s[...] = x1
        h = _rms_rows(x1, g_ref[...]) * (1.0 + mod[4:5]) + mod[3:4]
        h_s[...] = h.astype(BF16)
        acc_s[...] = jnp.zeros_like(acc_s)

    a = _dot(h_s[...], w1_ref[...])
    a = jnp.square(jnp.maximum(a, 0.0))
    acc_s[...] += _dot(a.astype(BF16), w2_ref[...])

    @pl.when(f == pl.num_programs(2) - 1)
    def _():
        out_ref[0] = x1_s[...] + mod_ref[0][5:6] * acc_s[...]


def _mix_mlp(x, mod, gain2, mixes, w_outs, w1, w2):
    g, t, d = x.shape
    dff = w1.shape[1]
    tm, tf = MLP_ROWS, MLP_FF_TILE
    n_mix = len(mixes)
    in_specs = [pl.BlockSpec((1, tm, d), lambda i, j, f: (i, j, 0)),
                pl.BlockSpec((1, N_MOD, d), lambda i, j, f: (i, 0, 0)),
                pl.BlockSpec((1, d), lambda i, j, f: (0, 0))]
    in_specs += [pl.BlockSpec((1, tm, o.shape[2]), lambda i, j, f: (i, j, 0)) for o in mixes]
    in_specs += [pl.BlockSpec(w.shape, lambda i, j, f: (0, 0)) for w in w_outs]
    in_specs += [pl.BlockSpec((d, tf), lambda i, j, f: (0, f)),
                 pl.BlockSpec((tf, d), lambda i, j, f: (f, 0))]
    return pl.pallas_call(
        functools.partial(_mlp_kernel, n_mix=n_mix),
        grid=(g, t // tm, dff // tf),
        in_specs=in_specs,
        out_specs=pl.BlockSpec((1, tm, d), lambda i, j, f: (i, j, 0)),
        out_shape=jax.ShapeDtypeStruct((g, t, d), F32),
        scratch_shapes=[pltpu.VMEM((tm, d), F32), pltpu.VMEM((tm, d), BF16), pltpu.VMEM((tm, d), F32)],
        compiler_params=_cparams(("arbitrary", "arbitrary", "arbitrary")),
        name="mix_mlp",
    )(x, mod, gain2.reshape(1, d), *mixes, *w_outs, w1, w2)


def _pair_ones():
    r = lax.broadcasted_iota(jnp.int32, (LANES, LANES), 0) // HEAD_DIM
    c = lax.broadcasted_iota(jnp.int32, (LANES, LANES), 1) // HEAD_DIM
    return jnp.where(r == c, 1.0, 0.0).astype(BF16)


def _head_rms(xb, gain, pair_ones):
    hi, lo = _split2(xb * xb)
    ss = _dot(jnp.concatenate([hi, lo], axis=1), jnp.concatenate([pair_ones, pair_ones], axis=0))
    return xb * lax.rsqrt(ss * (1.0 / HEAD_DIM) + EPS) * gain


def _dup_half(x, g):
    lane = lax.broadcasted_iota(jnp.int32, x.shape, 1)
    sw = pltpu.roll(x, HEAD_DIM, 1)
    if g == 0:
        return jnp.where(lane < HEAD_DIM, x, sw)
    return jnp.where(lane < HEAD_DIM, sw, x)


def _softmax_parts(scores):
    m = jnp.max(scores[0], axis=-1, keepdims=True)
    for s in scores[1:]:
        m = jnp.maximum(m, jnp.max(s, axis=-1, keepdims=True))
    ps = [jnp.exp2(s - m) for s in scores]
    l = jnp.sum(ps[0], axis=-1, keepdims=True)
    for p in ps[1:]:
        l += jnp.sum(p, axis=-1, keepdims=True)
    return ps, 1.0 / l


def _attend_pairs(jobs):
    tasks = []
    for qb, ks, vs, biases in jobs:
        lane = lax.broadcasted_iota(jnp.int32, qb.shape, 1)
        for hh in range(2):
            sel = (lane < HEAD_DIM) if hh == 0 else (lane >= HEAD_DIM)
            tasks.append(dict(q=jnp.where(sel, qb, jnp.zeros_like(qb)), ks=ks, vs=vs,
                              bias=None if biases is None else biases[hh]))
    for tk in tasks:
        scores = [_dot_nt(tk["q"], k) for k in tk["ks"]]
        if tk["bias"] is not None:
            scores = [s if b is None else s + b for s, b in zip(scores, tk["bias"])]
        tk["s"] = scores
    for tk in tasks:
        tk["p"], tk["inv"] = _softmax_parts(tk["s"])
    for tk in tasks:
        o = _dot(tk["p"][0].astype(BF16), tk["vs"][0])
        for p, v in zip(tk["p"][1:], tk["vs"][1:]):
            o += _dot(p.astype(BF16), v)
        tk["o"] = o * tk["inv"]
    outs = []
    for j in range(len(jobs)):
        o0, o1 = tasks[2 * j]["o"], tasks[2 * j + 1]["o"]
        lane_o = lax.broadcasted_iota(jnp.int32, o0.shape, 1)
        outs.append(jnp.where(lane_o < HEAD_DIM, o0, o1))
    return outs


_QA0, _KA, _VA, _QB0, _KB0, _VB0 = 0, 4, 5, 6, 10, 14


def _ctx_attn_kernel(qkv_ref, gain_ref, o_ref, nk_ref, nv_ref):
    ones = _pair_ones()

    def blk(j):
        return qkv_ref[0, :, j * LANES:(j + 1) * LANES]

    def gain(j):
        return gain_ref[:, j * LANES:(j + 1) * LANES]

    ka = _head_rms(blk(_KA), gain(_KA), ones)
    va = blk(_VA)
    nk_ref[0, :, 0:LANES] = ka
    nv_ref[0, :, 0:LANES] = va
    k2 = [_dup_half(ka, g).astype(BF16) for g in range(N_KV_A)]
    v2 = [_dup_half(va, g).astype(BF16) for g in range(N_KV_A)]
    jobs = []
    for j in range(4):
        qn = _head_rms(blk(_QA0 + j), gain(_QA0 + j), ones).astype(BF16)
        jobs.append((qn, [k2[j // 2]], [v2[j // 2]], None))
    for j, o in enumerate(_attend_pairs(jobs)):
        o_ref[0, :, j * LANES:(j + 1) * LANES] = o.astype(o_ref.dtype)
    jobs = []
    for j in range(4):
        kb = _head_rms(blk(_KB0 + j), gain(_KB0 + j), ones)
        vb = blk(_VB0 + j)
        nk_ref[0, :, (1 + j) * LANES:(2 + j) * LANES] = kb
        nv_ref[0, :, (1 + j) * LANES:(2 + j) * LANES] = vb
        qn = _head_rms(blk(_QB0 + j), gain(_QB0 + j), ones).astype(BF16)
        jobs.append((qn, [kb.astype(BF16)], [vb.astype(BF16)], None))
    for j, o in enumerate(_attend_pairs(jobs)):
        o_ref[0, :, (4 + j) * LANES:(5 + j) * LANES] = o.astype(o_ref.dtype)


def _ctx_attention(qkv, gain_row):
    b, t, n = qkv.shape
    n_cache = (N_KV_A + N_HEADS_B) * HEAD_DIM
    n_o = (N_HEADS_A + N_HEADS_B) * HEAD_DIM
    return pl.pallas_call(
        _ctx_attn_kernel,
        grid=(b,),
        in_specs=[pl.BlockSpec((1, t, n), lambda i: (i, 0, 0)),
                  pl.BlockSpec((1, n), lambda i: (0, 0))],
        out_specs=[pl.BlockSpec((1, t, n_o), lambda i: (i, 0, 0)),
                   pl.BlockSpec((1, t, n_cache), lambda i: (i, 0, 0)),
                   pl.BlockSpec((1, t, n_cache), lambda i: (i, 0, 0))],
        out_shape=[jax.ShapeDtypeStruct((b, t, n_o), BF16),
                   jax.ShapeDtypeStruct((b, t, n_cache), F32),
                   jax.ShapeDtypeStruct((b, t, n_cache), F32)],
        compiler_params=_cparams(("arbitrary",)),
        name="ctx_attention",
    )(qkv, gain_row)


def _rope(x, c, sa, sb):
    return x * c + pltpu.roll(x, LANES - 16, 1) * sa + pltpu.roll(x, 16, 1) * sb


def _lat_a_kernel(q_ref, k_ref, v_ref, ck_ref, cv_ref, gq_ref, gk_ref,
                  cq_ref, saq_ref, sbq_ref, ck_t_ref, sak_t_ref, sbk_t_ref,
                  o_ref, k2_s, v2_s):
    t = k_ref.shape[1]
    ones = _pair_ones()

    @pl.when(pl.program_id(1) == 0)
    def _():
        kn = _head_rms(k_ref[0], gk_ref[...], ones)
        kr = _rope(kn, ck_t_ref[...], sak_t_ref[...], sbk_t_ref[...])
        v = v_ref[0]
        ck = ck_ref[0]
        cv = cv_ref[0]
        for g in range(N_KV_A):
            k2_s[g, 0:t, :] = _dup_half(kr, g).astype(BF16)
            k2_s[g, t:, :] = _dup_half(ck, g).astype(BF16)
            v2_s[g, 0:t, :] = _dup_half(v, g).astype(BF16)
            v2_s[g, t:, :] = _dup_half(cv, g).astype(BF16)

    c, sa, sb = cq_ref[...], saq_ref[...], sbq_ref[...]
    for j0 in range(0, 4, LAT_A_JOBS):
        jobs = []
        for j in range(j0, j0 + LAT_A_JOBS):
            qn = _head_rms(q_ref[0, :, j * LANES:(j + 1) * LANES], gq_ref[:, j * LANES:(j + 1) * LANES], ones)
            jobs.append((_rope(qn, c, sa, sb).astype(BF16), [k2_s[j // 2]], [v2_s[j // 2]], None))
        for j, o in zip(range(j0, j0 + LAT_A_JOBS), _attend_pairs(jobs)):
            o_ref[0, :, j * LANES:(j + 1) * LANES] = o.astype(o_ref.dtype)


def _rope_tables(t):
    pos = jnp.arange(t, dtype=jnp.int32)
    rc = jnp.stack([pos // GRID_W, pos % GRID_W], axis=-1).astype(F32)
    axis_dim = HEAD_DIM // 2
    inv_freq = 1.0 / (ROPE_THETA ** (jnp.arange(0, axis_dim, 2, dtype=F32) / axis_dim))
    ang = rc[:, :, None] * inv_freq
    cos, sin = jnp.cos(ang), jnp.sin(ang)
    zero = jnp.zeros_like(sin)
    c = jnp.stack([cos, cos], axis=2).reshape(t, HEAD_DIM)
    sa = jnp.stack([-sin, zero], axis=2).reshape(t, HEAD_DIM)
    sb = jnp.stack([zero, sin], axis=2).reshape(t, HEAD_DIM)
    two = lambda a: jnp.concatenate([a, a], axis=-1)
    return two(c), two(sa), two(sb)


def _latent_attention_a(qkv, cache_k, cache_v, gain_row):
    b, t, _ = qkv.shape
    p = cache_k.shape[1]
    tq = LAT_A_QUERY_TILE
    c, sa, sb = _rope_tables(t)
    gq = gain_row[:, _QA0 * LANES:(_QA0 + 4) * LANES]
    gk = gain_row[:, _KA * LANES:(_KA + 1) * LANES]
    tab_q = pl.BlockSpec((tq, LANES), lambda i, j: (j, 0))
    tab_k = pl.BlockSpec((t, LANES), lambda i, j: (0, 0))
    return pl.pallas_call(
        _lat_a_kernel,
        grid=(b, t // tq),
        in_specs=[pl.BlockSpec((1, tq, 4 * LANES), lambda i, j: (i, j, 0)),
                  pl.BlockSpec((1, t, LANES), lambda i, j: (i, 0, _KA)),
                  pl.BlockSpec((1, t, LANES), lambda i, j: (i, 0, _VA)),
                  pl.BlockSpec((1, p, LANES), lambda i, j: (i, 0, 0)),
                  pl.BlockSpec((1, p, LANES), lambda i, j: (i, 0, 0)),
                  pl.BlockSpec((1, 4 * LANES), lambda i, j: (0, 0)),
                  pl.BlockSpec((1, LANES), lambda i, j: (0, 0)),
                  tab_q, tab_q, tab_q, tab_k, tab_k, tab_k],
        out_specs=pl.BlockSpec((1, tq, 4 * LANES), lambda i, j: (i, j, 0)),
        out_shape=jax.ShapeDtypeStruct((b, t, 4 * LANES), BF16),
        scratch_shapes=[pltpu.VMEM((N_KV_A, t + p, LANES), BF16), pltpu.VMEM((N_KV_A, t + p, LANES), BF16)],
        compiler_params=_cparams(("arbitrary", "arbitrary")),
        name="latent_attention_a",
    )(qkv, qkv, qkv, cache_k, cache_v, gq, gk, c, sa, sb, c, sa, sb)


def _nbr_bias_table(rel_bias, rows):
    wr = min(WIN_R, rows)
    cols = jnp.arange(GRID_W, dtype=jnp.int32)
    col_start = jnp.clip(cols - WIN_C // 2, 0, GRID_W - WIN_C)
    kc = jnp.arange(GRID_W, dtype=jnp.int32)
    in_win = (kc[None, :] >= col_start[:, None]) & (kc[None, :] < col_start[:, None] + WIN_C)
    nh, n_ro = rel_bias.shape[0], 2 * wr - 1
    rb = rel_bias[:, WIN_R - wr:WIN_R + wr - 1, :].astype(F32) * LOG2_E
    span = 2 * GRID_W
    lead = GRID_W - WIN_C
    v = jnp.pad(rb, ((0, 0), (0, 0), (lead, span - lead - (2 * WIN_C - 1))))
    flat = jnp.tile(v, (1, 1, GRID_W))
    toep = flat[..., GRID_W - 1:GRID_W - 1 + GRID_W * (span - 1)].reshape(nh, n_ro, GRID_W, span - 1)[..., :GRID_W]
    base = jnp.where(in_win[None, None], toep, NEG_BIG)
    return jnp.concatenate([base[:, :-1], base[:, 1:]], axis=-1)


def _nbr_kernel(q_ref, k_ref, v_ref, ck_ref, cv_ref, gq_ref, gk_ref, bias_ref, o_ref, qn_s, kn_s, vn_s):
    t = q_ref.shape[1]
    rows = t // GRID_W
    wr = min(WIN_R, rows)
    ones = _pair_ones()
    qn_s[...] = _head_rms(q_ref[0], gq_ref[0], ones).astype(BF16)
    kn_s[...] = _head_rms(k_ref[0], gk_ref[0], ones).astype(BF16)
    vn_s[...] = v_ref[0].astype(BF16)
    ck = ck_ref[0].astype(BF16)
    cv = cv_ref[0].astype(BF16)
    rb = math.gcd(rows, NBR_ROW_GROUP)

    def row_group(gi, carry):
        jobs, q0s = [], []
        for rr in range(rb):
            r = gi * rb + rr
            rs = jnp.clip(r - wr // 2, 0, rows - wr)
            var = r - rs
            q0 = pl.multiple_of(r * GRID_W, GRID_W)
            k0 = pl.multiple_of(rs * GRID_W, GRID_W)
            biases = [[jnp.concatenate([bias_ref[hh, wr - 1 - var + 2 * m] for m in range(wr // 2)], axis=-1), None]
                      for hh in range(2)]
            jobs.append((qn_s[pl.ds(q0, GRID_W), :], [kn_s[pl.ds(k0, wr * GRID_W), :], ck],
                         [vn_s[pl.ds(k0, wr * GRID_W), :], cv], biases))
            q0s.append(q0)
        for q0, o in zip(q0s, _attend_pairs(jobs)):
            o_ref[0, pl.ds(q0, GRID_W), :] = o.astype(o_ref.dtype)
        return carry

    lax.fori_loop(0, rows // rb, row_group, 0)


def _latent_attention_b(qkv, cache_k, cache_v, gain_row, rel_bias):
    b, t, _ = qkv.shape
    p = cache_k.shape[1]
    rows = t // GRID_W
    wr = min(WIN_R, rows)
    assert wr % 2 == 0
    n_tab = 2 * wr - 2
    table = _nbr_bias_table(rel_bias, rows).reshape(N_HEADS_B // 2, 2, n_tab, GRID_W, 2 * GRID_W)
    gq = gain_row[:, _QB0 * LANES:(_QB0 + 4) * LANES].reshape(4, 1, LANES)
    gk = gain_row[:, _KB0 * LANES:(_KB0 + 4) * LANES].reshape(4, 1, LANES)
    return pl.pallas_call(
        _nbr_kernel,
        grid=(b, N_HEADS_B // 2),
        in_specs=[pl.BlockSpec((1, t, LANES), lambda i, j: (i, 0, _QB0 + j)),
                  pl.BlockSpec((1, t, LANES), lambda i, j: (i, 0, _KB0 + j)),
                  pl.BlockSpec((1, t, LANES), lambda i, j: (i, 0, _VB0 + j)),
                  pl.BlockSpec((1, p, LANES), lambda i, j: (i, 0, 1 + j)),
                  pl.BlockSpec((1, p, LANES), lambda i, j: (i, 0, 1 + j)),
                  pl.BlockSpec((1, 1, LANES), lambda i, j: (j, 0, 0)),
                  pl.BlockSpec((1, 1, LANES), lambda i, j: (j, 0, 0)),
                  pl.BlockSpec((None, 2, n_tab, GRID_W, 2 * GRID_W), lambda i, j: (j, 0, 0, 0, 0))],
        out_specs=pl.BlockSpec((1, t, LANES), lambda i, j: (i, 0, j)),
        out_shape=jax.ShapeDtypeStruct((b, t, 4 * LANES), BF16),
        scratch_shapes=[pltpu.VMEM((t, LANES), BF16)] * 3,
        compiler_params=_cparams(("arbitrary", "arbitrary")),
        name="latent_attention_b",
    )(qkv, qkv, qkv, cache_k, cache_v, gq, gk, table)


def _delta_kernel(*refs, hb, hg, cg, has_s0, emit_state):
    (q_ref, k_ref, v_ref, cwq_ref, cwk_ref, cwv_ref, ab_ref, alog_ref, dtb_ref, z_ref, onorm_ref), rest = (
        refs[:11], refs[11:])
    s0_ref = None
    if has_s0:
        s0_ref, rest = rest[0], rest[1:]
    y_ref, rest = rest[0], rest[1:]
    sfin_ref = None
    if emit_state:
        sfin_ref, rest = rest[0], rest[1:]
    q_s, k_s, v_s, gates_s, u_s, wq_s, kt_s, att_s, el_s, o_s, s_s = rest
    pad_s = o_s.at[0]
    t = q_ref.shape[1]
    c = DELTA_CHUNK
    n = t // c
    h0 = pl.program_id(1) * hb
    rbs = min(t, DELTA_ROW_BLOCK)
    nrb = t // rbs
    unroll = min(2, nrb)

    pad_s[0:8, :] = jnp.zeros((8, LANES), F32)
    pad_s[t + 8:t + 16, :] = jnp.zeros((8, LANES), F32)

    def conv_silu_norm(x_ref, cw_ref, hd, dst, scale):
        for i in range(nrb):
            pad_s[8 + i * rbs:8 + (i + 1) * rbs, :] = x_ref[0, i * rbs:(i + 1) * rbs, hd * LANES:(hd + 1) * LANES]
        cw = cw_ref[:, hd * LANES:(hd + 1) * LANES]

        def blk(i, carry):
            r = pl.multiple_of(i * rbs, rbs)
            y = (pad_s[pl.ds(r + 7, rbs), :] * cw[0:1] + pad_s[pl.ds(r + 8, rbs), :] * cw[1:2]
                 + pad_s[pl.ds(r + 9, rbs), :] * cw[2:3])
            y = y * jax.nn.sigmoid(y)
            if scale is not None:
                y = y * lax.rsqrt(jnp.sum(y * y, axis=-1, keepdims=True) + EPS) * scale
            dst[hd, pl.ds(r, rbs), :] = y
            return carry

        lax.fori_loop(0, nrb, blk, 0, unroll=unroll)

    for hd in range(hb):
        conv_silu_norm(q_ref, cwq_ref, hd, q_s, DK_C ** -0.5)
        conv_silu_norm(k_ref, cwk_ref, hd, k_s, 1.0)
        conv_silu_norm(v_ref, cwv_ref, hd, v_s, None)

    def gate_blk(i, carry):
        r = pl.multiple_of(i * rbs, rbs)
        ab = ab_ref[0, pl.ds(r, rbs), :]
        xa = ab + dtb_ref[...]
        softplus = jnp.maximum(xa, 0.0) + jnp.log1p(jnp.exp(-jnp.abs(xa)))
        lane_t = lax.broadcasted_iota(jnp.int32, ab.shape, 1)
        gates_s[pl.ds(r, rbs), :] = jnp.where(lane_t < 2 * N_HEADS_C, -jnp.exp(alog_ref[...]) * softplus,
                                              jax.nn.sigmoid(ab))
        return carry

    lax.fori_loop(0, nrb, gate_blk, 0, unroll=unroll)

    ii = lax.broadcasted_iota(jnp.int32, (c, c), 0)
    jj = lax.broadcasted_iota(jnp.int32, (c, c), 1)
    eye = jnp.where(ii == jj, 1.0, 0.0)
    ones_cc = jnp.ones((c, c), BF16)
    bd8 = (ii // 8) == (jj // 8)
    offs = []
    blk = 8
    while blk < c:
        offs.append((blk, ((ii // (2 * blk)) == (jj // (2 * blk))) & ((ii // blk) != (jj // blk))))
        blk *= 2
    dir_masks = []
    for d in range(2):
        if d == 0:
            incl, strict, incl_t = ii >= jj, ii > jj, ii <= jj
        else:
            incl, strict, incl_t = ii <= jj, ii < jj, ii >= jj
        dir_masks.append((incl, strict, incl_t, jnp.where(incl, 1.0, 0.0).astype(BF16)))
    pick_r = lax.broadcasted_iota(jnp.int32, (LANES, 4 * LANES), 0)
    pick_c = lax.broadcasted_iota(jnp.int32, (LANES, 4 * LANES), 1)
    n_cgroups = n // cg

    def prep_unit(gi, carry):
        hg0 = (gi // n_cgroups) * hg
        c0 = (gi % n_cgroups) * cg
        chains = []
        for hh in range(hg):
            hd = hg0 + hh
            pick = jnp.where(pick_r == (pick_c // LANES) * N_HEADS_C + h0 + hd, 1.0, 0.0).astype(BF16)
            for cc in range(cg):
                ci = c0 + cc
                r0 = pl.multiple_of(ci * c, c)
                qb, kb = q_s[hd, pl.ds(r0, c), :].astype(BF16), k_s[hd, pl.ds(r0, c), :].astype(BF16)
                kk, qk = _dot_nt(kb, kb), _dot_nt(qb, kb)
                picked = _dot2_rhs01(gates_s[pl.ds(r0, c), :], pick)
                for d in range(2):
                    chains.append(dict(hd=hd, ci=ci, r0=r0, d=d, kk=kk, qk=qk,
                                       la=picked[:, d * LANES:(d + 1) * LANES],
                                       be=picked[:, (2 + d) * LANES:(3 + d) * LANES]))
        for ch in chains:
            incl, strict, incl_t, cum_mask = dir_masks[ch["d"]]
            ch["g_col"] = _dot2_lhs01(cum_mask, ch["la"])
            ch["g_row"] = _dot2_lhs01(ones_cc, jnp.where(incl_t, ch["la"], 0.0))
        for ch in chains:
            incl, strict, incl_t, cum_mask = dir_masks[ch["d"]]
            ch["decay"] = jnp.where(incl, jnp.exp(jnp.where(incl, ch["g_col"] - ch["g_row"], 0.0)), 0.0)
            ch["a"] = jnp.where(strict, (ch["kk"] * ch["decay"]) * ch["be"], 0.0)
            ch["p8"] = jnp.where(bd8, -ch["a"], 0.0)
        for ch, pk2 in zip(chains, _pair_dots([ch["p8"] for ch in chains], [ch["p8"] for ch in chains])):
            ch["pk2"] = pk2
        for ch in chains:
            ch["acc"] = eye + ch["p8"]
            ch["r"] = _dot(ch["pk2"].astype(BF16), jnp.concatenate([ch["pk2"], ch["acc"]], axis=1).astype(BF16))
        for ch in chains:
            ch["acc"] = ch["acc"] + ch["r"][:, c:]
        for ch, x in zip(chains, _pair_dots([ch["r"][:, :c] for ch in chains], [ch["acc"] for ch in chains])):
            ch["t"] = ch["acc"] + x
        for blk, off in offs:
            a_sel = [_row_blocks(jnp.where(off, ch["a"], 0.0), blk, ch["d"] == 1) for ch in chains]
            for ch, y in zip(chains, _pair_dots(a_sel, [ch["t"] for ch in chains])):
                ch["y"] = _row_blocks_put(y, jnp.zeros((c, c), F32), blk, ch["d"] == 1)
            t_sel = [_row_blocks(ch["t"], blk, ch["d"] == 1) for ch in chains]
            for ch, ts, x in zip(chains, t_sel, _pair_dots(t_sel, [ch["y"] for ch in chains])):
                ch["t"] = _row_blocks_put(ts - x, ch["t"], blk, ch["d"] == 1)
        for ch in chains:
            ch["eg"] = jnp.exp(ch["g_col"])
            ch["kc"] = k_s[ch["hd"], pl.ds(ch["r0"], c), :]
            vc = v_s[ch["hd"], pl.ds(ch["r0"], c), :]
            rhs = jnp.concatenate([vc * ch["be"], ch["kc"] * ch["be"] * ch["eg"]], axis=1)
            ch["uw"] = _dot2_lhs_bf16(ch["t"], rhs)
        for ch in chains:
            d, r0, ci, g_col = ch["d"], ch["r0"], ch["ci"], ch["g_col"]
            hx = ch["hd"] * 2 + d
            g_last = g_col[c - 1:c, :] if d == 0 else g_col[0:1, :]
            u_s[hx, pl.ds(r0, c), :] = ch["uw"][:, :c]
            r2 = pl.multiple_of(ci * 2 * c, 2 * c)
            wq_s[hx, pl.ds(r2, c), :] = ch["uw"][:, c:].astype(BF16)
            wq_s[hx, pl.ds(r2 + c, c), :] = (q_s[ch["hd"], pl.ds(r0, c), :] * ch["eg"]).astype(BF16)
            kt_s[hx, pl.ds(r0, c), :] = (ch["kc"] * jnp.exp(g_last - g_col)).astype(BF16)
            att_s[hx, pl.ds(r0, c), :] = (ch["qk"] * ch["decay"]).astype(BF16)
            r8 = pl.multiple_of(ci * 8, 8)
            el_s[hx, pl.ds(r8, 8), :] = jnp.broadcast_to(jnp.exp(g_last), (8, LANES))
        return carry

    def scan_step(i, carry):
        items = []
        for hx in range(2 * hb):
            ci = i if hx % 2 == 0 else n - 1 - i
            items.append(dict(hx=hx, s=s_s[hx], r0=pl.multiple_of(ci * c, c), r2=pl.multiple_of(ci * 2 * c, 2 * c),
                              r8=pl.multiple_of(ci * 8, 8)))
        for it in items:
            it["ws"] = _dot(wq_s[it["hx"], pl.ds(it["r2"], 2 * c), :], it["s"].astype(BF16))
        for it in items:
            it["vb"] = (u_s[it["hx"], pl.ds(it["r0"], c), :] - it["ws"][:c]).astype(BF16)
        for it in items:
            hx = it["hx"]
            o_s[hx // 2, pl.ds(it["r0"], c), :] += (it["ws"][c:]
                                                    + _dot(att_s[hx, pl.ds(it["r0"], c), :], it["vb"]))
            s_s[hx] = (it["s"] * el_s[hx, pl.ds(it["r8"], 1), :]
                       + _dot_tn(kt_s[hx, pl.ds(it["r0"], c), :], it["vb"]))
        return carry

    for hx in range(2 * hb):
        if has_s0:
            s_s[hx] = s0_ref[0, hx % 2, hx // 2]
        else:
            s_s[hx] = jnp.zeros((DK_C, DV_C), F32)

    def zero_blk(i, carry):
        for hd in range(hb):
            o_s[hd, pl.ds(pl.multiple_of(i * rbs, rbs), rbs), :] = jnp.zeros((rbs, LANES), F32)
        return carry

    lax.fori_loop(0, nrb, zero_blk, 0)

    lax.fori_loop(0, (hb // hg) * n_cgroups, prep_unit, 0)
    lax.fori_loop(0, n, scan_step, 0)

    if emit_state:
        for hx in range(2 * hb):
            sfin_ref[0, hx % 2, hx // 2] = s_s[hx]

    for hd in range(hb):
        def out_blk(i, carry, hd=hd):
            r = pl.multiple_of(i * rbs, rbs)
            o = o_s[hd, pl.ds(r, rbs), :]
            z = z_ref[0, pl.ds(r, rbs), hd * LANES:(hd + 1) * LANES]
            y = _rms_rows(o, onorm_ref[...]) * (z * jax.nn.sigmoid(z))
            y_ref[0, pl.ds(r, rbs), hd * LANES:(hd + 1) * LANES] = y.astype(y_ref.dtype)
            return carry

        lax.fori_loop(0, nrb, out_blk, 0, unroll=unroll)


def _delta_mixer(qkv, z, ab, conv_w, alog_row, dtb_row, out_norm, s0, hb, emit_state):
    b, t, _ = qkv.shape
    nh = N_HEADS_C
    c = DELTA_CHUNK
    n = t // c
    cg = min(n, 4)
    hg = min(hb, max(1, DELTA_PREP_CHAINS // (2 * cg)))
    assert hb % hg == 0 and n % cg == 0 and nh % hb == 0
    w = hb * LANES
    has_s0 = s0 is not None
    col = lambda off: pl.BlockSpec((1, t, w), lambda i, j, off=off: (i, 0, off + j))
    cw = lambda off: pl.BlockSpec((3, w), lambda i, j, off=off: (0, off + j))
    row = pl.BlockSpec((1, LANES), lambda i, j: (0, 0))
    st = pl.BlockSpec((1, 2, hb, DK_C, DV_C), lambda i, j: (i, 0, j, 0, 0))
    nb = nh // hb
    in_specs = [col(0), col(nb), col(2 * nb), cw(0), cw(nb), cw(2 * nb),
                pl.BlockSpec((1, t, LANES), lambda i, j: (i, 0, 0)), row, row, col(0), row]
    args = [qkv, qkv, qkv, conv_w, conv_w, conv_w, ab, alog_row, dtb_row, z, out_norm.reshape(1, DV_C)]
    if has_s0:
        in_specs.append(st)
        args.append(s0)
    out_specs = [col(0)]
    out_shape = [jax.ShapeDtypeStruct((b, t, nh * DV_C), BF16)]
    if emit_state:
        out_specs.append(st)
        out_shape.append(jax.ShapeDtypeStruct((b, 2, nh, DK_C, DV_C), F32))
    hd2 = 2 * hb
    scratch = [pltpu.VMEM((hb, t, LANES), F32), pltpu.VMEM((hb, t, LANES), F32), pltpu.VMEM((hb, t, LANES), F32),
               pltpu.VMEM((t, LANES), F32),
               pltpu.VMEM((hd2, t, LANES), F32), pltpu.VMEM((hd2, 2 * t, LANES), BF16),
               pltpu.VMEM((hd2, t, LANES), BF16), pltpu.VMEM((hd2, t, LANES), BF16),
               pltpu.VMEM((hd2, 8 * n, LANES), F32), pltpu.VMEM((hb, t + 16, LANES), F32),
               pltpu.VMEM((hd2, DK_C, DV_C), F32)]
    return pl.pallas_call(
        functools.partial(_delta_kernel, hb=hb, hg=hg, cg=cg, has_s0=has_s0, emit_state=emit_state),
        grid=(b, nb),
        in_specs=in_specs, out_specs=out_specs, out_shape=out_shape, scratch_shapes=scratch,
        compiler_params=_cparams(("arbitrary", "arbitrary"), DELTA_VMEM_LIMIT),
        name="delta_mixer",
    )(*args)


def _pad_lanes(x, n):
    return jnp.pad(x, ((0, 0), (0, n - x.shape[1])))


def kernel(x_prompt, x_sample, c, cache_l0_k, cache_l0_v, state_l1, c_ctx, l0_mod_w, l0_mod_b, l0_norm1, l0_w_in, l0_q_norm_a, l0_k_norm_a, l0_q_norm_b, l0_k_norm_b, l0_rel_bias, l0_w_out, l0_norm2, l0_mlp_w1, l0_mlp_w2, l1_mod_w, l1_mod_b, l1_norm1, l1_w_in, l1_conv_w, l1_a_log, l1_dt_bias, l1_out_norm, l1_w_out, l1_norm2, l1_mlp_w1, l1_mlp_w2):
    bp, tp, d = x_prompt.shape
    bs, ts, _ = x_sample.shape
    n_cache = (N_KV_A + N_HEADS_B) * HEAD_DIM
    bf = lambda w: w.astype(BF16)

    n_rows = -(-(1 + bs) // 8) * 8
    cvec = jnp.concatenate([c_ctx[None, :], c, jnp.zeros((n_rows - 1 - bs, d), F32)], axis=0)
    mods = []
    for mw, mb in ((l0_mod_w, l0_mod_b), (l1_mod_w, l1_mod_b)):
        m = _modulation(cvec, mw, mb).reshape(n_rows, N_MOD, d)
        mods.append((m[0:1], m[1:1 + bs]))

    xp = x_prompt.reshape(1, bp * tp, d)
    xs = x_sample

    scale = HEAD_DIM ** -0.5 * LOG2_E
    ones_a = jnp.ones((N_KV_A * HEAD_DIM,), F32)
    ones_b = jnp.ones((N_HEADS_B * HEAD_DIM,), F32)
    gain_row = jnp.concatenate([jnp.tile(l0_q_norm_a, N_HEADS_A) * scale, jnp.tile(l0_k_norm_a, N_KV_A), ones_a,
                                jnp.tile(l0_q_norm_b, N_HEADS_B) * scale, jnp.tile(l0_k_norm_b, N_HEADS_B),
                                ones_b])[None, :]
    w_in0 = bf(l0_w_in)
    w1_0, w2_0 = bf(l0_mlp_w1), bf(l0_mlp_w2)
    half = N_HEADS_A * HEAD_DIM

    mod_p, mod_s = mods[0]
    (qkv_p,) = _project(xp, mod_p, l0_norm1, [w_in0], PROJ_ROWS_L0)
    o_p, new_k, new_v = _ctx_attention(qkv_p.reshape(bp, tp, -1), gain_row)
    xp = _mix_mlp(xp, mod_p, l0_norm2, [o_p.reshape(1, bp * tp, -1)], [bf(l0_w_out)], w1_0, w2_0)

    (qkv_s,) = _project(xs, mod_s, l0_norm1, [w_in0], PROJ_ROWS_L0)
    ck = cache_l0_k.reshape(bs, -1, n_cache)
    cv = cache_l0_v.reshape(bs, -1, n_cache)
    o_a = _latent_attention_a(qkv_s, ck, cv, gain_row)
    o_b = _latent_attention_b(qkv_s, ck, cv, gain_row, l0_rel_bias)
    xs = _mix_mlp(xs, mod_s, l0_norm2, [o_a, o_b], [bf(l0_w_out[:half]), bf(l0_w_out[half:])], w1_0, w2_0)

    n_qkv = N_HEADS_C * (2 * DK_C + DV_C)
    n_z = N_HEADS_C * DV_C
    w_pieces = [bf(l1_w_in[:, :n_qkv]), bf(l1_w_in[:, n_qkv:n_qkv + n_z]),
                bf(_pad_lanes(l1_w_in[:, n_qkv + n_z:], LANES))]
    w_out1 = bf(l1_w_out)
    w1_1, w2_1 = bf(l1_mlp_w1), bf(l1_mlp_w2)
    alog_row = _pad_lanes(l1_a_log.reshape(1, -1), LANES)
    dtb_row = _pad_lanes(l1_dt_bias.reshape(1, -1), LANES)

    mod_p, mod_s = mods[1]
    qkv1_p, z_p, ab_p = _project(xp, mod_p, l1_norm1, w_pieces, PROJ_ROWS_L1)
    y_p, new_s = _delta_mixer(qkv1_p.reshape(bp, tp, -1), z_p.reshape(bp, tp, -1), ab_p.reshape(bp, tp, -1),
                              l1_conv_w, alog_row, dtb_row, l1_out_norm, None, hb=DELTA_HEADS_PROMPT, emit_state=True)
    xp = _mix_mlp(xp, mod_p, l1_norm2, [y_p.reshape(1, bp * tp, -1)], [w_out1], w1_1, w2_1)

    qkv1_s, z_s, ab_s = _project(xs, mod_s, l1_norm1, w_pieces, PROJ_ROWS_L1)
    (y_s,) = _delta_mixer(qkv1_s, z_s, ab_s, l1_conv_w, alog_row, dtb_row, l1_out_norm, state_l1.astype(F32),
                          hb=DELTA_HEADS_SAMPLE, emit_state=False)
    xs = _mix_mlp(xs, mod_s, l1_norm2, [y_s], [w_out1], w1_1, w2_1)

    return (xp.reshape(bp, tp, d), xs,
            new_k.reshape(bp, tp, N_KV_A + N_HEADS_B, HEAD_DIM), new_v.reshape(bp, tp, N_KV_A + N_HEADS_B, HEAD_DIM),
            new_s.astype(x_prompt.dtype))
```

```python
import functools
import math

import jax
import jax.numpy as jnp
from jax import lax
from jax.experimental import pallas as pl
from jax.experimental.pallas import tpu as pltpu

F32 = jnp.float32
BF16 = jnp.bfloat16

EPS = 1e-6
N_MOD = 6
HEAD_DIM = 64
N_HEADS_A = 8
N_KV_A = 2
N_HEADS_B = 8
GRID_W = 64
WIN_R = 8
WIN_C = 16
ROPE_THETA = 10000.0
N_HEADS_C = 8
DK_C = 128
DV_C = 128
NEG_BIG = -1e30
LOG2_E = math.log2(math.e)

LANES = 128
VMEM_LIMIT = 48 * 1024 * 1024
DELTA_VMEM_LIMIT = 56 * 1024 * 1024
PROJ_ROWS_L0 = 1024
PROJ_ROWS_L1 = 512
MLP_ROWS = 1024
MLP_FF_TILE = 1024
LAT_A_QUERY_TILE = 512
LAT_A_JOBS = 2
NBR_ROW_GROUP = 16
DELTA_CHUNK = LANES
DELTA_PREP_CHAINS_PROMPT = 32
DELTA_PREP_CHAINS_SAMPLE = 16
DELTA_ROW_BLOCK = 256
DELTA_HEADS_PROMPT = 8
DELTA_HEADS_SAMPLE = 2


def _cparams(sem, limit=VMEM_LIMIT):
    return pltpu.CompilerParams(dimension_semantics=sem, vmem_limit_bytes=limit)


def _dot(a, b):
    return jnp.dot(a, b, preferred_element_type=F32)


def _dot_nt(a, b):
    return lax.dot_general(a, b, (((1,), (1,)), ((), ())), preferred_element_type=F32)


def _dot_tn(a, b):
    return lax.dot_general(a, b, (((0,), (0,)), ((), ())), preferred_element_type=F32)


def _split2(x):
    hi = x.astype(BF16)
    lo = (x - hi.astype(F32)).astype(BF16)
    return hi, lo


def _dot2_rhs01(a_f32, b01):
    hi, lo = _split2(a_f32)
    return _dot(jnp.concatenate([hi, lo], axis=1), jnp.concatenate([b01, b01], axis=0))


def _dot2_lhs01(a01, b_f32):
    hi, lo = _split2(b_f32)
    return _dot(jnp.concatenate([a01, a01], axis=1), jnp.concatenate([hi, lo], axis=0))


def _dot2_lhs_bf16(a_f32, b_f32):
    ab = a_f32.astype(BF16)
    hi, lo = _split2(b_f32)
    return _dot(jnp.concatenate([ab, ab], axis=1), jnp.concatenate([hi, lo], axis=0))


def _pair_dots(lhs, rhs):
    outs = []
    for i in range(0, len(lhs), 2):
        r0, r1 = rhs[i].astype(BF16), rhs[i + 1].astype(BF16)
        z = jnp.zeros_like(r0)
        rr = jnp.concatenate([jnp.concatenate([r0, z], axis=1), jnp.concatenate([z, r1], axis=1)], axis=0)
        x = _dot(jnp.concatenate([lhs[i].astype(BF16), lhs[i + 1].astype(BF16)], axis=1), rr)
        outs += [x[:, :LANES], x[:, LANES:]]
    return outs


def _row_blocks(x, blk, first):
    return jnp.concatenate([x[i * blk:(i + 1) * blk] for i in range(0 if first else 1, x.shape[0] // blk, 2)], axis=0)


def _row_blocks_put(sel, rest, blk, first):
    pieces = []
    for i in range(rest.shape[0] // blk):
        if (i % 2 == 0) == first:
            pieces.append(sel[(i // 2) * blk:(i // 2 + 1) * blk])
        else:
            pieces.append(rest[i * blk:(i + 1) * blk])
    return jnp.concatenate(pieces, axis=0)


def _rms_rows(x, gain):
    ms = jnp.mean(x * x, axis=-1, keepdims=True)
    return x * lax.rsqrt(ms + EPS) * gain


def _mod_kernel(c_ref, w_ref, b_ref, o_ref):
    c = c_ref[...]
    a = (c * jax.nn.sigmoid(c)).astype(BF16)
    o_ref[...] = _dot(a, w_ref[...].astype(BF16)) + b_ref[...]


def _modulation(cvec, w, b):
    rows, d = cvec.shape
    n = w.shape[1]
    tn = n // 4
    return pl.pallas_call(
        _mod_kernel,
        grid=(n // tn,),
        in_specs=[pl.BlockSpec((rows, d), lambda j: (0, 0)),
                  pl.BlockSpec((d, tn), lambda j: (0, j)),
                  pl.BlockSpec((1, tn), lambda j: (0, j))],
        out_specs=pl.BlockSpec((rows, tn), lambda j: (0, j)),
        out_shape=jax.ShapeDtypeStruct((rows, n), F32),
        compiler_params=_cparams(("arbitrary",)),
        name="modulation",
    )(cvec, w, b.reshape(1, n))


def _proj_kernel(x_ref, mod_ref, g_ref, *rest, n_out):
    w_refs, o_refs = rest[:n_out], rest[n_out:]
    mod = mod_ref[0]
    h = _rms_rows(x_ref[0], g_ref[...]) * (1.0 + mod[1:2]) + mod[0:1]
    hb = h.astype(BF16)
    for w_ref, o_ref in zip(w_refs, o_refs):
        o_ref[0] = _dot(hb, w_ref[...]).astype(o_ref.dtype)


def _project(x, mod, gain, weights, tm):
    g, t, d = x.shape
    n_out = len(weights)
    tm = min(tm, t)
    in_specs = [pl.BlockSpec((1, tm, d), lambda i, j: (i, j, 0)),
                pl.BlockSpec((1, N_MOD, d), lambda i, j: (i, 0, 0)),
                pl.BlockSpec((1, d), lambda i, j: (0, 0))]
    in_specs += [pl.BlockSpec(w.shape, lambda i, j: (0, 0)) for w in weights]
    out_specs = [pl.BlockSpec((1, tm, w.shape[1]), lambda i, j: (i, j, 0)) for w in weights]
    out_shape = [jax.ShapeDtypeStruct((g, t, w.shape[1]), F32) for w in weights]
    return pl.pallas_call(
        functools.partial(_proj_kernel, n_out=n_out),
        grid=(g, t // tm),
        in_specs=in_specs, out_specs=out_specs, out_shape=out_shape,
        compiler_params=_cparams(("arbitrary", "arbitrary")),
        name="norm_project",
    )(x, mod, gain.reshape(1, d), *weights)


def _mlp_kernel(*refs, n_mix):
    x_ref, mod_ref, g_ref = refs[:3]
    o_refs = refs[3:3 + n_mix]
    wo_refs = refs[3 + n_mix:3 + 2 * n_mix]
    w1_ref, w2_ref, out_ref, x1_s, h_s, acc_s = refs[3 + 2 * n_mix:]
    f = pl.program_id(2)

    @pl.when(f == 0)
    def _():
        mod = mod_ref[0]
        mp = _dot(o_refs[0][0].astype(BF16), wo_refs[0][...])
        for o_ref, wo_ref in zip(o_refs[1:], wo_refs[1:]):
            mp += _dot(o_ref[0].astype(BF16), wo_ref[...])
        x1 = x_ref[0] + mod[2:3] * mp
        x1_s[...] = x1
        h = _rms_rows(x1, g_ref[...]) * (1.0 + mod[4:5]) + mod[3:4]
        h_s[...] = h.astype(BF16)
        acc_s[...] = jnp.zeros_like(acc_s)

    a = _dot(h_s[...], w1_ref[...])
    a = jnp.square(jnp.maximum(a, 0.0))
    acc_s[...] += _dot(a.astype(BF16), w2_ref[...])

    @pl.when(f == pl.num_programs(2) - 1)
    def _():
        out_ref[0] = x1_s[...] + mod_ref[0][5:6] * acc_s[...]


def _mix_mlp(x, mod, gain2, mixes, w_outs, w1, w2):
    g, t, d = x.shape
    dff = w1.shape[1]
    tm, tf = min(MLP_ROWS, t), min(MLP_FF_TILE, dff)
    n_mix = len(mixes)
    in_specs = [pl.BlockSpec((1, tm, d), lambda i, j, f: (i, j, 0)),
                pl.BlockSpec((1, N_MOD, d), lambda i, j, f: (i, 0, 0)),
                pl.BlockSpec((1, d), lambda i, j, f: (0, 0))]
    in_specs += [pl.BlockSpec((1, tm, o.shape[2]), lambda i, j, f: (i, j, 0)) for o in mixes]
    in_specs += [pl.BlockSpec(w.shape, lambda i, j, f: (0, 0)) for w in w_outs]
    in_specs += [pl.BlockSpec((d, tf), lambda i, j, f: (0, f)),
                 pl.BlockSpec((tf, d), lambda i, j, f: (f, 0))]
    return pl.pallas_call(
        functools.partial(_mlp_kernel, n_mix=n_mix),
        grid=(g, t // tm, dff // tf),
        in_specs=in_specs,
        out_specs=pl.BlockSpec((1, tm, d), lambda i, j, f: (i, j, 0)),
        out_shape=jax.ShapeDtypeStruct((g, t, d), F32),
        scratch_shapes=[pltpu.VMEM((tm, d), F32), pltpu.VMEM((tm, d), BF16), pltpu.VMEM((tm, d), F32)],
        compiler_params=_cparams(("arbitrary", "arbitrary", "arbitrary")),
        name="mix_mlp",
    )(x, mod, gain2.reshape(1, d), *mixes, *w_outs, w1, w2)


def _pair_ones():
    r = lax.broadcasted_iota(jnp.int32, (LANES, LANES), 0) // HEAD_DIM
    c = lax.broadcasted_iota(jnp.int32, (LANES, LANES), 1) // HEAD_DIM
    return jnp.where(r == c, 1.0, 0.0).astype(BF16)


def _head_rms(xb, gain, pair_ones):
    hi, lo = _split2(xb * xb)
    ss = _dot(jnp.concatenate([hi, lo], axis=1), jnp.concatenate([pair_ones, pair_ones], axis=0))
    return xb * lax.rsqrt(ss * (1.0 / HEAD_DIM) + EPS) * gain


def _dup_half(x, g):
    lane = lax.broadcasted_iota(jnp.int32, x.shape, 1)
    sw = pltpu.roll(x, HEAD_DIM, 1)
    if g == 0:
        return jnp.where(lane < HEAD_DIM, x, sw)
    return jnp.where(lane < HEAD_DIM, sw, x)


def _softmax_parts(scores):
    m = jnp.max(scores[0], axis=-1, keepdims=True)
    for s in scores[1:]:
        m = jnp.maximum(m, jnp.max(s, axis=-1, keepdims=True))
    ps = [jnp.exp2(s - m) for s in scores]
    l = jnp.sum(ps[0], axis=-1, keepdims=True)
    for p in ps[1:]:
        l += jnp.sum(p, axis=-1, keepdims=True)
    return ps, 1.0 / l


def _attend_pairs(jobs):
    tasks = []
    for qb, ks, vs, biases in jobs:
        lane = lax.broadcasted_iota(jnp.int32, qb.shape, 1)
        for hh in range(2):
            sel = (lane < HEAD_DIM) if hh == 0 else (lane >= HEAD_DIM)
            tasks.append(dict(q=jnp.where(sel, qb, jnp.zeros_like(qb)), ks=ks, vs=vs,
                              bias=None if biases is None else biases[hh]))
    for tk in tasks:
        scores = [_dot_nt(tk["q"], k) for k in tk["ks"]]
        if tk["bias"] is not None:
            scores = [s if b is None else s + b for s, b in zip(scores, tk["bias"])]
        tk["s"] = scores
    for tk in tasks:
        tk["p"], tk["inv"] = _softmax_parts(tk["s"])
    for tk in tasks:
        o = _dot(tk["p"][0].astype(BF16), tk["vs"][0])
        for p, v in zip(tk["p"][1:], tk["vs"][1:]):
            o += _dot(p.astype(BF16), v)
        tk["o"] = o * tk["inv"]
    outs = []
    for j in range(len(jobs)):
        o0, o1 = tasks[2 * j]["o"], tasks[2 * j + 1]["o"]
        lane_o = lax.broadcasted_iota(jnp.int32, o0.shape, 1)
        outs.append(jnp.where(lane_o < HEAD_DIM, o0, o1))
    return outs


_QA0, _KA, _VA, _QB0, _KB0, _VB0 = 0, 4, 5, 6, 10, 14


def _ctx_attn_kernel(qkv_ref, gain_ref, o_ref, nk_ref, nv_ref):
    ones = _pair_ones()

    def blk(j):
        return qkv_ref[0, :, j * LANES:(j + 1) * LANES]

    def gain(j):
        return gain_ref[:, j * LANES:(j + 1) * LANES]

    ka = _head_rms(blk(_KA), gain(_KA), ones)
    va = blk(_VA)
    nk_ref[0, :, 0:LANES] = ka
    nv_ref[0, :, 0:LANES] = va
    k2 = [_dup_half(ka, g).astype(BF16) for g in range(N_KV_A)]
    v2 = [_dup_half(va, g).astype(BF16) for g in range(N_KV_A)]
    jobs = []
    for j in range(4):
        qn = _head_rms(blk(_QA0 + j), gain(_QA0 + j), ones).astype(BF16)
        jobs.append((qn, [k2[j // 2]], [v2[j // 2]], None))
    for j, o in enumerate(_attend_pairs(jobs)):
        o_ref[0, :, j * LANES:(j + 1) * LANES] = o.astype(o_ref.dtype)
    jobs = []
    for j in range(4):
        kb = _head_rms(blk(_KB0 + j), gain(_KB0 + j), ones)
        vb = blk(_VB0 + j)
        nk_ref[0, :, (1 + j) * LANES:(2 + j) * LANES] = kb
        nv_ref[0, :, (1 + j) * LANES:(2 + j) * LANES] = vb
        qn = _head_rms(blk(_QB0 + j), gain(_QB0 + j), ones).astype(BF16)
        jobs.append((qn, [kb.astype(BF16)], [vb.astype(BF16)], None))
    for j, o in enumerate(_attend_pairs(jobs)):
        o_ref[0, :, (4 + j) * LANES:(5 + j) * LANES] = o.astype(o_ref.dtype)


def _ctx_attention(qkv, gain_row):
    b, t, n = qkv.shape
    n_cache = (N_KV_A + N_HEADS_B) * HEAD_DIM
    n_o = (N_HEADS_A + N_HEADS_B) * HEAD_DIM
    return pl.pallas_call(
        _ctx_attn_kernel,
        grid=(b,),
        in_specs=[pl.BlockSpec((1, t, n), lambda i: (i, 0, 0)),
                  pl.BlockSpec((1, n), lambda i: (0, 0))],
        out_specs=[pl.BlockSpec((1, t, n_o), lambda i: (i, 0, 0)),
                   pl.BlockSpec((1, t, n_cache), lambda i: (i, 0, 0)),
                   pl.BlockSpec((1, t, n_cache), lambda i: (i, 0, 0))],
        out_shape=[jax.ShapeDtypeStruct((b, t, n_o), BF16),
                   jax.ShapeDtypeStruct((b, t, n_cache), F32),
                   jax.ShapeDtypeStruct((b, t, n_cache), F32)],
        compiler_params=_cparams(("arbitrary",)),
        name="ctx_attention",
    )(qkv, gain_row)


def _rope(x, c, sa, sb):
    return x * c + pltpu.roll(x, LANES - 16, 1) * sa + pltpu.roll(x, 16, 1) * sb


def _lat_a_kernel(q_ref, k_ref, v_ref, ck_ref, cv_ref, gq_ref, gk_ref,
                  cq_ref, saq_ref, sbq_ref, ck_t_ref, sak_t_ref, sbk_t_ref,
                  o_ref, k2_s, v2_s):
    t = k_ref.shape[1]
    ones = _pair_ones()

    @pl.when(pl.program_id(1) == 0)
    def _():
        kn = _head_rms(k_ref[0], gk_ref[...], ones)
        kr = _rope(kn, ck_t_ref[...], sak_t_ref[...], sbk_t_ref[...])
        v = v_ref[0]
        ck = ck_ref[0]
        cv = cv_ref[0]
        for g in range(N_KV_A):
            k2_s[g, 0:t, :] = _dup_half(kr, g).astype(BF16)
            k2_s[g, t:, :] = _dup_half(ck, g).astype(BF16)
            v2_s[g, 0:t, :] = _dup_half(v, g).astype(BF16)
            v2_s[g, t:, :] = _dup_half(cv, g).astype(BF16)

    c, sa, sb = cq_ref[...], saq_ref[...], sbq_ref[...]
    for j0 in range(0, 4, LAT_A_JOBS):
        jobs = []
        for j in range(j0, j0 + LAT_A_JOBS):
            qn = _head_rms(q_ref[0, :, j * LANES:(j + 1) * LANES], gq_ref[:, j * LANES:(j + 1) * LANES], ones)
            jobs.append((_rope(qn, c, sa, sb).astype(BF16), [k2_s[j // 2]], [v2_s[j // 2]], None))
        for j, o in zip(range(j0, j0 + LAT_A_JOBS), _attend_pairs(jobs)):
            o_ref[0, :, j * LANES:(j + 1) * LANES] = o.astype(o_ref.dtype)


def _rope_tables(t):
    pos = jnp.arange(t, dtype=jnp.int32)
    rc = jnp.stack([pos // GRID_W, pos % GRID_W], axis=-1).astype(F32)
    axis_dim = HEAD_DIM // 2
    inv_freq = 1.0 / (ROPE_THETA ** (jnp.arange(0, axis_dim, 2, dtype=F32) / axis_dim))
    ang = rc[:, :, None] * inv_freq
    cos, sin = jnp.cos(ang), jnp.sin(ang)
    zero = jnp.zeros_like(sin)
    c = jnp.stack([cos, cos], axis=2).reshape(t, HEAD_DIM)
    sa = jnp.stack([-sin, zero], axis=2).reshape(t, HEAD_DIM)
    sb = jnp.stack([zero, sin], axis=2).reshape(t, HEAD_DIM)
    two = lambda a: jnp.concatenate([a, a], axis=-1)
    return two(c), two(sa), two(sb)


def _latent_attention_a(qkv, cache_k, cache_v, gain_row):
    b, t, _ = qkv.shape
    p = cache_k.shape[1]
    tq = min(LAT_A_QUERY_TILE, t)
    c, sa, sb = _rope_tables(t)
    gq = gain_row[:, _QA0 * LANES:(_QA0 + 4) * LANES]
    gk = gain_row[:, _KA * LANES:(_KA + 1) * LANES]
    tab_q = pl.BlockSpec((tq, LANES), lambda i, j: (j, 0))
    tab_k = pl.BlockSpec((t, LANES), lambda i, j: (0, 0))
    return pl.pallas_call(
        _lat_a_kernel,
        grid=(b, t // tq),
        in_specs=[pl.BlockSpec((1, tq, 4 * LANES), lambda i, j: (i, j, 0)),
                  pl.BlockSpec((1, t, LANES), lambda i, j: (i, 0, _KA)),
                  pl.BlockSpec((1, t, LANES), lambda i, j: (i, 0, _VA)),
                  pl.BlockSpec((1, p, LANES), lambda i, j: (i, 0, 0)),
                  pl.BlockSpec((1, p, LANES), lambda i, j: (i, 0, 0)),
                  pl.BlockSpec((1, 4 * LANES), lambda i, j: (0, 0)),
                  pl.BlockSpec((1, LANES), lambda i, j: (0, 0)),
                  tab_q, tab_q, tab_q, tab_k, tab_k, tab_k],
        out_specs=pl.BlockSpec((1, tq, 4 * LANES), lambda i, j: (i, j, 0)),
        out_shape=jax.ShapeDtypeStruct((b, t, 4 * LANES), BF16),
        scratch_shapes=[pltpu.VMEM((N_KV_A, t + p, LANES), BF16), pltpu.VMEM((N_KV_A, t + p, LANES), BF16)],
        compiler_params=_cparams(("arbitrary", "arbitrary")),
        name="latent_attention_a",
    )(qkv, qkv, qkv, cache_k, cache_v, gq, gk, c, sa, sb, c, sa, sb)


def _nbr_bias_table(rel_bias, rows):
    wr = min(WIN_R, rows)
    cols = jnp.arange(GRID_W, dtype=jnp.int32)
    col_start = jnp.clip(cols - WIN_C // 2, 0, GRID_W - WIN_C)
    kc = jnp.arange(GRID_W, dtype=jnp.int32)
    in_win = (kc[None, :] >= col_start[:, None]) & (kc[None, :] < col_start[:, None] + WIN_C)
    nh, n_ro = rel_bias.shape[0], 2 * wr - 1
    rb = rel_bias[:, WIN_R - wr:WIN_R + wr - 1, :].astype(F32) * LOG2_E
    span = 2 * GRID_W
    lead = GRID_W - WIN_C
    v = jnp.pad(rb, ((0, 0), (0, 0), (lead, span - lead - (2 * WIN_C - 1))))
    flat = jnp.tile(v, (1, 1, GRID_W))
    toep = flat[..., GRID_W - 1:GRID_W - 1 + GRID_W * (span - 1)].reshape(nh, n_ro, GRID_W, span - 1)[..., :GRID_W]
    base = jnp.where(in_win[None, None], toep, NEG_BIG)
    return jnp.concatenate([base[:, :-1], base[:, 1:]], axis=-1)


def _nbr_kernel(q_ref, k_ref, v_ref, ck_ref, cv_ref, gq_ref, gk_ref, bias_ref, o_ref, qn_s, kn_s, vn_s):
    t = q_ref.shape[1]
    rows = t // GRID_W
    wr = min(WIN_R, rows)
    ones = _pair_ones()
    qn_s[...] = _head_rms(q_ref[0], gq_ref[0], ones).astype(BF16)
    kn_s[...] = _head_rms(k_ref[0], gk_ref[0], ones).astype(BF16)
    vn_s[...] = v_ref[0].astype(BF16)
    ck = ck_ref[0].astype(BF16)
    cv = cv_ref[0].astype(BF16)
    rb = math.gcd(rows, NBR_ROW_GROUP)

    def row_group(gi, carry):
        jobs, q0s = [], []
        for rr in range(rb):
            r = gi * rb + rr
            rs = jnp.clip(r - wr // 2, 0, rows - wr)
            var = r - rs
            q0 = pl.multiple_of(r * GRID_W, GRID_W)
            k0 = pl.multiple_of(rs * GRID_W, GRID_W)
            biases = [[jnp.concatenate([bias_ref[hh, wr - 1 - var + 2 * m] for m in range(wr // 2)], axis=-1), None]
                      for hh in range(2)]
            jobs.append((qn_s[pl.ds(q0, GRID_W), :], [kn_s[pl.ds(k0, wr * GRID_W), :], ck],
                         [vn_s[pl.ds(k0, wr * GRID_W), :], cv], biases))
            q0s.append(q0)
        for q0, o in zip(q0s, _attend_pairs(jobs)):
            o_ref[0, pl.ds(q0, GRID_W), :] = o.astype(o_ref.dtype)
        return carry

    lax.fori_loop(0, rows // rb, row_group, 0)


def _latent_attention_b(qkv, cache_k, cache_v, gain_row, rel_bias):
    b, t, _ = qkv.shape
    p = cache_k.shape[1]
    rows = t // GRID_W
    wr = min(WIN_R, rows)
    assert wr % 2 == 0
    n_tab = 2 * wr - 2
    table = _nbr_bias_table(rel_bias, rows).reshape(N_HEADS_B // 2, 2, n_tab, GRID_W, 2 * GRID_W)
    gq = gain_row[:, _QB0 * LANES:(_QB0 + 4) * LANES].reshape(4, 1, LANES)
    gk = gain_row[:, _KB0 * LANES:(_KB0 + 4) * LANES].reshape(4, 1, LANES)
    return pl.pallas_call(
        _nbr_kernel,
        grid=(b, N_HEADS_B // 2),
        in_specs=[pl.BlockSpec((1, t, LANES), lambda i, j: (i, 0, _QB0 + j)),
                  pl.BlockSpec((1, t, LANES), lambda i, j: (i, 0, _KB0 + j)),
                  pl.BlockSpec((1, t, LANES), lambda i, j: (i, 0, _VB0 + j)),
                  pl.BlockSpec((1, p, LANES), lambda i, j: (i, 0, 1 + j)),
                  pl.BlockSpec((1, p, LANES), lambda i, j: (i, 0, 1 + j)),
                  pl.BlockSpec((1, 1, LANES), lambda i, j: (j, 0, 0)),
                  pl.BlockSpec((1, 1, LANES), lambda i, j: (j, 0, 0)),
                  pl.BlockSpec((None, 2, n_tab, GRID_W, 2 * GRID_W), lambda i, j: (j, 0, 0, 0, 0))],
        out_specs=pl.BlockSpec((1, t, LANES), lambda i, j: (i, 0, j)),
        out_shape=jax.ShapeDtypeStruct((b, t, 4 * LANES), BF16),
        scratch_shapes=[pltpu.VMEM((t, LANES), BF16)] * 3,
        compiler_params=_cparams(("arbitrary", "arbitrary")),
        name="latent_attention_b",
    )(qkv, qkv, qkv, cache_k, cache_v, gq, gk, table)


def _delta_kernel(*refs, hb, hg, cg, has_s0, emit_state):
    (q_ref, k_ref, v_ref, cwq_ref, cwk_ref, cwv_ref, ab_ref, alog_ref, dtb_ref, z_ref, onorm_ref), rest = (
        refs[:11], refs[11:])
    s0_ref = None
    if has_s0:
        s0_ref, rest = rest[0], rest[1:]
    y_ref, rest = rest[0], rest[1:]
    sfin_ref = None
    if emit_state:
        sfin_ref, rest = rest[0], rest[1:]
    q_s, k_s, v_s, gates_s, u_s, wq_s, kt_s, att_s, el_s, o_s, s_s = rest
    pad_s = o_s.at[0]
    t = q_ref.shape[1]
    c = DELTA_CHUNK
    n = t // c
    h0 = pl.program_id(1) * hb
    rbs = min(t, DELTA_ROW_BLOCK)
    nrb = t // rbs
    unroll = min(4, nrb)

    pad_s[0:8, :] = jnp.zeros((8, LANES), F32)
    pad_s[t + 8:t + 16, :] = jnp.zeros((8, LANES), F32)

    def conv_silu_norm(x_ref, cw_ref, hd, dst, scale):
        for i in range(nrb):
            pad_s[8 + i * rbs:8 + (i + 1) * rbs, :] = x_ref[0, i * rbs:(i + 1) * rbs, hd * LANES:(hd + 1) * LANES]
        cw = cw_ref[:, hd * LANES:(hd + 1) * LANES]

        def blk(i, carry):
            r = pl.multiple_of(i * rbs, rbs)
            y = (pad_s[pl.ds(r + 7, rbs), :] * cw[0:1] + pad_s[pl.ds(r + 8, rbs), :] * cw[1:2]
                 + pad_s[pl.ds(r + 9, rbs), :] * cw[2:3])
            y = y * jax.nn.sigmoid(y)
            if scale is not None:
                y = y * lax.rsqrt(jnp.sum(y * y, axis=-1, keepdims=True) + EPS) * scale
            dst[hd, pl.ds(r, rbs), :] = y
            return carry

        lax.fori_loop(0, nrb, blk, 0, unroll=unroll)

    for hd in range(hb):
        conv_silu_norm(q_ref, cwq_ref, hd, q_s, DK_C ** -0.5)
        conv_silu_norm(k_ref, cwk_ref, hd, k_s, 1.0)
        conv_silu_norm(v_ref, cwv_ref, hd, v_s, None)

    def gate_blk(i, carry):
        r = pl.multiple_of(i * rbs, rbs)
        ab = ab_ref[0, pl.ds(r, rbs), :]
        xa = ab + dtb_ref[...]
        softplus = jnp.maximum(xa, 0.0) + jnp.log1p(jnp.exp(-jnp.abs(xa)))
        lane_t = lax.broadcasted_iota(jnp.int32, ab.shape, 1)
        gates_s[pl.ds(r, rbs), :] = jnp.where(lane_t < 2 * N_HEADS_C, -jnp.exp(alog_ref[...]) * softplus,
                                              jax.nn.sigmoid(ab))
        return carry

    lax.fori_loop(0, nrb, gate_blk, 0, unroll=unroll)

    ii = lax.broadcasted_iota(jnp.int32, (c, c), 0)
    jj = lax.broadcasted_iota(jnp.int32, (c, c), 1)
    eye = jnp.where(ii == jj, 1.0, 0.0)
    ones_cc = jnp.ones((c, c), BF16)
    bd8 = (ii // 8) == (jj // 8)
    offs = []
    blk = 8
    while blk < c:
        offs.append((blk, ((ii // (2 * blk)) == (jj // (2 * blk))) & ((ii // blk) != (jj // blk))))
        blk *= 2
    dir_masks = []
    for d in range(2):
        if d == 0:
            incl, strict, incl_t = ii >= jj, ii > jj, ii <= jj
        else:
            incl, strict, incl_t = ii <= jj, ii < jj, ii >= jj
        dir_masks.append((incl, strict, incl_t, jnp.where(incl, 1.0, 0.0).astype(BF16)))
    pick_r = lax.broadcasted_iota(jnp.int32, (LANES, 4 * LANES), 0)
    pick_c = lax.broadcasted_iota(jnp.int32, (LANES, 4 * LANES), 1)
    n_cgroups = n // cg

    def prep_unit(gi, carry):
        hg0 = (gi // n_cgroups) * hg
        c0 = (gi % n_cgroups) * cg
        chains = []
        for hh in range(hg):
            hd = hg0 + hh
            pick = jnp.where(pick_r == (pick_c // LANES) * N_HEADS_C + h0 + hd, 1.0, 0.0).astype(BF16)
            for cc in range(cg):
                ci = c0 + cc
                r0 = pl.multiple_of(ci * c, c)
                qb, kb = q_s[hd, pl.ds(r0, c), :].astype(BF16), k_s[hd, pl.ds(r0, c), :].astype(BF16)
                kk, qk = _dot_nt(kb, kb), _dot_nt(qb, kb)
                picked = _dot2_rhs01(gates_s[pl.ds(r0, c), :], pick)
                for d in range(2):
                    chains.append(dict(hd=hd, ci=ci, r0=r0, d=d, kk=kk, qk=qk,
                                       la=picked[:, d * LANES:(d + 1) * LANES],
                                       be=picked[:, (2 + d) * LANES:(3 + d) * LANES]))
        for ch in chains:
            incl, strict, incl_t, cum_mask = dir_masks[ch["d"]]
            ch["g_col"] = _dot2_lhs01(cum_mask, ch["la"])
            ch["g_row"] = _dot2_lhs01(ones_cc, jnp.where(incl_t, ch["la"], 0.0))
        for ch in chains:
            incl, strict, incl_t, cum_mask = dir_masks[ch["d"]]
            ch["decay"] = jnp.where(incl, jnp.exp(jnp.where(incl, ch["g_col"] - ch["g_row"], 0.0)), 0.0)
            ch["a"] = jnp.where(strict, (ch["kk"] * ch["decay"]) * ch["be"], 0.0)
            ch["p8"] = jnp.where(bd8, -ch["a"], 0.0)
        for ch, pk2 in zip(chains, _pair_dots([ch["p8"] for ch in chains], [ch["p8"] for ch in chains])):
            ch["pk2"] = pk2
        for ch in chains:
            ch["acc"] = eye + ch["p8"]
            ch["r"] = _dot(ch["pk2"].astype(BF16), jnp.concatenate([ch["pk2"], ch["acc"]], axis=1).astype(BF16))
        for ch in chains:
            ch["acc"] = ch["acc"] + ch["r"][:, c:]
        for ch, x in zip(chains, _pair_dots([ch["r"][:, :c] for ch in chains], [ch["acc"] for ch in chains])):
            ch["t"] = ch["acc"] + x
        for blk, off in offs:
            a_sel = [_row_blocks(jnp.where(off, ch["a"], 0.0), blk, ch["d"] == 1) for ch in chains]
            for ch, y in zip(chains, _pair_dots(a_sel, [ch["t"] for ch in chains])):
                ch["y"] = _row_blocks_put(y, jnp.zeros((c, c), F32), blk, ch["d"] == 1)
            t_sel = [_row_blocks(ch["t"], blk, ch["d"] == 1) for ch in chains]
            for ch, ts, x in zip(chains, t_sel, _pair_dots(t_sel, [ch["y"] for ch in chains])):
                ch["t"] = _row_blocks_put(ts - x, ch["t"], blk, ch["d"] == 1)
        for ch in chains:
            ch["eg"] = jnp.exp(ch["g_col"])
            ch["kc"] = k_s[ch["hd"], pl.ds(ch["r0"], c), :]
            vc = v_s[ch["hd"], pl.ds(ch["r0"], c), :]
            rhs = jnp.concatenate([vc * ch["be"], ch["kc"] * ch["be"] * ch["eg"]], axis=1)
            ch["uw"] = _dot2_lhs_bf16(ch["t"], rhs)
        for ch in chains:
            d, r0, ci, g_col = ch["d"], ch["r0"], ch["ci"], ch["g_col"]
            hx = ch["hd"] * 2 + d
            g_last = g_col[c - 1:c, :] if d == 0 else g_col[0:1, :]
            u_s[hx, pl.ds(r0, c), :] = ch["uw"][:, :c]
            r2 = pl.multiple_of(ci * 2 * c, 2 * c)
            wq_s[hx, pl.ds(r2, c), :] = ch["uw"][:, c:].astype(BF16)
            wq_s[hx, pl.ds(r2 + c, c), :] = (q_s[ch["hd"], pl.ds(r0, c), :] * ch["eg"]).astype(BF16)
            kt_s[hx, pl.ds(r0, c), :] = (ch["kc"] * jnp.exp(g_last - g_col)).astype(BF16)
            att_s[hx, pl.ds(r0, c), :] = (ch["qk"] * ch["decay"]).astype(BF16)
            r8 = pl.multiple_of(ci * 8, 8)
            el_s[hx, pl.ds(r8, 8), :] = jnp.broadcast_to(jnp.exp(g_last), (8, LANES))
        return carry

    def scan_step(i, carry):
        items = []
        for hx in range(2 * hb):
            ci = i if hx % 2 == 0 else n - 1 - i
            items.append(dict(hx=hx, s=s_s[hx], r0=pl.multiple_of(ci * c, c), r2=pl.multiple_of(ci * 2 * c, 2 * c),
                              r8=pl.multiple_of(ci * 8, 8)))
        for it in items:
            it["ws"] = _dot(wq_s[it["hx"], pl.ds(it["r2"], 2 * c), :], it["s"].astype(BF16))
        for it in items:
            it["vb"] = (u_s[it["hx"], pl.ds(it["r0"], c), :] - it["ws"][:c]).astype(BF16)
        for it in items:
            hx = it["hx"]
            o_s[hx // 2, pl.ds(it["r0"], c), :] += (it["ws"][c:]
                                                    + _dot(att_s[hx, pl.ds(it["r0"], c), :], it["vb"]))
            s_s[hx] = (it["s"] * el_s[hx, pl.ds(it["r8"], 1), :]
                       + _dot_tn(kt_s[hx, pl.ds(it["r0"], c), :], it["vb"]))
        return carry

    for hx in range(2 * hb):
        if has_s0:
            s_s[hx] = s0_ref[0, hx % 2, hx // 2]
        else:
            s_s[hx] = jnp.zeros((DK_C, DV_C), F32)

    def zero_blk(i, carry):
        for hd in range(hb):
            o_s[hd, pl.ds(pl.multiple_of(i * rbs, rbs), rbs), :] = jnp.zeros((rbs, LANES), F32)
        return carry

    lax.fori_loop(0, nrb, zero_blk, 0)

    lax.fori_loop(0, (hb // hg) * n_cgroups, prep_unit, 0)
    lax.fori_loop(0, n, scan_step, 0)

    if emit_state:
        for hx in range(2 * hb):
            sfin_ref[0, hx % 2, hx // 2] = s_s[hx]

    for hd in range(hb):
        def out_blk(i, carry, hd=hd):
            r = pl.multiple_of(i * rbs, rbs)
            o = o_s[hd, pl.ds(r, rbs), :]
            z = z_ref[0, pl.ds(r, rbs), hd * LANES:(hd + 1) * LANES]
            y = _rms_rows(o, onorm_ref[...]) * (z * jax.nn.sigmoid(z))
            y_ref[0, pl.ds(r, rbs), hd * LANES:(hd + 1) * LANES] = y.astype(y_ref.dtype)
            return carry

        lax.fori_loop(0, nrb, out_blk, 0, unroll=unroll)


def _delta_mixer(qkv, z, ab, conv_w, alog_row, dtb_row, out_norm, s0, hb, chains, emit_state):
    b, t, _ = qkv.shape
    nh = N_HEADS_C
    c = DELTA_CHUNK
    n = t // c
    hg = min(hb, chains // 2)
    cg = max(k for k in range(1, n + 1) if n % k == 0 and 2 * hg * k <= max(chains, 2 * hg))
    assert hb % hg == 0 and n % cg == 0 and nh % hb == 0
    w = hb * LANES
    has_s0 = s0 is not None
    col = lambda off: pl.BlockSpec((1, t, w), lambda i, j, off=off: (i, 0, off + j))
    cw = lambda off: pl.BlockSpec((3, w), lambda i, j, off=off: (0, off + j))
    row = pl.BlockSpec((1, LANES), lambda i, j: (0, 0))
    st = pl.BlockSpec((1, 2, hb, DK_C, DV_C), lambda i, j: (i, 0, j, 0, 0))
    nb = nh // hb
    in_specs = [col(0), col(nb), col(2 * nb), cw(0), cw(nb), cw(2 * nb),
                pl.BlockSpec((1, t, LANES), lambda i, j: (i, 0, 0)), row, row, col(0), row]
    args = [qkv, qkv, qkv, conv_w, conv_w, conv_w, ab, alog_row, dtb_row, z, out_norm.reshape(1, DV_C)]
    if has_s0:
        in_specs.append(st)
        args.append(s0)
    out_specs = [col(0)]
    out_shape = [jax.ShapeDtypeStruct((b, t, nh * DV_C), BF16)]
    if emit_state:
        out_specs.append(st)
        out_shape.append(jax.ShapeDtypeStruct((b, 2, nh, DK_C, DV_C), F32))
    hd2 = 2 * hb
    scratch = [pltpu.VMEM((hb, t, LANES), F32), pltpu.VMEM((hb, t, LANES), F32), pltpu.VMEM((hb, t, LANES), F32),
               pltpu.VMEM((t, LANES), F32),
               pltpu.VMEM((hd2, t, LANES), F32), pltpu.VMEM((hd2, 2 * t, LANES), BF16),
               pltpu.VMEM((hd2, t, LANES), BF16), pltpu.VMEM((hd2, t, LANES), BF16),
               pltpu.VMEM((hd2, 8 * n, LANES), F32), pltpu.VMEM((hb, t + 16, LANES), F32),
               pltpu.VMEM((hd2, DK_C, DV_C), F32)]
    return pl.pallas_call(
        functools.partial(_delta_kernel, hb=hb, hg=hg, cg=cg, has_s0=has_s0, emit_state=emit_state),
        grid=(b, nb),
        in_specs=in_specs, out_specs=out_specs, out_shape=out_shape, scratch_shapes=scratch,
        compiler_params=_cparams(("arbitrary", "arbitrary"), DELTA_VMEM_LIMIT),
        name="delta_mixer",
    )(*args)


def _pad_lanes(x, n):
    return jnp.pad(x, ((0, 0), (0, n - x.shape[1])))


def kernel(x_prompt, x_sample, c, cache_l0_k, cache_l0_v, state_l1, c_ctx, l0_mod_w, l0_mod_b, l0_norm1, l0_w_in, l0_q_norm_a, l0_k_norm_a, l0_q_norm_b, l0_k_norm_b, l0_rel_bias, l0_w_out, l0_norm2, l0_mlp_w1, l0_mlp_w2, l1_mod_w, l1_mod_b, l1_norm1, l1_w_in, l1_conv_w, l1_a_log, l1_dt_bias, l1_out_norm, l1_w_out, l1_norm2, l1_mlp_w1, l1_mlp_w2):
    bp, tp, d = x_prompt.shape
    bs, ts, _ = x_sample.shape
    n_cache = (N_KV_A + N_HEADS_B) * HEAD_DIM
    bf = lambda w: w.astype(BF16)

    n_rows = -(-(1 + bs) // 8) * 8
    cvec = jnp.concatenate([c_ctx[None, :], c, jnp.zeros((n_rows - 1 - bs, d), F32)], axis=0)
    mods = []
    for mw, mb in ((l0_mod_w, l0_mod_b), (l1_mod_w, l1_mod_b)):
        m = _modulation(cvec, mw, mb).reshape(n_rows, N_MOD, d)
        mods.append((m[0:1], m[1:1 + bs]))

    xp = x_prompt.reshape(1, bp * tp, d)
    xs = x_sample

    scale = HEAD_DIM ** -0.5 * LOG2_E
    ones_a = jnp.ones((N_KV_A * HEAD_DIM,), F32)
    ones_b = jnp.ones((N_HEADS_B * HEAD_DIM,), F32)
    gain_row = jnp.concatenate([jnp.tile(l0_q_norm_a, N_HEADS_A) * scale, jnp.tile(l0_k_norm_a, N_KV_A), ones_a,
                                jnp.tile(l0_q_norm_b, N_HEADS_B) * scale, jnp.tile(l0_k_norm_b, N_HEADS_B),
                                ones_b])[None, :]
    w_in0 = bf(l0_w_in)
    w1_0, w2_0 = bf(l0_mlp_w1), bf(l0_mlp_w2)
    half = N_HEADS_A * HEAD_DIM

    mod_p, mod_s = mods[0]
    (qkv_p,) = _project(xp, mod_p, l0_norm1, [w_in0], PROJ_ROWS_L0)
    o_p, new_k, new_v = _ctx_attention(qkv_p.reshape(bp, tp, -1), gain_row)
    xp = _mix_mlp(xp, mod_p, l0_norm2, [o_p.reshape(1, bp * tp, -1)], [bf(l0_w_out)], w1_0, w2_0)

    (qkv_s,) = _project(xs, mod_s, l0_norm1, [w_in0], PROJ_ROWS_L0)
    ck = cache_l0_k.reshape(bs, -1, n_cache)
    cv = cache_l0_v.reshape(bs, -1, n_cache)
    o_a = _latent_attention_a(qkv_s, ck, cv, gain_row)
    o_b = _latent_attention_b(qkv_s, ck, cv, gain_row, l0_rel_bias)
    xs = _mix_mlp(xs, mod_s, l0_norm2, [o_a, o_b], [bf(l0_w_out[:half]), bf(l0_w_out[half:])], w1_0, w2_0)

    n_qkv = N_HEADS_C * (2 * DK_C + DV_C)
    n_z = N_HEADS_C * DV_C
    w_pieces = [bf(l1_w_in[:, :n_qkv]), bf(l1_w_in[:, n_qkv:n_qkv + n_z]),
                bf(_pad_lanes(l1_w_in[:, n_qkv + n_z:], LANES))]
    w_out1 = bf(l1_w_out)
    w1_1, w2_1 = bf(l1_mlp_w1), bf(l1_mlp_w2)
    alog_row = _pad_lanes(l1_a_log.reshape(1, -1), LANES)
    dtb_row = _pad_lanes(l1_dt_bias.reshape(1, -1), LANES)

    mod_p, mod_s = mods[1]
    qkv1_p, z_p, ab_p = _project(xp, mod_p, l1_norm1, w_pieces, PROJ_ROWS_L1)
    y_p, new_s = _delta_mixer(qkv1_p.reshape(bp, tp, -1), z_p.reshape(bp, tp, -1), ab_p.reshape(bp, tp, -1),
                              l1_conv_w, alog_row, dtb_row, l1_out_norm, None, hb=DELTA_HEADS_PROMPT,
                              chains=DELTA_PREP_CHAINS_PROMPT, emit_state=True)
    xp = _mix_mlp(xp, mod_p, l1_norm2, [y_p.reshape(1, bp * tp, -1)], [w_out1], w1_1, w2_1)

    qkv1_s, z_s, ab_s = _project(xs, mod_s, l1_norm1, w_pieces, PROJ_ROWS_L1)
    (y_s,) = _delta_mixer(qkv1_s, z_s, ab_s, l1_conv_w, alog_row, dtb_row, l1_out_norm, state_l1.astype(F32),
                          hb=DELTA_HEADS_SAMPLE, chains=DELTA_PREP_CHAINS_SAMPLE, emit_state=False)
    xs = _mix_mlp(xs, mod_s, l1_norm2, [y_s], [w_out1], w1_1, w2_1)

    return (xp.reshape(bp, tp, d), xs,
            new_k.reshape(bp, tp, N_KV_A + N_HEADS_B, HEAD_DIM), new_v.reshape(bp, tp, N_KV_A + N_HEADS_B, HEAD_DIM),
            new_s.astype(x_prompt.dtype))
```

```python
import functools
import math

import jax
import jax.numpy as jnp
from jax import lax
from jax.experimental import pallas as pl
from jax.experimental.pallas import tpu as pltpu

F32 = jnp.float32
BF16 = jnp.bfloat16

EPS = 1e-6
N_MOD = 6
HEAD_DIM = 64
N_HEADS_A = 8
N_KV_A = 2
N_HEADS_B = 8
GRID_W = 64
WIN_R = 8
WIN_C = 16
ROPE_THETA = 10000.0
N_HEADS_C = 8
DK_C = 128
DV_C = 128
NEG_BIG = -1e30
LOG2_E = math.log2(math.e)

LANES = 128
VMEM_LIMIT = 48 * 1024 * 1024
DELTA_VMEM_LIMIT = 56 * 1024 * 1024
PROJ_ROWS_L0 = 1024
PROJ_ROWS_L1 = 512
MLP_ROWS = 1024
MLP_FF_TILE = 1024
LAT_A_QUERY_TILE = 512
LAT_A_JOBS = 2
NBR_ROW_GROUP = 16
DELTA_CHUNK = LANES
DELTA_PREP_CHAINS_PROMPT = 32
DELTA_PREP_CHAINS_SAMPLE = 16
DELTA_ROW_BLOCK = 256
DELTA_HEADS_PROMPT = 8
DELTA_HEADS_SAMPLE = 2


def _cparams(sem, limit=VMEM_LIMIT):
    return pltpu.CompilerParams(dimension_semantics=sem, vmem_limit_bytes=limit)


def _dot(a, b):
    return jnp.dot(a, b, preferred_element_type=F32)


def _dot_nt(a, b):
    return lax.dot_general(a, b, (((1,), (1,)), ((), ())), preferred_element_type=F32)


def _dot_tn(a, b):
    return lax.dot_general(a, b, (((0,), (0,)), ((), ())), preferred_element_type=F32)


def _split2(x):
    hi = x.astype(BF16)
    lo = (x - hi.astype(F32)).astype(BF16)
    return hi, lo


def _dot2_rhs01(a_f32, b01):
    hi, lo = _split2(a_f32)
    return _dot(jnp.concatenate([hi, lo], axis=1), jnp.concatenate([b01, b01], axis=0))


def _dot2_lhs01(a01, b_f32):
    hi, lo = _split2(b_f32)
    return _dot(jnp.concatenate([a01, a01], axis=1), jnp.concatenate([hi, lo], axis=0))


def _dot2_lhs_bf16(a_f32, b_f32):
    ab = a_f32.astype(BF16)
    hi, lo = _split2(b_f32)
    return _dot(jnp.concatenate([ab, ab], axis=1), jnp.concatenate([hi, lo], axis=0))


def _pair_dots(lhs, rhs):
    outs = []
    for i in range(0, len(lhs), 2):
        r0, r1 = rhs[i].astype(BF16), rhs[i + 1].astype(BF16)
        z = jnp.zeros_like(r0)
        rr = jnp.concatenate([jnp.concatenate([r0, z], axis=1), jnp.concatenate([z, r1], axis=1)], axis=0)
        x = _dot(jnp.concatenate([lhs[i].astype(BF16), lhs[i + 1].astype(BF16)], axis=1), rr)
        outs += [x[:, :LANES], x[:, LANES:]]
    return outs


def _row_blocks(x, blk, first):
    return jnp.concatenate([x[i * blk:(i + 1) * blk] for i in range(0 if first else 1, x.shape[0] // blk, 2)], axis=0)


def _row_blocks_put(sel, rest, blk, first):
    pieces = []
    for i in range(rest.shape[0] // blk):
        if (i % 2 == 0) == first:
            pieces.append(sel[(i // 2) * blk:(i // 2 + 1) * blk])
        else:
            pieces.append(rest[i * blk:(i + 1) * blk])
    return jnp.concatenate(pieces, axis=0)


def _rms_rows(x, gain):
    ms = jnp.mean(x * x, axis=-1, keepdims=True)
    return x * lax.rsqrt(ms + EPS) * gain


def _mod_kernel(c_ref, w_ref, b_ref, o_ref):
    c = c_ref[...]
    a = (c * jax.nn.sigmoid(c)).astype(BF16)
    o_ref[...] = _dot(a, w_ref[...].astype(BF16)) + b_ref[...]


def _modulation(cvec, w, b):
    rows, d = cvec.shape
    n = w.shape[1]
    tn = n // 4
    return pl.pallas_call(
        _mod_kernel,
        grid=(n // tn,),
        in_specs=[pl.BlockSpec((rows, d), lambda j: (0, 0)),
                  pl.BlockSpec((d, tn), lambda j: (0, j)),
                  pl.BlockSpec((1, tn), lambda j: (0, j))],
        out_specs=pl.BlockSpec((rows, tn), lambda j: (0, j)),
        out_shape=jax.ShapeDtypeStruct((rows, n), F32),
        compiler_params=_cparams(("arbitrary",)),
        name="modulation",
    )(cvec, w, b.reshape(1, n))


def _proj_kernel(x_ref, mod_ref, g_ref, *rest, n_out):
    w_refs, o_refs = rest[:n_out], rest[n_out:]
    mod = mod_ref[0]
    h = _rms_rows(x_ref[0], g_ref[...]) * (1.0 + mod[1:2]) + mod[0:1]
    hb = h.astype(BF16)
    for w_ref, o_ref in zip(w_refs, o_refs):
        o_ref[0] = _dot(hb, w_ref[...]).astype(o_ref.dtype)


def _project(x, mod, gain, weights, tm):
    g, t, d = x.shape
    n_out = len(weights)
    tm = min(tm, t)
    in_specs = [pl.BlockSpec((1, tm, d), lambda i, j: (i, j, 0)),
                pl.BlockSpec((1, N_MOD, d), lambda i, j: (i, 0, 0)),
                pl.BlockSpec((1, d), lambda i, j: (0, 0))]
    in_specs += [pl.BlockSpec((d, width), lambda i, j, cb=cb: (0, cb)) for _, width, cb in weights]
    out_specs = [pl.BlockSpec((1, tm, width), lambda i, j: (i, j, 0)) for _, width, _ in weights]
    out_shape = [jax.ShapeDtypeStruct((g, t, width), F32) for _, width, _ in weights]
    return pl.pallas_call(
        functools.partial(_proj_kernel, n_out=n_out),
        grid=(g, t // tm),
        in_specs=in_specs, out_specs=out_specs, out_shape=out_shape,
        compiler_params=_cparams(("arbitrary", "arbitrary")),
        name="norm_project",
    )(x, mod, gain.reshape(1, d), *[w for w, _, _ in weights])


def _mlp_kernel(*refs, n_mix):
    x_ref, mod_ref, g_ref = refs[:3]
    o_refs = refs[3:3 + n_mix]
    wo_refs = refs[3 + n_mix:3 + 2 * n_mix]
    w1_ref, w2_ref, out_ref, x1_s, h_s, acc_s = refs[3 + 2 * n_mix:]
    f = pl.program_id(2)

    @pl.when(f == 0)
    def _():
        mod = mod_ref[0]
        mp = _dot(o_refs[0][0].astype(BF16), wo_refs[0][...])
        for o_ref, wo_ref in zip(o_refs[1:], wo_refs[1:]):
            mp += _dot(o_ref[0].astype(BF16), wo_ref[...])
        x1 = x_ref[0] + mod[2:3] * mp
        x1_s[...] = x1
        h = _rms_rows(x1, g_ref[...]) * (1.0 + mod[4:5]) + mod[3:4]
        h_s[...] = h.astype(BF16)
        acc_s[...] = jnp.zeros_like(acc_s)

    a = _dot(h_s[...], w1_ref[...])
    a = jnp.square(jnp.maximum(a, 0.0))
    acc_s[...] += _dot(a.astype(BF16), w2_ref[...])

    @pl.when(f == pl.num_programs(2) - 1)
    def _():
        out_ref[0] = x1_s[...] + mod_ref[0][5:6] * acc_s[...]


def _mix_mlp(x, mod, gain2, mixes, w_outs, w1, w2):
    g, t, d = x.shape
    dff = w1.shape[1]
    tm, tf = min(MLP_ROWS, t), min(MLP_FF_TILE, dff)
    n_mix = len(mixes)
    in_specs = [pl.BlockSpec((1, tm, d), lambda i, j, f: (i, j, 0)),
                pl.BlockSpec((1, N_MOD, d), lambda i, j, f: (i, 0, 0)),
                pl.BlockSpec((1, d), lambda i, j, f: (0, 0))]
    in_specs += [pl.BlockSpec((1, tm, o.shape[2]), lambda i, j, f: (i, j, 0)) for o in mixes]
    in_specs += [pl.BlockSpec(w.shape, lambda i, j, f: (0, 0)) for w in w_outs]
    in_specs += [pl.BlockSpec((d, tf), lambda i, j, f: (0, f)),
                 pl.BlockSpec((tf, d), lambda i, j, f: (f, 0))]
    return pl.pallas_call(
        functools.partial(_mlp_kernel, n_mix=n_mix),
        grid=(g, t // tm, dff // tf),
        in_specs=in_specs,
        out_specs=pl.BlockSpec((1, tm, d), lambda i, j, f: (i, j, 0)),
        out_shape=jax.ShapeDtypeStruct((g, t, d), F32),
        scratch_shapes=[pltpu.VMEM((tm, d), F32), pltpu.VMEM((tm, d), BF16), pltpu.VMEM((tm, d), F32)],
        compiler_params=_cparams(("arbitrary", "arbitrary", "arbitrary")),
        name="mix_mlp",
    )(x, mod, gain2.reshape(1, d), *mixes, *w_outs, w1, w2)


def _pair_ones():
    r = lax.broadcasted_iota(jnp.int32, (LANES, LANES), 0) // HEAD_DIM
    c = lax.broadcasted_iota(jnp.int32, (LANES, LANES), 1) // HEAD_DIM
    return jnp.where(r == c, 1.0, 0.0).astype(BF16)


def _head_rms(xb, gain, pair_ones):
    hi, lo = _split2(xb * xb)
    ss = _dot(jnp.concatenate([hi, lo], axis=1), jnp.concatenate([pair_ones, pair_ones], axis=0))
    return xb * lax.rsqrt(ss * (1.0 / HEAD_DIM) + EPS) * gain


def _dup_half(x, g):
    lane = lax.broadcasted_iota(jnp.int32, x.shape, 1)
    sw = pltpu.roll(x, HEAD_DIM, 1)
    if g == 0:
        return jnp.where(lane < HEAD_DIM, x, sw)
    return jnp.where(lane < HEAD_DIM, sw, x)


def _softmax_parts(scores):
    m = jnp.max(scores[0], axis=-1, keepdims=True)
    for s in scores[1:]:
        m = jnp.maximum(m, jnp.max(s, axis=-1, keepdims=True))
    ps = [jnp.exp2(s - m) for s in scores]
    l = jnp.sum(ps[0], axis=-1, keepdims=True)
    for p in ps[1:]:
        l += jnp.sum(p, axis=-1, keepdims=True)
    return ps, 1.0 / l


def _attend_pairs(jobs):
    tasks = []
    for qb, ks, vs, biases in jobs:
        lane = lax.broadcasted_iota(jnp.int32, qb.shape, 1)
        for hh in range(2):
            sel = (lane < HEAD_DIM) if hh == 0 else (lane >= HEAD_DIM)
            tasks.append(dict(q=jnp.where(sel, qb, jnp.zeros_like(qb)), ks=ks, vs=vs,
                              bias=None if biases is None else biases[hh]))
    for tk in tasks:
        scores = [_dot_nt(tk["q"], k) for k in tk["ks"]]
        if tk["bias"] is not None:
            scores = [s if b is None else s + b for s, b in zip(scores, tk["bias"])]
        tk["s"] = scores
    for tk in tasks:
        tk["p"], tk["inv"] = _softmax_parts(tk["s"])
    for tk in tasks:
        o = _dot(tk["p"][0].astype(BF16), tk["vs"][0])
        for p, v in zip(tk["p"][1:], tk["vs"][1:]):
            o += _dot(p.astype(BF16), v)
        tk["o"] = o * tk["inv"]
    outs = []
    for j in range(len(jobs)):
        o0, o1 = tasks[2 * j]["o"], tasks[2 * j + 1]["o"]
        lane_o = lax.broadcasted_iota(jnp.int32, o0.shape, 1)
        outs.append(jnp.where(lane_o < HEAD_DIM, o0, o1))
    return outs


_QA0, _KA, _VA, _QB0, _KB0, _VB0 = 0, 4, 5, 6, 10, 14


def _ctx_attn_kernel(qkv_ref, gain_ref, o_ref, nk_ref, nv_ref):
    ones = _pair_ones()

    def blk(j):
        return qkv_ref[0, :, j * LANES:(j + 1) * LANES]

    def gain(j):
        return gain_ref[:, j * LANES:(j + 1) * LANES]

    ka = _head_rms(blk(_KA), gain(_KA), ones)
    va = blk(_VA)
    nk_ref[0, :, 0:LANES] = ka
    nv_ref[0, :, 0:LANES] = va
    k2 = [_dup_half(ka, g).astype(BF16) for g in range(N_KV_A)]
    v2 = [_dup_half(va, g).astype(BF16) for g in range(N_KV_A)]
    jobs = []
    for j in range(4):
        qn = _head_rms(blk(_QA0 + j), gain(_QA0 + j), ones).astype(BF16)
        jobs.append((qn, [k2[j // 2]], [v2[j // 2]], None))
    for j, o in enumerate(_attend_pairs(jobs)):
        o_ref[0, :, j * LANES:(j + 1) * LANES] = o.astype(o_ref.dtype)
    jobs = []
    for j in range(4):
        kb = _head_rms(blk(_KB0 + j), gain(_KB0 + j), ones)
        vb = blk(_VB0 + j)
        nk_ref[0, :, (1 + j) * LANES:(2 + j) * LANES] = kb
        nv_ref[0, :, (1 + j) * LANES:(2 + j) * LANES] = vb
        qn = _head_rms(blk(_QB0 + j), gain(_QB0 + j), ones).astype(BF16)
        jobs.append((qn, [kb.astype(BF16)], [vb.astype(BF16)], None))
    for j, o in enumerate(_attend_pairs(jobs)):
        o_ref[0, :, (4 + j) * LANES:(5 + j) * LANES] = o.astype(o_ref.dtype)


def _ctx_attention(qkv, gain_row):
    b, t, n = qkv.shape
    n_cache = (N_KV_A + N_HEADS_B) * HEAD_DIM
    n_o = (N_HEADS_A + N_HEADS_B) * HEAD_DIM
    return pl.pallas_call(
        _ctx_attn_kernel,
        grid=(b,),
        in_specs=[pl.BlockSpec((1, t, n), lambda i: (i, 0, 0)),
                  pl.BlockSpec((1, n), lambda i: (0, 0))],
        out_specs=[pl.BlockSpec((1, t, n_o), lambda i: (i, 0, 0)),
                   pl.BlockSpec((1, t, n_cache), lambda i: (i, 0, 0)),
                   pl.BlockSpec((1, t, n_cache), lambda i: (i, 0, 0))],
        out_shape=[jax.ShapeDtypeStruct((b, t, n_o), BF16),
                   jax.ShapeDtypeStruct((b, t, n_cache), F32),
                   jax.ShapeDtypeStruct((b, t, n_cache), F32)],
        compiler_params=_cparams(("arbitrary",)),
        name="ctx_attention",
    )(qkv, gain_row)


def _rope(x, c, sa, sb):
    return x * c + pltpu.roll(x, LANES - 16, 1) * sa + pltpu.roll(x, 16, 1) * sb


def _lat_a_kernel(q_ref, k_ref, v_ref, ck_ref, cv_ref, gq_ref, gk_ref,
                  cq_ref, saq_ref, sbq_ref, ck_t_ref, sak_t_ref, sbk_t_ref,
                  o_ref, k2_s, v2_s):
    t = k_ref.shape[1]
    ones = _pair_ones()

    @pl.when(pl.program_id(1) == 0)
    def _():
        kn = _head_rms(k_ref[0], gk_ref[...], ones)
        kr = _rope(kn, ck_t_ref[...], sak_t_ref[...], sbk_t_ref[...])
        v = v_ref[0]
        ck = ck_ref[0]
        cv = cv_ref[0]
        for g in range(N_KV_A):
            k2_s[g, 0:t, :] = _dup_half(kr, g).astype(BF16)
            k2_s[g, t:, :] = _dup_half(ck, g).astype(BF16)
            v2_s[g, 0:t, :] = _dup_half(v, g).astype(BF16)
            v2_s[g, t:, :] = _dup_half(cv, g).astype(BF16)

    c, sa, sb = cq_ref[...], saq_ref[...], sbq_ref[...]
    for j0 in range(0, 4, LAT_A_JOBS):
        jobs = []
        for j in range(j0, j0 + LAT_A_JOBS):
            qn = _head_rms(q_ref[0, :, j * LANES:(j + 1) * LANES], gq_ref[:, j * LANES:(j + 1) * LANES], ones)
            jobs.append((_rope(qn, c, sa, sb).astype(BF16), [k2_s[j // 2]], [v2_s[j // 2]], None))
        for j, o in zip(range(j0, j0 + LAT_A_JOBS), _attend_pairs(jobs)):
            o_ref[0, :, j * LANES:(j + 1) * LANES] = o.astype(o_ref.dtype)


def _rope_tables(t):
    pos = jnp.arange(t, dtype=jnp.int32)
    rc = jnp.stack([pos // GRID_W, pos % GRID_W], axis=-1).astype(F32)
    axis_dim = HEAD_DIM // 2
    inv_freq = 1.0 / (ROPE_THETA ** (jnp.arange(0, axis_dim, 2, dtype=F32) / axis_dim))
    ang = rc[:, :, None] * inv_freq
    cos, sin = jnp.cos(ang), jnp.sin(ang)
    zero = jnp.zeros_like(sin)
    c = jnp.stack([cos, cos], axis=2).reshape(t, HEAD_DIM)
    sa = jnp.stack([-sin, zero], axis=2).reshape(t, HEAD_DIM)
    sb = jnp.stack([zero, sin], axis=2).reshape(t, HEAD_DIM)
    two = lambda a: jnp.concatenate([a, a], axis=-1)
    return two(c), two(sa), two(sb)


def _latent_attention_a(qkv, cache_k, cache_v, gain_row):
    b, t, _ = qkv.shape
    p = cache_k.shape[1]
    tq = min(LAT_A_QUERY_TILE, t)
    c, sa, sb = _rope_tables(t)
    gq = gain_row[:, _QA0 * LANES:(_QA0 + 4) * LANES]
    gk = gain_row[:, _KA * LANES:(_KA + 1) * LANES]
    tab_q = pl.BlockSpec((tq, LANES), lambda i, j: (j, 0))
    tab_k = pl.BlockSpec((t, LANES), lambda i, j: (0, 0))
    return pl.pallas_call(
        _lat_a_kernel,
        grid=(b, t // tq),
        in_specs=[pl.BlockSpec((1, tq, 4 * LANES), lambda i, j: (i, j, 0)),
                  pl.BlockSpec((1, t, LANES), lambda i, j: (i, 0, _KA)),
                  pl.BlockSpec((1, t, LANES), lambda i, j: (i, 0, _VA)),
                  pl.BlockSpec((1, p, LANES), lambda i, j: (i, 0, 0)),
                  pl.BlockSpec((1, p, LANES), lambda i, j: (i, 0, 0)),
                  pl.BlockSpec((1, 4 * LANES), lambda i, j: (0, 0)),
                  pl.BlockSpec((1, LANES), lambda i, j: (0, 0)),
                  tab_q, tab_q, tab_q, tab_k, tab_k, tab_k],
        out_specs=pl.BlockSpec((1, tq, 4 * LANES), lambda i, j: (i, j, 0)),
        out_shape=jax.ShapeDtypeStruct((b, t, 4 * LANES), BF16),
        scratch_shapes=[pltpu.VMEM((N_KV_A, t + p, LANES), BF16), pltpu.VMEM((N_KV_A, t + p, LANES), BF16)],
        compiler_params=_cparams(("arbitrary", "arbitrary")),
        name="latent_attention_a",
    )(qkv, qkv, qkv, cache_k, cache_v, gq, gk, c, sa, sb, c, sa, sb)


def _nbr_bias_table(rel_bias, rows):
    wr = min(WIN_R, rows)
    cols = jnp.arange(GRID_W, dtype=jnp.int32)
    col_start = jnp.clip(cols - WIN_C // 2, 0, GRID_W - WIN_C)
    kc = jnp.arange(GRID_W, dtype=jnp.int32)
    in_win = (kc[None, :] >= col_start[:, None]) & (kc[None, :] < col_start[:, None] + WIN_C)
    nh, n_ro = rel_bias.shape[0], 2 * wr - 1
    rb = rel_bias[:, WIN_R - wr:WIN_R + wr - 1, :].astype(F32) * LOG2_E
    span = 2 * GRID_W
    lead = GRID_W - WIN_C
    v = jnp.pad(rb, ((0, 0), (0, 0), (lead, span - lead - (2 * WIN_C - 1))))
    flat = jnp.tile(v, (1, 1, GRID_W))
    toep = flat[..., GRID_W - 1:GRID_W - 1 + GRID_W * (span - 1)].reshape(nh, n_ro, GRID_W, span - 1)[..., :GRID_W]
    base = jnp.where(in_win[None, None], toep, NEG_BIG)
    return jnp.concatenate([base[:, :-1], base[:, 1:]], axis=-1)


def _nbr_kernel(q_ref, k_ref, v_ref, ck_ref, cv_ref, gq_ref, gk_ref, bias_ref, o_ref, qn_s, kn_s, vn_s):
    t = q_ref.shape[1]
    rows = t // GRID_W
    wr = min(WIN_R, rows)
    ones = _pair_ones()
    qn_s[...] = _head_rms(q_ref[0], gq_ref[0], ones).astype(BF16)
    kn_s[...] = _head_rms(k_ref[0], gk_ref[0], ones).astype(BF16)
    vn_s[...] = v_ref[0].astype(BF16)
    ck = ck_ref[0].astype(BF16)
    cv = cv_ref[0].astype(BF16)
    rb = math.gcd(rows, NBR_ROW_GROUP)

    def row_group(gi, carry):
        jobs, q0s = [], []
        for rr in range(rb):
            r = gi * rb + rr
            rs = jnp.clip(r - wr // 2, 0, rows - wr)
            var = r - rs
            q0 = pl.multiple_of(r * GRID_W, GRID_W)
            k0 = pl.multiple_of(rs * GRID_W, GRID_W)
            biases = [[jnp.concatenate([bias_ref[hh, wr - 1 - var + 2 * m] for m in range(wr // 2)], axis=-1), None]
                      for hh in range(2)]
            jobs.append((qn_s[pl.ds(q0, GRID_W), :], [kn_s[pl.ds(k0, wr * GRID_W), :], ck],
                         [vn_s[pl.ds(k0, wr * GRID_W), :], cv], biases))
            q0s.append(q0)
        for q0, o in zip(q0s, _attend_pairs(jobs)):
            o_ref[0, pl.ds(q0, GRID_W), :] = o.astype(o_ref.dtype)
        return carry

    lax.fori_loop(0, rows // rb, row_group, 0)


def _latent_attention_b(qkv, cache_k, cache_v, gain_row, rel_bias):
    b, t, _ = qkv.shape
    p = cache_k.shape[1]
    rows = t // GRID_W
    wr = min(WIN_R, rows)
    assert wr % 2 == 0
    n_tab = 2 * wr - 2
    table = _nbr_bias_table(rel_bias, rows).reshape(N_HEADS_B // 2, 2, n_tab, GRID_W, 2 * GRID_W)
    gq = gain_row[:, _QB0 * LANES:(_QB0 + 4) * LANES].reshape(4, 1, LANES)
    gk = gain_row[:, _KB0 * LANES:(_KB0 + 4) * LANES].reshape(4, 1, LANES)
    return pl.pallas_call(
        _nbr_kernel,
        grid=(b, N_HEADS_B // 2),
        in_specs=[pl.BlockSpec((1, t, LANES), lambda i, j: (i, 0, _QB0 + j)),
                  pl.BlockSpec((1, t, LANES), lambda i, j: (i, 0, _KB0 + j)),
                  pl.BlockSpec((1, t, LANES), lambda i, j: (i, 0, _VB0 + j)),
                  pl.BlockSpec((1, p, LANES), lambda i, j: (i, 0, 1 + j)),
                  pl.BlockSpec((1, p, LANES), lambda i, j: (i, 0, 1 + j)),
                  pl.BlockSpec((1, 1, LANES), lambda i, j: (j, 0, 0)),
                  pl.BlockSpec((1, 1, LANES), lambda i, j: (j, 0, 0)),
                  pl.BlockSpec((None, 2, n_tab, GRID_W, 2 * GRID_W), lambda i, j: (j, 0, 0, 0, 0))],
        out_specs=pl.BlockSpec((1, t, LANES), lambda i, j: (i, 0, j)),
        out_shape=jax.ShapeDtypeStruct((b, t, 4 * LANES), BF16),
        scratch_shapes=[pltpu.VMEM((t, LANES), BF16)] * 3,
        compiler_params=_cparams(("arbitrary", "arbitrary")),
        name="latent_attention_b",
    )(qkv, qkv, qkv, cache_k, cache_v, gq, gk, table)


def _delta_kernel(*refs, hb, hg, cg, has_s0, emit_state):
    (q_ref, k_ref, v_ref, cwq_ref, cwk_ref, cwv_ref, ab_ref, alog_ref, dtb_ref, z_ref, onorm_ref), rest = (
        refs[:11], refs[11:])
    s0_ref = None
    if has_s0:
        s0_ref, rest = rest[0], rest[1:]
    y_ref, rest = rest[0], rest[1:]
    sfin_ref = None
    if emit_state:
        sfin_ref, rest = rest[0], rest[1:]
    q_s, k_s, v_s, gates_s, u_s, wq_s, kt_s, att_s, el_s, o_s, s_s = rest
    pad_s = o_s.at[0]
    t = q_ref.shape[1]
    c = DELTA_CHUNK
    n = t // c
    h0 = pl.program_id(1) * hb
    rbs = min(t, DELTA_ROW_BLOCK)
    nrb = t // rbs
    unroll = min(4, nrb)

    pad_s[0:8, :] = jnp.zeros((8, LANES), F32)
    pad_s[t + 8:t + 16, :] = jnp.zeros((8, LANES), F32)

    def conv_silu_norm(x_ref, cw_ref, hd, dst, scale):
        for i in range(nrb):
            pad_s[8 + i * rbs:8 + (i + 1) * rbs, :] = x_ref[0, i * rbs:(i + 1) * rbs, hd * LANES:(hd + 1) * LANES]
        cw = cw_ref[:, hd * LANES:(hd + 1) * LANES]

        def blk(i, carry):
            r = pl.multiple_of(i * rbs, rbs)
            y = (pad_s[pl.ds(r + 7, rbs), :] * cw[0:1] + pad_s[pl.ds(r + 8, rbs), :] * cw[1:2]
                 + pad_s[pl.ds(r + 9, rbs), :] * cw[2:3])
            y = y * jax.nn.sigmoid(y)
            if scale is not None:
                y = y * lax.rsqrt(jnp.sum(y * y, axis=-1, keepdims=True) + EPS) * scale
            dst[hd, pl.ds(r, rbs), :] = y
            return carry

        lax.fori_loop(0, nrb, blk, 0, unroll=unroll)

    for hd in range(hb):
        conv_silu_norm(q_ref, cwq_ref, hd, q_s, DK_C ** -0.5)
        conv_silu_norm(k_ref, cwk_ref, hd, k_s, 1.0)
        conv_silu_norm(v_ref, cwv_ref, hd, v_s, None)

    def gate_blk(i, carry):
        r = pl.multiple_of(i * rbs, rbs)
        ab = ab_ref[0, pl.ds(r, rbs), :]
        xa = ab + dtb_ref[...]
        softplus = jnp.maximum(xa, 0.0) + jnp.log1p(jnp.exp(-jnp.abs(xa)))
        lane_t = lax.broadcasted_iota(jnp.int32, ab.shape, 1)
        gates_s[pl.ds(r, rbs), :] = jnp.where(lane_t < 2 * N_HEADS_C, -jnp.exp(alog_ref[...]) * softplus,
                                              jax.nn.sigmoid(ab))
        return carry

    lax.fori_loop(0, nrb, gate_blk, 0, unroll=unroll)

    ii = lax.broadcasted_iota(jnp.int32, (c, c), 0)
    jj = lax.broadcasted_iota(jnp.int32, (c, c), 1)
    eye = jnp.where(ii == jj, 1.0, 0.0)
    ones_cc = jnp.ones((c, c), BF16)
    bd8 = (ii // 8) == (jj // 8)
    offs = []
    blk = 8
    while blk < c:
        offs.append((blk, ((ii // (2 * blk)) == (jj // (2 * blk))) & ((ii // blk) != (jj // blk))))
        blk *= 2
    dir_masks = []
    for d in range(2):
        if d == 0:
            incl, strict, incl_t = ii >= jj, ii > jj, ii <= jj
        else:
            incl, strict, incl_t = ii <= jj, ii < jj, ii >= jj
        dir_masks.append((incl, strict, incl_t, jnp.where(incl, 1.0, 0.0).astype(BF16)))
    pick_r = lax.broadcasted_iota(jnp.int32, (LANES, 4 * LANES), 0)
    pick_c = lax.broadcasted_iota(jnp.int32, (LANES, 4 * LANES), 1)
    n_cgroups = n // cg

    def prep_unit(gi, carry):
        hg0 = (gi // n_cgroups) * hg
        c0 = (gi % n_cgroups) * cg
        chains = []
        for hh in range(hg):
            hd = hg0 + hh
            pick = jnp.where(pick_r == (pick_c // LANES) * N_HEADS_C + h0 + hd, 1.0, 0.0).astype(BF16)
            for cc in range(cg):
                ci = c0 + cc
                r0 = pl.multiple_of(ci * c, c)
                qb, kb = q_s[hd, pl.ds(r0, c), :].astype(BF16), k_s[hd, pl.ds(r0, c), :].astype(BF16)
                kk, qk = _dot_nt(kb, kb), _dot_nt(qb, kb)
                picked = _dot2_rhs01(gates_s[pl.ds(r0, c), :], pick)
                for d in range(2):
                    chains.append(dict(hd=hd, ci=ci, r0=r0, d=d, kk=kk, qk=qk,
                                       la=picked[:, d * LANES:(d + 1) * LANES],
                                       be=picked[:, (2 + d) * LANES:(3 + d) * LANES]))
        for ch in chains:
            incl, strict, incl_t, cum_mask = dir_masks[ch["d"]]
            ch["g_col"] = _dot2_lhs01(cum_mask, ch["la"])
            ch["g_row"] = _dot2_lhs01(ones_cc, jnp.where(incl_t, ch["la"], 0.0))
        for ch in chains:
            incl, strict, incl_t, cum_mask = dir_masks[ch["d"]]
            ch["decay"] = jnp.where(incl, jnp.exp(jnp.where(incl, ch["g_col"] - ch["g_row"], 0.0)), 0.0)
            ch["a"] = jnp.where(strict, (ch["kk"] * ch["decay"]) * ch["be"], 0.0)
            ch["p8"] = jnp.where(bd8, -ch["a"], 0.0)
        for ch, pk2 in zip(chains, _pair_dots([ch["p8"] for ch in chains], [ch["p8"] for ch in chains])):
            ch["pk2"] = pk2
        for ch in chains:
            ch["acc"] = eye + ch["p8"]
            ch["r"] = _dot(ch["pk2"].astype(BF16), jnp.concatenate([ch["pk2"], ch["acc"]], axis=1).astype(BF16))
        for ch in chains:
            ch["acc"] = ch["acc"] + ch["r"][:, c:]
        for ch, x in zip(chains, _pair_dots([ch["r"][:, :c] for ch in chains], [ch["acc"] for ch in chains])):
            ch["t"] = ch["acc"] + x
        for blk, off in offs:
            a_sel = [_row_blocks(jnp.where(off, ch["a"], 0.0), blk, ch["d"] == 1) for ch in chains]
            for ch, y in zip(chains, _pair_dots(a_sel, [ch["t"] for ch in chains])):
                ch["y"] = _row_blocks_put(y, jnp.zeros((c, c), F32), blk, ch["d"] == 1)
            t_sel = [_row_blocks(ch["t"], blk, ch["d"] == 1) for ch in chains]
            for ch, ts, x in zip(chains, t_sel, _pair_dots(t_sel, [ch["y"] for ch in chains])):
                ch["t"] = _row_blocks_put(ts - x, ch["t"], blk, ch["d"] == 1)
        for ch in chains:
            ch["eg"] = jnp.exp(ch["g_col"])
            ch["kc"] = k_s[ch["hd"], pl.ds(ch["r0"], c), :]
            vc = v_s[ch["hd"], pl.ds(ch["r0"], c), :]
            rhs = jnp.concatenate([vc * ch["be"], ch["kc"] * ch["be"] * ch["eg"]], axis=1)
            ch["uw"] = _dot2_lhs_bf16(ch["t"], rhs)
        for ch in chains:
            d, r0, ci, g_col = ch["d"], ch["r0"], ch["ci"], ch["g_col"]
            hx = ch["hd"] * 2 + d
            g_last = g_col[c - 1:c, :] if d == 0 else g_col[0:1, :]
            u_s[hx, pl.ds(r0, c), :] = ch["uw"][:, :c]
            r2 = pl.multiple_of(ci * 2 * c, 2 * c)
            wq_s[hx, pl.ds(r2, c), :] = ch["uw"][:, c:].astype(BF16)
            wq_s[hx, pl.ds(r2 + c, c), :] = (q_s[ch["hd"], pl.ds(r0, c), :] * ch["eg"]).astype(BF16)
            kt_s[hx, pl.ds(r0, c), :] = (ch["kc"] * jnp.exp(g_last - g_col)).astype(BF16)
            att_s[hx, pl.ds(r0, c), :] = (ch["qk"] * ch["decay"]).astype(BF16)
            r8 = pl.multiple_of(ci * 8, 8)
            el_s[hx, pl.ds(r8, 8), :] = jnp.broadcast_to(jnp.exp(g_last), (8, LANES))
        return carry

    def scan_step(i, carry):
        items = []
        for hx in range(2 * hb):
            ci = i if hx % 2 == 0 else n - 1 - i
            items.append(dict(hx=hx, s=s_s[hx], r0=pl.multiple_of(ci * c, c), r2=pl.multiple_of(ci * 2 * c, 2 * c),
                              r8=pl.multiple_of(ci * 8, 8)))
        for it in items:
            it["ws"] = _dot(wq_s[it["hx"], pl.ds(it["r2"], 2 * c), :], it["s"].astype(BF16))
        for it in items:
            it["vb"] = (u_s[it["hx"], pl.ds(it["r0"], c), :] - it["ws"][:c]).astype(BF16)
        for it in items:
            hx = it["hx"]
            o_s[hx // 2, pl.ds(it["r0"], c), :] += (it["ws"][c:]
                                                    + _dot(att_s[hx, pl.ds(it["r0"], c), :], it["vb"]))
            s_s[hx] = (it["s"] * el_s[hx, pl.ds(it["r8"], 1), :]
                       + _dot_tn(kt_s[hx, pl.ds(it["r0"], c), :], it["vb"]))
        return carry

    for hx in range(2 * hb):
        if has_s0:
            s_s[hx] = s0_ref[0, hx % 2, hx // 2]
        else:
            s_s[hx] = jnp.zeros((DK_C, DV_C), F32)

    def zero_blk(i, carry):
        for hd in range(hb):
            o_s[hd, pl.ds(pl.multiple_of(i * rbs, rbs), rbs), :] = jnp.zeros((rbs, LANES), F32)
        return carry

    lax.fori_loop(0, nrb, zero_blk, 0)

    lax.fori_loop(0, (hb // hg) * n_cgroups, prep_unit, 0)
    lax.fori_loop(0, n, scan_step, 0)

    if emit_state:
        for hx in range(2 * hb):
            sfin_ref[0, hx % 2, hx // 2] = s_s[hx]

    for hd in range(hb):
        def out_blk(i, carry, hd=hd):
            r = pl.multiple_of(i * rbs, rbs)
            o = o_s[hd, pl.ds(r, rbs), :]
            z = z_ref[0, pl.ds(r, rbs), hd * LANES:(hd + 1) * LANES]
            y = _rms_rows(o, onorm_ref[...]) * (z * jax.nn.sigmoid(z))
            y_ref[0, pl.ds(r, rbs), hd * LANES:(hd + 1) * LANES] = y.astype(y_ref.dtype)
            return carry

        lax.fori_loop(0, nrb, out_blk, 0, unroll=unroll)


def _delta_mixer(qkv, z, ab, conv_w, alog_row, dtb_row, out_norm, s0, hb, chains, emit_state):
    b, t, _ = qkv.shape
    nh = N_HEADS_C
    c = DELTA_CHUNK
    n = t // c
    hg = min(hb, chains // 2)
    cg = max(k for k in range(1, n + 1) if n % k == 0 and 2 * hg * k <= max(chains, 2 * hg))
    assert hb % hg == 0 and n % cg == 0 and nh % hb == 0
    w = hb * LANES
    has_s0 = s0 is not None
    col = lambda off: pl.BlockSpec((1, t, w), lambda i, j, off=off: (i, 0, off + j))
    cw = lambda off: pl.BlockSpec((3, w), lambda i, j, off=off: (0, off + j))
    row = pl.BlockSpec((1, LANES), lambda i, j: (0, 0))
    st = pl.BlockSpec((1, 2, hb, DK_C, DV_C), lambda i, j: (i, 0, j, 0, 0))
    nb = nh // hb
    in_specs = [col(0), col(nb), col(2 * nb), cw(0), cw(nb), cw(2 * nb),
                pl.BlockSpec((1, t, LANES), lambda i, j: (i, 0, 0)), row, row, col(0), row]
    args = [qkv, qkv, qkv, conv_w, conv_w, conv_w, ab, alog_row, dtb_row, z, out_norm.reshape(1, DV_C)]
    if has_s0:
        in_specs.append(st)
        args.append(s0)
    out_specs = [col(0)]
    out_shape = [jax.ShapeDtypeStruct((b, t, nh * DV_C), BF16)]
    if emit_state:
        out_specs.append(st)
        out_shape.append(jax.ShapeDtypeStruct((b, 2, nh, DK_C, DV_C), F32))
    hd2 = 2 * hb
    scratch = [pltpu.VMEM((hb, t, LANES), F32), pltpu.VMEM((hb, t, LANES), F32), pltpu.VMEM((hb, t, LANES), F32),
               pltpu.VMEM((t, LANES), F32),
               pltpu.VMEM((hd2, t, LANES), F32), pltpu.VMEM((hd2, 2 * t, LANES), BF16),
               pltpu.VMEM((hd2, t, LANES), BF16), pltpu.VMEM((hd2, t, LANES), BF16),
               pltpu.VMEM((hd2, 8 * n, LANES), F32), pltpu.VMEM((hb, t + 16, LANES), F32),
               pltpu.VMEM((hd2, DK_C, DV_C), F32)]
    return pl.pallas_call(
        functools.partial(_delta_kernel, hb=hb, hg=hg, cg=cg, has_s0=has_s0, emit_state=emit_state),
        grid=(b, nb),
        in_specs=in_specs, out_specs=out_specs, out_shape=out_shape, scratch_shapes=scratch,
        compiler_params=_cparams(("arbitrary", "arbitrary"), DELTA_VMEM_LIMIT),
        name="delta_mixer",
    )(*args)


def _pad_lanes(x, n):
    return jnp.pad(x, ((0, 0), (0, n - x.shape[1])))


def kernel(x_prompt, x_sample, c, cache_l0_k, cache_l0_v, state_l1, c_ctx, l0_mod_w, l0_mod_b, l0_norm1, l0_w_in, l0_q_norm_a, l0_k_norm_a, l0_q_norm_b, l0_k_norm_b, l0_rel_bias, l0_w_out, l0_norm2, l0_mlp_w1, l0_mlp_w2, l1_mod_w, l1_mod_b, l1_norm1, l1_w_in, l1_conv_w, l1_a_log, l1_dt_bias, l1_out_norm, l1_w_out, l1_norm2, l1_mlp_w1, l1_mlp_w2):
    bp, tp, d = x_prompt.shape
    bs, ts, _ = x_sample.shape
    n_cache = (N_KV_A + N_HEADS_B) * HEAD_DIM
    bf = lambda w: w.astype(BF16)

    n_rows = -(-(1 + bs) // 8) * 8
    cvec = jnp.concatenate([c_ctx[None, :], c, jnp.zeros((n_rows - 1 - bs, d), F32)], axis=0)
    mods = []
    for mw, mb in ((l0_mod_w, l0_mod_b), (l1_mod_w, l1_mod_b)):
        m = _modulation(cvec, mw, mb).reshape(n_rows, N_MOD, d)
        mods.append((m[0:1], m[1:1 + bs]))

    xp = x_prompt.reshape(1, bp * tp, d)
    xs = x_sample

    scale = HEAD_DIM ** -0.5 * LOG2_E
    ones_a = jnp.ones((N_KV_A * HEAD_DIM,), F32)
    ones_b = jnp.ones((N_HEADS_B * HEAD_DIM,), F32)
    gain_row = jnp.concatenate([jnp.tile(l0_q_norm_a, N_HEADS_A) * scale, jnp.tile(l0_k_norm_a, N_KV_A), ones_a,
                                jnp.tile(l0_q_norm_b, N_HEADS_B) * scale, jnp.tile(l0_k_norm_b, N_HEADS_B),
                                ones_b])[None, :]
    w_in0 = [(bf(l0_w_in), l0_w_in.shape[1], 0)]
    w1_0, w2_0 = bf(l0_mlp_w1), bf(l0_mlp_w2)
    half = N_HEADS_A * HEAD_DIM

    mod_p, mod_s = mods[0]
    (qkv_p,) = _project(xp, mod_p, l0_norm1, w_in0, PROJ_ROWS_L0)
    o_p, new_k, new_v = _ctx_attention(qkv_p.reshape(bp, tp, -1), gain_row)
    xp = _mix_mlp(xp, mod_p, l0_norm2, [o_p.reshape(1, bp * tp, -1)], [bf(l0_w_out)], w1_0, w2_0)

    (qkv_s,) = _project(xs, mod_s, l0_norm1, w_in0, PROJ_ROWS_L0)
    ck = cache_l0_k.reshape(bs, -1, n_cache)
    cv = cache_l0_v.reshape(bs, -1, n_cache)
    o_a = _latent_attention_a(qkv_s, ck, cv, gain_row)
    o_b = _latent_attention_b(qkv_s, ck, cv, gain_row, l0_rel_bias)
    xs = _mix_mlp(xs, mod_s, l0_norm2, [o_a, o_b], [bf(l0_w_out[:half]), bf(l0_w_out[half:])], w1_0, w2_0)

    n_qkv = N_HEADS_C * (2 * DK_C + DV_C)
    n_z = N_HEADS_C * DV_C
    assert n_qkv % n_z == 0
    w_in1 = bf(l1_w_in)
    w_pieces = [(w_in1, n_qkv, 0), (w_in1, n_z, n_qkv // n_z),
                (bf(_pad_lanes(l1_w_in[:, n_qkv + n_z:], LANES)), LANES, 0)]
    w_out1 = bf(l1_w_out)
    w1_1, w2_1 = bf(l1_mlp_w1), bf(l1_mlp_w2)
    alog_row = _pad_lanes(l1_a_log.reshape(1, -1), LANES)
    dtb_row = _pad_lanes(l1_dt_bias.reshape(1, -1), LANES)

    mod_p, mod_s = mods[1]
    qkv1_p, z_p, ab_p = _project(xp, mod_p, l1_norm1, w_pieces, PROJ_ROWS_L1)
    y_p, new_s = _delta_mixer(qkv1_p.reshape(bp, tp, -1), z_p.reshape(bp, tp, -1), ab_p.reshape(bp, tp, -1),
                              l1_conv_w, alog_row, dtb_row, l1_out_norm, None, hb=DELTA_HEADS_PROMPT,
                              chains=DELTA_PREP_CHAINS_PROMPT, emit_state=True)
    xp = _mix_mlp(xp, mod_p, l1_norm2, [y_p.reshape(1, bp * tp, -1)], [w_out1], w1_1, w2_1)

    qkv1_s, z_s, ab_s = _project(xs, mod_s, l1_norm1, w_pieces, PROJ_ROWS_L1)
    (y_s,) = _delta_mixer(qkv1_s, z_s, ab_s, l1_conv_w, alog_row, dtb_row, l1_out_norm, state_l1.astype(F32),
                          hb=DELTA_HEADS_SAMPLE, chains=DELTA_PREP_CHAINS_SAMPLE, emit_state=False)
    xs = _mix_mlp(xs, mod_s, l1_norm2, [y_s], [w_out1], w1_1, w2_1)

    return (xp.reshape(bp, tp, d), xs,
            new_k.reshape(bp, tp, N_KV_A + N_HEADS_B, HEAD_DIM), new_v.reshape(bp, tp, N_KV_A + N_HEADS_B, HEAD_DIM),
            new_s.astype(x_prompt.dtype))
```

```python
import functools
import math

import jax
import jax.numpy as jnp
from jax import lax
from jax.experimental import pallas as pl
from jax.experimental.pallas import tpu as pltpu

F32 = jnp.float32
BF16 = jnp.bfloat16

EPS = 1e-6
N_MOD = 6
HEAD_DIM = 64
N_HEADS_A = 8
N_KV_A = 2
N_HEADS_B = 8
GRID_W = 64
WIN_R = 8
WIN_C = 16
ROPE_THETA = 10000.0
N_HEADS_C = 8
DK_C = 128
DV_C = 128
NEG_BIG = -1e30
LOG2_E = math.log2(math.e)

LANES = 128
VMEM_LIMIT = 48 * 1024 * 1024
DELTA_VMEM_LIMIT = 56 * 1024 * 1024
PROJ_ROWS_L0 = 1024
PROJ_ROWS_L1 = 512
MLP_ROWS = 1024
MLP_FF_TILE = 1024
LAT_A_QUERY_TILE = 512
LAT_A_JOBS = 2
NBR_ROW_GROUP = 16
DELTA_CHUNK = LANES
DELTA_PREP_CHAINS_PROMPT = 32
DELTA_PREP_CHAINS_SAMPLE = 16
DELTA_ROW_BLOCK = 256
DELTA_HEADS_PROMPT = 8
DELTA_HEADS_SAMPLE = 2


def _cparams(sem, limit=VMEM_LIMIT):
    return pltpu.CompilerParams(dimension_semantics=sem, vmem_limit_bytes=limit)


def _dot(a, b):
    return jnp.dot(a, b, preferred_element_type=F32)


def _dot_nt(a, b):
    return lax.dot_general(a, b, (((1,), (1,)), ((), ())), preferred_element_type=F32)


def _dot_tn(a, b):
    return lax.dot_general(a, b, (((0,), (0,)), ((), ())), preferred_element_type=F32)


def _split2(x):
    hi = x.astype(BF16)
    lo = (x - hi.astype(F32)).astype(BF16)
    return hi, lo


def _dot2_rhs01(a_f32, b01):
    hi, lo = _split2(a_f32)
    return _dot(jnp.concatenate([hi, lo], axis=1), jnp.concatenate([b01, b01], axis=0))


def _dot2_lhs01(a01, b_f32):
    hi, lo = _split2(b_f32)
    return _dot(jnp.concatenate([a01, a01], axis=1), jnp.concatenate([hi, lo], axis=0))


def _dot2_lhs_bf16(a_f32, b_f32):
    ab = a_f32.astype(BF16)
    hi, lo = _split2(b_f32)
    return _dot(jnp.concatenate([ab, ab], axis=1), jnp.concatenate([hi, lo], axis=0))


def _pair_dots(lhs, rhs):
    outs = []
    for i in range(0, len(lhs), 2):
        r0, r1 = rhs[i].astype(BF16), rhs[i + 1].astype(BF16)
        z = jnp.zeros_like(r0)
        rr = jnp.concatenate([jnp.concatenate([r0, z], axis=1), jnp.concatenate([z, r1], axis=1)], axis=0)
        x = _dot(jnp.concatenate([lhs[i].astype(BF16), lhs[i + 1].astype(BF16)], axis=1), rr)
        outs += [x[:, :LANES], x[:, LANES:]]
    return outs


def _row_blocks(x, blk, first):
    return jnp.concatenate([x[i * blk:(i + 1) * blk] for i in range(0 if first else 1, x.shape[0] // blk, 2)], axis=0)


def _row_blocks_put(sel, rest, blk, first):
    pieces = []
    for i in range(rest.shape[0] // blk):
        if (i % 2 == 0) == first:
            pieces.append(sel[(i // 2) * blk:(i // 2 + 1) * blk])
        else:
            pieces.append(rest[i * blk:(i + 1) * blk])
    return jnp.concatenate(pieces, axis=0)


def _rms_rows(x, gain):
    ms = jnp.mean(x * x, axis=-1, keepdims=True)
    return x * lax.rsqrt(ms + EPS) * gain


def _mod_kernel(c_ref, w_ref, b_ref, o_ref):
    c = c_ref[...]
    a = (c * jax.nn.sigmoid(c)).astype(BF16)
    o_ref[...] = _dot(a, w_ref[...].astype(BF16)) + b_ref[...]


def _modulation(cvec, w, b):
    rows, d = cvec.shape
    n = w.shape[1]
    tn = n // 4
    return pl.pallas_call(
        _mod_kernel,
        grid=(n // tn,),
        in_specs=[pl.BlockSpec((rows, d), lambda j: (0, 0)),
                  pl.BlockSpec((d, tn), lambda j: (0, j)),
                  pl.BlockSpec((1, tn), lambda j: (0, j))],
        out_specs=pl.BlockSpec((rows, tn), lambda j: (0, j)),
        out_shape=jax.ShapeDtypeStruct((rows, n), F32),
        compiler_params=_cparams(("arbitrary",)),
        name="modulation",
    )(cvec, w, b.reshape(1, n))


def _proj_kernel(x_ref, mod_ref, g_ref, *rest, n_out):
    w_refs, o_refs = rest[:n_out], rest[n_out:]
    mod = mod_ref[0]
    h = _rms_rows(x_ref[0], g_ref[...]) * (1.0 + mod[1:2]) + mod[0:1]
    hb = h.astype(BF16)
    for w_ref, o_ref in zip(w_refs, o_refs):
        o_ref[0] = _dot(hb, w_ref[...]).astype(o_ref.dtype)


def _project(x, mod, gain, weights, tm):
    g, t, d = x.shape
    n_out = len(weights)
    tm = min(tm, t)
    in_specs = [pl.BlockSpec((1, tm, d), lambda i, j: (i, j, 0)),
                pl.BlockSpec((1, N_MOD, d), lambda i, j: (i, 0, 0)),
                pl.BlockSpec((1, d), lambda i, j: (0, 0))]
    in_specs += [pl.BlockSpec(w.shape, lambda i, j: (0, 0)) for w in weights]
    out_specs = [pl.BlockSpec((1, tm, w.shape[1]), lambda i, j: (i, j, 0)) for w in weights]
    out_shape = [jax.ShapeDtypeStruct((g, t, w.shape[1]), F32) for w in weights]
    return pl.pallas_call(
        functools.partial(_proj_kernel, n_out=n_out),
        grid=(g, t // tm),
        in_specs=in_specs, out_specs=out_specs, out_shape=out_shape,
        compiler_params=_cparams(("arbitrary", "arbitrary")),
        name="norm_project",
    )(x, mod, gain.reshape(1, d), *weights)


def _mlp_kernel(*refs, n_mix):
    x_ref, mod_ref, g_ref = refs[:3]
    o_refs = refs[3:3 + n_mix]
    wo_refs = refs[3 + n_mix:3 + 2 * n_mix]
    w1_ref, w2_ref, out_ref, x1_s, h_s, acc_s = refs[3 + 2 * n_mix:]
    f = pl.program_id(2)

    @pl.when(f == 0)
    def _():
        mod = mod_ref[0]
        mp = _dot(o_refs[0][0].astype(BF16), wo_refs[0][...])
        for o_ref, wo_ref in zip(o_refs[1:], wo_refs[1:]):
            mp += _dot(o_ref[0].astype(BF16), wo_ref[...])
        x1 = x_ref[0] + mod[2:3] * mp
        x1_s[...] = x1
        h = _rms_rows(x1, g_ref[...]) * (1.0 + mod[4:5]) + mod[3:4]
        h_s[...] = h.astype(BF16)
        acc_s[...] = jnp.zeros_like(acc_s)

    a = _dot(h_s[...], w1_ref[...])
    a = jnp.square(jnp.maximum(a, 0.0))
    acc_s[...] += _dot(a.astype(BF16), w2_ref[...])

    @pl.when(f == pl.num_programs(2) - 1)
    def _():
        out_ref[0] = x1_s[...] + mod_ref[0][5:6] * acc_s[...]


def _mix_mlp(x, mod, gain2, mixes, w_outs, w1, w2):
    g, t, d = x.shape
    dff = w1.shape[1]
    tm, tf = min(MLP_ROWS, t), min(MLP_FF_TILE, dff)
    n_mix = len(mixes)
    in_specs = [pl.BlockSpec((1, tm, d), lambda i, j, f: (i, j, 0)),
                pl.BlockSpec((1, N_MOD, d), lambda i, j, f: (i, 0, 0)),
                pl.BlockSpec((1, d), lambda i, j, f: (0, 0))]
    in_specs += [pl.BlockSpec((1, tm, o.shape[2]), lambda i, j, f: (i, j, 0)) for o in mixes]
    in_specs += [pl.BlockSpec(w.shape, lambda i, j, f: (0, 0)) for w in w_outs]
    in_specs += [pl.BlockSpec((d, tf), lambda i, j, f: (0, f)),
                 pl.BlockSpec((tf, d), lambda i, j, f: (f, 0))]
    return pl.pallas_call(
        functools.partial(_mlp_kernel, n_mix=n_mix),
        grid=(g, t // tm, dff // tf),
        in_specs=in_specs,
        out_specs=pl.BlockSpec((1, tm, d), lambda i, j, f: (i, j, 0)),
        out_shape=jax.ShapeDtypeStruct((g, t, d), F32),
        scratch_shapes=[pltpu.VMEM((tm, d), F32), pltpu.VMEM((tm, d), BF16), pltpu.VMEM((tm, d), F32)],
        compiler_params=_cparams(("arbitrary", "arbitrary", "arbitrary")),
        name="mix_mlp",
    )(x, mod, gain2.reshape(1, d), *mixes, *w_outs, w1, w2)


def _pair_ones():
    r = lax.broadcasted_iota(jnp.int32, (LANES, LANES), 0) // HEAD_DIM
    c = lax.broadcasted_iota(jnp.int32, (LANES, LANES), 1) // HEAD_DIM
    return jnp.where(r == c, 1.0, 0.0).astype(BF16)


def _head_rms(xb, gain, pair_ones):
    hi, lo = _split2(xb * xb)
    ss = _dot(jnp.concatenate([hi, lo], axis=1), jnp.concatenate([pair_ones, pair_ones], axis=0))
    return xb * lax.rsqrt(ss * (1.0 / HEAD_DIM) + EPS) * gain


def _dup_half(x, g):
    lane = lax.broadcasted_iota(jnp.int32, x.shape, 1)
    sw = pltpu.roll(x, HEAD_DIM, 1)
    if g == 0:
        return jnp.where(lane < HEAD_DIM, x, sw)
    return jnp.where(lane < HEAD_DIM, sw, x)


def _softmax_parts(scores):
    m = jnp.max(scores[0], axis=-1, keepdims=True)
    for s in scores[1:]:
        m = jnp.maximum(m, jnp.max(s, axis=-1, keepdims=True))
    ps = [jnp.exp2(s - m) for s in scores]
    l = jnp.sum(ps[0], axis=-1, keepdims=True)
    for p in ps[1:]:
        l += jnp.sum(p, axis=-1, keepdims=True)
    return ps, 1.0 / l


def _attend_pairs(jobs):
    tasks = []
    for qb, ks, vs, biases in jobs:
        lane = lax.broadcasted_iota(jnp.int32, qb.shape, 1)
        for hh in range(2):
            sel = (lane < HEAD_DIM) if hh == 0 else (lane >= HEAD_DIM)
            tasks.append(dict(q=jnp.where(sel, qb, jnp.zeros_like(qb)), ks=ks, vs=vs,
                              bias=None if biases is None else biases[hh]))
    for tk in tasks:
        scores = [_dot_nt(tk["q"], k) for k in tk["ks"]]
        if tk["bias"] is not None:
            scores = [s if b is None else s + b for s, b in zip(scores, tk["bias"])]
        tk["s"] = scores
    for tk in tasks:
        tk["p"], tk["inv"] = _softmax_parts(tk["s"])
    for tk in tasks:
        o = _dot(tk["p"][0].astype(BF16), tk["vs"][0])
        for p, v in zip(tk["p"][1:], tk["vs"][1:]):
            o += _dot(p.astype(BF16), v)
        tk["o"] = o * tk["inv"]
    outs = []
    for j in range(len(jobs)):
        o0, o1 = tasks[2 * j]["o"], tasks[2 * j + 1]["o"]
        lane_o = lax.broadcasted_iota(jnp.int32, o0.shape, 1)
        outs.append(jnp.where(lane_o < HEAD_DIM, o0, o1))
    return outs


_QA0, _KA, _VA, _QB0, _KB0, _VB0 = 0, 4, 5, 6, 10, 14


def _ctx_attn_kernel(qkv_ref, gain_ref, o_ref, nk_ref, nv_ref):
    ones = _pair_ones()

    def blk(j):
        return qkv_ref[0, :, j * LANES:(j + 1) * LANES]

    def gain(j):
        return gain_ref[:, j * LANES:(j + 1) * LANES]

    ka = _head_rms(blk(_KA), gain(_KA), ones)
    va = blk(_VA)
    nk_ref[0, :, 0:LANES] = ka
    nv_ref[0, :, 0:LANES] = va
    k2 = [_dup_half(ka, g).astype(BF16) for g in range(N_KV_A)]
    v2 = [_dup_half(va, g).astype(BF16) for g in range(N_KV_A)]
    jobs = []
    for j in range(4):
        qn = _head_rms(blk(_QA0 + j), gain(_QA0 + j), ones).astype(BF16)
        jobs.append((qn, [k2[j // 2]], [v2[j // 2]], None))
    for j, o in enumerate(_attend_pairs(jobs)):
        o_ref[0, :, j * LANES:(j + 1) * LANES] = o.astype(o_ref.dtype)
    jobs = []
    for j in range(4):
        kb = _head_rms(blk(_KB0 + j), gain(_KB0 + j), ones)
        vb = blk(_VB0 + j)
        nk_ref[0, :, (1 + j) * LANES:(2 + j) * LANES] = kb
        nv_ref[0, :, (1 + j) * LANES:(2 + j) * LANES] = vb
        qn = _head_rms(blk(_QB0 + j), gain(_QB0 + j), ones).astype(BF16)
        jobs.append((qn, [kb.astype(BF16)], [vb.astype(BF16)], None))
    for j, o in enumerate(_attend_pairs(jobs)):
        o_ref[0, :, (4 + j) * LANES:(5 + j) * LANES] = o.astype(o_ref.dtype)


def _ctx_attention(qkv, gain_row):
    b, t, n = qkv.shape
    n_cache = (N_KV_A + N_HEADS_B) * HEAD_DIM
    n_o = (N_HEADS_A + N_HEADS_B) * HEAD_DIM
    return pl.pallas_call(
        _ctx_attn_kernel,
        grid=(b,),
        in_specs=[pl.BlockSpec((1, t, n), lambda i: (i, 0, 0)),
                  pl.BlockSpec((1, n), lambda i: (0, 0))],
        out_specs=[pl.BlockSpec((1, t, n_o), lambda i: (i, 0, 0)),
                   pl.BlockSpec((1, t, n_cache), lambda i: (i, 0, 0)),
                   pl.BlockSpec((1, t, n_cache), lambda i: (i, 0, 0))],
        out_shape=[jax.ShapeDtypeStruct((b, t, n_o), BF16),
                   jax.ShapeDtypeStruct((b, t, n_cache), F32),
                   jax.ShapeDtypeStruct((b, t, n_cache), F32)],
        compiler_params=_cparams(("arbitrary",)),
        name="ctx_attention",
    )(qkv, gain_row)


def _rope(x, c, sa, sb):
    return x * c + pltpu.roll(x, LANES - 16, 1) * sa + pltpu.roll(x, 16, 1) * sb


def _lat_a_kernel(q_ref, k_ref, v_ref, ck_ref, cv_ref, gq_ref, gk_ref,
                  cq_ref, saq_ref, sbq_ref, ck_t_ref, sak_t_ref, sbk_t_ref,
                  o_ref, k2_s, v2_s):
    t = k_ref.shape[1]
    ones = _pair_ones()

    @pl.when(pl.program_id(1) == 0)
    def _():
        kn = _head_rms(k_ref[0], gk_ref[...], ones)
        kr = _rope(kn, ck_t_ref[...], sak_t_ref[...], sbk_t_ref[...])
        v = v_ref[0]
        ck = ck_ref[0]
        cv = cv_ref[0]
        for g in range(N_KV_A):
            k2_s[g, 0:t, :] = _dup_half(kr, g).astype(BF16)
            k2_s[g, t:, :] = _dup_half(ck, g).astype(BF16)
            v2_s[g, 0:t, :] = _dup_half(v, g).astype(BF16)
            v2_s[g, t:, :] = _dup_half(cv, g).astype(BF16)

    c, sa, sb = cq_ref[...], saq_ref[...], sbq_ref[...]
    for j0 in range(0, 4, LAT_A_JOBS):
        jobs = []
        for j in range(j0, j0 + LAT_A_JOBS):
            qn = _head_rms(q_ref[0, :, j * LANES:(j + 1) * LANES], gq_ref[:, j * LANES:(j + 1) * LANES], ones)
            jobs.append((_rope(qn, c, sa, sb).astype(BF16), [k2_s[j // 2]], [v2_s[j // 2]], None))
        for j, o in zip(range(j0, j0 + LAT_A_JOBS), _attend_pairs(jobs)):
            o_ref[0, :, j * LANES:(j + 1) * LANES] = o.astype(o_ref.dtype)


def _rope_tables(t):
    pos = jnp.arange(t, dtype=jnp.int32)
    rc = jnp.stack([pos // GRID_W, pos % GRID_W], axis=-1).astype(F32)
    axis_dim = HEAD_DIM // 2
    inv_freq = 1.0 / (ROPE_THETA ** (jnp.arange(0, axis_dim, 2, dtype=F32) / axis_dim))
    ang = rc[:, :, None] * inv_freq
    cos, sin = jnp.cos(ang), jnp.sin(ang)
    zero = jnp.zeros_like(sin)
    c = jnp.stack([cos, cos], axis=2).reshape(t, HEAD_DIM)
    sa = jnp.stack([-sin, zero], axis=2).reshape(t, HEAD_DIM)
    sb = jnp.stack([zero, sin], axis=2).reshape(t, HEAD_DIM)
    two = lambda a: jnp.concatenate([a, a], axis=-1)
    return two(c), two(sa), two(sb)


def _latent_attention_a(qkv, cache_k, cache_v, gain_row):
    b, t, _ = qkv.shape
    p = cache_k.shape[1]
    tq = min(LAT_A_QUERY_TILE, t)
    c, sa, sb = _rope_tables(t)
    gq = gain_row[:, _QA0 * LANES:(_QA0 + 4) * LANES]
    gk = gain_row[:, _KA * LANES:(_KA + 1) * LANES]
    tab_q = pl.BlockSpec((tq, LANES), lambda i, j: (j, 0))
    tab_k = pl.BlockSpec((t, LANES), lambda i, j: (0, 0))
    return pl.pallas_call(
        _lat_a_kernel,
        grid=(b, t // tq),
        in_specs=[pl.BlockSpec((1, tq, 4 * LANES), lambda i, j: (i, j, 0)),
                  pl.BlockSpec((1, t, LANES), lambda i, j: (i, 0, _KA)),
                  pl.BlockSpec((1, t, LANES), lambda i, j: (i, 0, _VA)),
                  pl.BlockSpec((1, p, LANES), lambda i, j: (i, 0, 0)),
                  pl.BlockSpec((1, p, LANES), lambda i, j: (i, 0, 0)),
                  pl.BlockSpec((1, 4 * LANES), lambda i, j: (0, 0)),
                  pl.BlockSpec((1, LANES), lambda i, j: (0, 0)),
                  tab_q, tab_q, tab_q, tab_k, tab_k, tab_k],
        out_specs=pl.BlockSpec((1, tq, 4 * LANES), lambda i, j: (i, j, 0)),
        out_shape=jax.ShapeDtypeStruct((b, t, 4 * LANES), BF16),
        scratch_shapes=[pltpu.VMEM((N_KV_A, t + p, LANES), BF16), pltpu.VMEM((N_KV_A, t + p, LANES), BF16)],
        compiler_params=_cparams(("arbitrary", "arbitrary")),
        name="latent_attention_a",
    )(qkv, qkv, qkv, cache_k, cache_v, gq, gk, c, sa, sb, c, sa, sb)


def _nbr_bias_table(rel_bias, rows):
    wr = min(WIN_R, rows)
    cols = jnp.arange(GRID_W, dtype=jnp.int32)
    col_start = jnp.clip(cols - WIN_C // 2, 0, GRID_W - WIN_C)
    kc = jnp.arange(GRID_W, dtype=jnp.int32)
    in_win = (kc[None, :] >= col_start[:, None]) & (kc[None, :] < col_start[:, None] + WIN_C)
    nh, n_ro = rel_bias.shape[0], 2 * wr - 1
    rb = rel_bias[:, WIN_R - wr:WIN_R + wr - 1, :].astype(F32) * LOG2_E
    span = 2 * GRID_W
    lead = GRID_W - WIN_C
    v = jnp.pad(rb, ((0, 0), (0, 0), (lead, span - lead - (2 * WIN_C - 1))))
    flat = jnp.tile(v, (1, 1, GRID_W))
    toep = flat[..., GRID_W - 1:GRID_W - 1 + GRID_W * (span - 1)].reshape(nh, n_ro, GRID_W, span - 1)[..., :GRID_W]
    base = jnp.where(in_win[None, None], toep, NEG_BIG)
    return jnp.concatenate([base[:, :-1], base[:, 1:]], axis=-1)


def _nbr_kernel(q_ref, k_ref, v_ref, ck_ref, cv_ref, gq_ref, gk_ref, bias_ref, o_ref, qn_s, kn_s, vn_s):
    t = q_ref.shape[1]
    rows = t // GRID_W
    wr = min(WIN_R, rows)
    ones = _pair_ones()
    qn_s[...] = _head_rms(q_ref[0], gq_ref[0], ones).astype(BF16)
    kn_s[...] = _head_rms(k_ref[0], gk_ref[0], ones).astype(BF16)
    vn_s[...] = v_ref[0].astype(BF16)
    ck = ck_ref[0].astype(BF16)
    cv = cv_ref[0].astype(BF16)
    rb = math.gcd(rows, NBR_ROW_GROUP)

    def row_group(gi, carry):
        jobs, q0s = [], []
        for rr in range(rb):
            r = gi * rb + rr
            rs = jnp.clip(r - wr // 2, 0, rows - wr)
            var = r - rs
            q0 = pl.multiple_of(r * GRID_W, GRID_W)
            k0 = pl.multiple_of(rs * GRID_W, GRID_W)
            biases = [[jnp.concatenate([bias_ref[hh, wr - 1 - var + 2 * m] for m in range(wr // 2)], axis=-1), None]
                      for hh in range(2)]
            jobs.append((qn_s[pl.ds(q0, GRID_W), :], [kn_s[pl.ds(k0, wr * GRID_W), :], ck],
                         [vn_s[pl.ds(k0, wr * GRID_W), :], cv], biases))
            q0s.append(q0)
        for q0, o in zip(q0s, _attend_pairs(jobs)):
            o_ref[0, pl.ds(q0, GRID_W), :] = o.astype(o_ref.dtype)
        return carry

    lax.fori_loop(0, rows // rb, row_group, 0)


def _latent_attention_b(qkv, cache_k, cache_v, gain_row, rel_bias):
    b, t, _ = qkv.shape
    p = cache_k.shape[1]
    rows = t // GRID_W
    wr = min(WIN_R, rows)
    assert wr % 2 == 0
    n_tab = 2 * wr - 2
    table = _nbr_bias_table(rel_bias, rows).reshape(N_HEADS_B // 2, 2, n_tab, GRID_W, 2 * GRID_W)
    gq = gain_row[:, _QB0 * LANES:(_QB0 + 4) * LANES].reshape(4, 1, LANES)
    gk = gain_row[:, _KB0 * LANES:(_KB0 + 4) * LANES].reshape(4, 1, LANES)
    return pl.pallas_call(
        _nbr_kernel,
        grid=(b, N_HEADS_B // 2),
        in_specs=[pl.BlockSpec((1, t, LANES), lambda i, j: (i, 0, _QB0 + j)),
                  pl.BlockSpec((1, t, LANES), lambda i, j: (i, 0, _KB0 + j)),
                  pl.BlockSpec((1, t, LANES), lambda i, j: (i, 0, _VB0 + j)),
                  pl.BlockSpec((1, p, LANES), lambda i, j: (i, 0, 1 + j)),
                  pl.BlockSpec((1, p, LANES), lambda i, j: (i, 0, 1 + j)),
                  pl.BlockSpec((1, 1, LANES), lambda i, j: (j, 0, 0)),
                  pl.BlockSpec((1, 1, LANES), lambda i, j: (j, 0, 0)),
                  pl.BlockSpec((None, 2, n_tab, GRID_W, 2 * GRID_W), lambda i, j: (j, 0, 0, 0, 0))],
        out_specs=pl.BlockSpec((1, t, LANES), lambda i, j: (i, 0, j)),
        out_shape=jax.ShapeDtypeStruct((b, t, 4 * LANES), BF16),
        scratch_shapes=[pltpu.VMEM((t, LANES), BF16)] * 3,
        compiler_params=_cparams(("arbitrary", "arbitrary")),
        name="latent_attention_b",
    )(qkv, qkv, qkv, cache_k, cache_v, gq, gk, table)


def _delta_kernel(*refs, hb, hg, cg, has_s0, emit_state):
    (q_ref, k_ref, v_ref, cwq_ref, cwk_ref, cwv_ref, ab_ref, alog_ref, dtb_ref, z_ref, onorm_ref), rest = (
        refs[:11], refs[11:])
    s0_ref = None
    if has_s0:
        s0_ref, rest = rest[0], rest[1:]
    y_ref, rest = rest[0], rest[1:]
    sfin_ref = None
    if emit_state:
        sfin_ref, rest = rest[0], rest[1:]
    q_s, k_s, v_s, gates_s, u_s, wq_s, kt_s, att_s, el_s, o_s, s_s = rest
    pad_s = o_s.at[0]
    t = q_ref.shape[1]
    c = DELTA_CHUNK
    n = t // c
    h0 = pl.program_id(1) * hb
    rbs = min(t, DELTA_ROW_BLOCK)
    nrb = t // rbs
    unroll = min(4, nrb)

    pad_s[0:8, :] = jnp.zeros((8, LANES), F32)
    pad_s[t + 8:t + 16, :] = jnp.zeros((8, LANES), F32)

    def conv_silu_norm(x_ref, cw_ref, hd, dst, scale):
        for i in range(nrb):
            pad_s[8 + i * rbs:8 + (i + 1) * rbs, :] = x_ref[0, i * rbs:(i + 1) * rbs, hd * LANES:(hd + 1) * LANES]
        cw = cw_ref[:, hd * LANES:(hd + 1) * LANES]

        def blk(i, carry):
            r = pl.multiple_of(i * rbs, rbs)
            y = (pad_s[pl.ds(r + 7, rbs), :] * cw[0:1] + pad_s[pl.ds(r + 8, rbs), :] * cw[1:2]
                 + pad_s[pl.ds(r + 9, rbs), :] * cw[2:3])
            y = y * jax.nn.sigmoid(y)
            if scale is not None:
                y = y * lax.rsqrt(jnp.sum(y * y, axis=-1, keepdims=True) + EPS) * scale
            dst[hd, pl.ds(r, rbs), :] = y
            return carry

        lax.fori_loop(0, nrb, blk, 0, unroll=unroll)

    for hd in range(hb):
        conv_silu_norm(q_ref, cwq_ref, hd, q_s, DK_C ** -0.5)
        conv_silu_norm(k_ref, cwk_ref, hd, k_s, 1.0)
        conv_silu_norm(v_ref, cwv_ref, hd, v_s, None)

    def gate_blk(i, carry):
        r = pl.multiple_of(i * rbs, rbs)
        ab = ab_ref[0, pl.ds(r, rbs), :]
        xa = ab + dtb_ref[...]
        softplus = jnp.maximum(xa, 0.0) + jnp.log1p(jnp.exp(-jnp.abs(xa)))
        lane_t = lax.broadcasted_iota(jnp.int32, ab.shape, 1)
        gates_s[pl.ds(r, rbs), :] = jnp.where(lane_t < 2 * N_HEADS_C, -jnp.exp(alog_ref[...]) * softplus,
                                              jax.nn.sigmoid(ab))
        return carry

    lax.fori_loop(0, nrb, gate_blk, 0, unroll=unroll)

    ii = lax.broadcasted_iota(jnp.int32, (c, c), 0)
    jj = lax.broadcasted_iota(jnp.int32, (c, c), 1)
    eye = jnp.where(ii == jj, 1.0, 0.0)
    ones_cc = jnp.ones((c, c), BF16)
    bd8 = (ii // 8) == (jj // 8)
    offs = []
    blk = 8
    while blk < c:
        offs.append((blk, ((ii // (2 * blk)) == (jj // (2 * blk))) & ((ii // blk) != (jj // blk))))
        blk *= 2
    dir_masks = []
    for d in range(2):
        if d == 0:
            incl, strict, incl_t = ii >= jj, ii > jj, ii <= jj
        else:
            incl, strict, incl_t = ii <= jj, ii < jj, ii >= jj
        dir_masks.append((incl, strict, incl_t, jnp.where(incl, 1.0, 0.0).astype(BF16)))
    pick_r = lax.broadcasted_iota(jnp.int32, (LANES, 4 * LANES), 0)
    pick_c = lax.broadcasted_iota(jnp.int32, (LANES, 4 * LANES), 1)
    n_cgroups = n // cg

    def prep_unit(gi, carry):
        hg0 = (gi // n_cgroups) * hg
        c0 = (gi % n_cgroups) * cg
        chains = []
        for hh in range(hg):
            hd = hg0 + hh
            pick = jnp.where(pick_r == (pick_c // LANES) * N_HEADS_C + h0 + hd, 1.0, 0.0).astype(BF16)
            for cc in range(cg):
                ci = c0 + cc
                r0 = pl.multiple_of(ci * c, c)
                qb, kb = q_s[hd, pl.ds(r0, c), :].astype(BF16), k_s[hd, pl.ds(r0, c), :].astype(BF16)
                kk, qk = _dot_nt(kb, kb), _dot_nt(qb, kb)
                picked = _dot2_rhs01(gates_s[pl.ds(r0, c), :], pick)
                for d in range(2):
                    chains.append(dict(hd=hd, ci=ci, r0=r0, d=d, kk=kk, qk=qk,
                                       la=picked[:, d * LANES:(d + 1) * LANES],
                                       be=picked[:, (2 + d) * LANES:(3 + d) * LANES]))
        for ch in chains:
            incl, strict, incl_t, cum_mask = dir_masks[ch["d"]]
            ch["g_col"] = _dot2_lhs01(cum_mask, ch["la"])
            ch["g_row"] = _dot2_lhs01(ones_cc, jnp.where(incl_t, ch["la"], 0.0))
        for ch in chains:
            incl, strict, incl_t, cum_mask = dir_masks[ch["d"]]
            ch["decay"] = jnp.where(incl, jnp.exp(jnp.where(incl, ch["g_col"] - ch["g_row"], 0.0)), 0.0)
            ch["a"] = jnp.where(strict, (ch["kk"] * ch["decay"]) * ch["be"], 0.0)
            ch["p8"] = jnp.where(bd8, -ch["a"], 0.0)
        for ch, pk2 in zip(chains, _pair_dots([ch["p8"] for ch in chains], [ch["p8"] for ch in chains])):
            ch["pk2"] = pk2
        for ch in chains:
            ch["acc"] = eye + ch["p8"]
            ch["r"] = _dot(ch["pk2"].astype(BF16), jnp.concatenate([ch["pk2"], ch["acc"]], axis=1).astype(BF16))
        for ch in chains:
            ch["acc"] = ch["acc"] + ch["r"][:, c:]
        for ch, x in zip(chains, _pair_dots([ch["r"][:, :c] for ch in chains], [ch["acc"] for ch in chains])):
            ch["t"] = ch["acc"] + x
        for blk, off in offs:
            a_sel = [_row_blocks(jnp.where(off, ch["a"], 0.0), blk, ch["d"] == 1) for ch in chains]
            for ch, y in zip(chains, _pair_dots(a_sel, [ch["t"] for ch in chains])):
                ch["y"] = _row_blocks_put(y, jnp.zeros((c, c), F32), blk, ch["d"] == 1)
            t_sel = [_row_blocks(ch["t"], blk, ch["d"] == 1) for ch in chains]
            for ch, ts, x in zip(chains, t_sel, _pair_dots(t_sel, [ch["y"] for ch in chains])):
                ch["t"] = _row_blocks_put(ts - x, ch["t"], blk, ch["d"] == 1)
        for ch in chains:
            ch["eg"] = jnp.exp(ch["g_col"])
            ch["kc"] = k_s[ch["hd"], pl.ds(ch["r0"], c), :]
            vc = v_s[ch["hd"], pl.ds(ch["r0"], c), :]
            rhs = jnp.concatenate([vc * ch["be"], ch["kc"] * ch["be"] * ch["eg"]], axis=1)
            ch["uw"] = _dot2_lhs_bf16(ch["t"], rhs)
        for ch in chains:
            d, r0, ci, g_col = ch["d"], ch["r0"], ch["ci"], ch["g_col"]
            hx = ch["hd"] * 2 + d
            g_last = g_col[c - 1:c, :] if d == 0 else g_col[0:1, :]
            ktb = (ch["kc"] * jnp.exp(g_last - g_col)).astype(BF16)
            uw_hi, uw_lo = _split2(ch["uw"])
            po = _dot_tn(jnp.concatenate([ktb, ktb], axis=0), jnp.concatenate([uw_hi, uw_lo], axis=0))
            awu = _dot((ch["qk"] * ch["decay"]).astype(BF16), uw_hi)
            om_hi, om_lo = _split2(po[:, c:])
            u_s[hx, pl.ds(r0, c), :] = po[:, :c]
            r2 = pl.multiple_of(ci * 2 * c, 2 * c)
            wq_s[hx, pl.ds(r2, c), :] = (q_s[ch["hd"], pl.ds(r0, c), :] * ch["eg"] - awu[:, c:]).astype(BF16)
            wq_s[hx, pl.ds(r2 + c, c), :] = om_hi
            kt_s[hx, pl.ds(r0, c), :] = om_lo
            o_s[ch["hd"], pl.ds(r0, c), :] += awu[:, :c]
            r8 = pl.multiple_of(ci * 8, 8)
            el_s[hx, pl.ds(r8, 8), :] = jnp.broadcast_to(jnp.exp(g_last), (8, LANES))
        return carry

    def scan_step(i, carry):
        items = []
        for hx in range(2 * hb):
            ci = i if hx % 2 == 0 else n - 1 - i
            items.append(dict(hx=hx, s=s_s[hx], r0=pl.multiple_of(ci * c, c), r2=pl.multiple_of(ci * 2 * c, 2 * c),
                              r8=pl.multiple_of(ci * 8, 8)))
        for it in items:
            sb = it["s"].astype(BF16)
            it["ws"] = _dot(wq_s[it["hx"], pl.ds(it["r2"], 2 * c), :], sb)
            it["ls"] = _dot(kt_s[it["hx"], pl.ds(it["r0"], c), :], sb)
        for it in items:
            hx = it["hx"]
            o_s[hx // 2, pl.ds(it["r0"], c), :] += it["ws"][:c]
            s_s[hx] = (it["s"] * el_s[hx, pl.ds(it["r8"], 1), :]
                       + (u_s[hx, pl.ds(it["r0"], c), :] - it["ws"][c:] - it["ls"]))
        return carry

    for hx in range(2 * hb):
        if has_s0:
            s_s[hx] = s0_ref[0, hx % 2, hx // 2]
        else:
            s_s[hx] = jnp.zeros((DK_C, DV_C), F32)

    def zero_blk(i, carry):
        for hd in range(hb):
            o_s[hd, pl.ds(pl.multiple_of(i * rbs, rbs), rbs), :] = jnp.zeros((rbs, LANES), F32)
        return carry

    lax.fori_loop(0, nrb, zero_blk, 0)

    lax.fori_loop(0, (hb // hg) * n_cgroups, prep_unit, 0)
    lax.fori_loop(0, n, scan_step, 0)

    if emit_state:
        for hx in range(2 * hb):
            sfin_ref[0, hx % 2, hx // 2] = s_s[hx]

    for hd in range(hb):
        def out_blk(i, carry, hd=hd):
            r = pl.multiple_of(i * rbs, rbs)
            o = o_s[hd, pl.ds(r, rbs), :]
            z = z_ref[0, pl.ds(r, rbs), hd * LANES:(hd + 1) * LANES]
            y = _rms_rows(o, onorm_ref[...]) * (z * jax.nn.sigmoid(z))
            y_ref[0, pl.ds(r, rbs), hd * LANES:(hd + 1) * LANES] = y.astype(y_ref.dtype)
            return carry

        lax.fori_loop(0, nrb, out_blk, 0, unroll=unroll)


def _delta_mixer(qkv, z, ab, conv_w, alog_row, dtb_row, out_norm, s0, hb, chains, emit_state):
    b, t, _ = qkv.shape
    nh = N_HEADS_C
    c = DELTA_CHUNK
    n = t // c
    hg = min(hb, chains // 2)
    cg = max(k for k in range(1, n + 1) if n % k == 0 and 2 * hg * k <= max(chains, 2 * hg))
    assert hb % hg == 0 and n % cg == 0 and nh % hb == 0
    w = hb * LANES
    has_s0 = s0 is not None
    col = lambda off: pl.BlockSpec((1, t, w), lambda i, j, off=off: (i, 0, off + j))
    cw = lambda off: pl.BlockSpec((3, w), lambda i, j, off=off: (0, off + j))
    row = pl.BlockSpec((1, LANES), lambda i, j: (0, 0))
    st = pl.BlockSpec((1, 2, hb, DK_C, DV_C), lambda i, j: (i, 0, j, 0, 0))
    nb = nh // hb
    in_specs = [col(0), col(nb), col(2 * nb), cw(0), cw(nb), cw(2 * nb),
                pl.BlockSpec((1, t, LANES), lambda i, j: (i, 0, 0)), row, row, col(0), row]
    args = [qkv, qkv, qkv, conv_w, conv_w, conv_w, ab, alog_row, dtb_row, z, out_norm.reshape(1, DV_C)]
    if has_s0:
        in_specs.append(st)
        args.append(s0)
    out_specs = [col(0)]
    out_shape = [jax.ShapeDtypeStruct((b, t, nh * DV_C), BF16)]
    if emit_state:
        out_specs.append(st)
        out_shape.append(jax.ShapeDtypeStruct((b, 2, nh, DK_C, DV_C), F32))
    hd2 = 2 * hb
    scratch = [pltpu.VMEM((hb, t, LANES), F32), pltpu.VMEM((hb, t, LANES), F32), pltpu.VMEM((hb, t, LANES), F32),
               pltpu.VMEM((t, LANES), F32),
               pltpu.VMEM((hd2, t, LANES), F32), pltpu.VMEM((hd2, 2 * t, LANES), BF16),
               pltpu.VMEM((hd2, t, LANES), BF16), pltpu.VMEM((hd2, t, LANES), BF16),
               pltpu.VMEM((hd2, 8 * n, LANES), F32), pltpu.VMEM((hb, t + 16, LANES), F32),
               pltpu.VMEM((hd2, DK_C, DV_C), F32)]
    return pl.pallas_call(
        functools.partial(_delta_kernel, hb=hb, hg=hg, cg=cg, has_s0=has_s0, emit_state=emit_state),
        grid=(b, nb),
        in_specs=in_specs, out_specs=out_specs, out_shape=out_shape, scratch_shapes=scratch,
        compiler_params=_cparams(("arbitrary", "arbitrary"), DELTA_VMEM_LIMIT),
        name="delta_mixer",
    )(*args)


def _pad_lanes(x, n):
    return jnp.pad(x, ((0, 0), (0, n - x.shape[1])))


def kernel(x_prompt, x_sample, c, cache_l0_k, cache_l0_v, state_l1, c_ctx, l0_mod_w, l0_mod_b, l0_norm1, l0_w_in, l0_q_norm_a, l0_k_norm_a, l0_q_norm_b, l0_k_norm_b, l0_rel_bias, l0_w_out, l0_norm2, l0_mlp_w1, l0_mlp_w2, l1_mod_w, l1_mod_b, l1_norm1, l1_w_in, l1_conv_w, l1_a_log, l1_dt_bias, l1_out_norm, l1_w_out, l1_norm2, l1_mlp_w1, l1_mlp_w2):
    bp, tp, d = x_prompt.shape
    bs, ts, _ = x_sample.shape
    n_cache = (N_KV_A + N_HEADS_B) * HEAD_DIM
    bf = lambda w: w.astype(BF16)

    n_rows = -(-(1 + bs) // 8) * 8
    cvec = jnp.concatenate([c_ctx[None, :], c, jnp.zeros((n_rows - 1 - bs, d), F32)], axis=0)
    mods = []
    for mw, mb in ((l0_mod_w, l0_mod_b), (l1_mod_w, l1_mod_b)):
        m = _modulation(cvec, mw, mb).reshape(n_rows, N_MOD, d)
        mods.append((m[0:1], m[1:1 + bs]))

    xp = x_prompt.reshape(1, bp * tp, d)
    xs = x_sample

    scale = HEAD_DIM ** -0.5 * LOG2_E
    ones_a = jnp.ones((N_KV_A * HEAD_DIM,), F32)
    ones_b = jnp.ones((N_HEADS_B * HEAD_DIM,), F32)
    gain_row = jnp.concatenate([jnp.tile(l0_q_norm_a, N_HEADS_A) * scale, jnp.tile(l0_k_norm_a, N_KV_A), ones_a,
                                jnp.tile(l0_q_norm_b, N_HEADS_B) * scale, jnp.tile(l0_k_norm_b, N_HEADS_B),
                                ones_b])[None, :]
    w_in0 = bf(l0_w_in)
    w1_0, w2_0 = bf(l0_mlp_w1), bf(l0_mlp_w2)
    half = N_HEADS_A * HEAD_DIM

    mod_p, mod_s = mods[0]
    (qkv_p,) = _project(xp, mod_p, l0_norm1, [w_in0], PROJ_ROWS_L0)
    o_p, new_k, new_v = _ctx_attention(qkv_p.reshape(bp, tp, -1), gain_row)
    xp = _mix_mlp(xp, mod_p, l0_norm2, [o_p.reshape(1, bp * tp, -1)], [bf(l0_w_out)], w1_0, w2_0)

    (qkv_s,) = _project(xs, mod_s, l0_norm1, [w_in0], PROJ_ROWS_L0)
    ck = cache_l0_k.reshape(bs, -1, n_cache)
    cv = cache_l0_v.reshape(bs, -1, n_cache)
    o_a = _latent_attention_a(qkv_s, ck, cv, gain_row)
    o_b = _latent_attention_b(qkv_s, ck, cv, gain_row, l0_rel_bias)
    xs = _mix_mlp(xs, mod_s, l0_norm2, [o_a, o_b], [bf(l0_w_out[:half]), bf(l0_w_out[half:])], w1_0, w2_0)

    n_qkv = N_HEADS_C * (2 * DK_C + DV_C)
    n_z = N_HEADS_C * DV_C
    w_pieces = [bf(l1_w_in[:, :n_qkv]), bf(l1_w_in[:, n_qkv:n_qkv + n_z]),
                bf(_pad_lanes(l1_w_in[:, n_qkv + n_z:], LANES))]
    w_out1 = bf(l1_w_out)
    w1_1, w2_1 = bf(l1_mlp_w1), bf(l1_mlp_w2)
    alog_row = _pad_lanes(l1_a_log.reshape(1, -1), LANES)
    dtb_row = _pad_lanes(l1_dt_bias.reshape(1, -1), LANES)

    mod_p, mod_s = mods[1]
    qkv1_p, z_p, ab_p = _project(xp, mod_p, l1_norm1, w_pieces, PROJ_ROWS_L1)
    y_p, new_s = _delta_mixer(qkv1_p.reshape(bp, tp, -1), z_p.reshape(bp, tp, -1), ab_p.reshape(bp, tp, -1),
                              l1_conv_w, alog_row, dtb_row, l1_out_norm, None, hb=DELTA_HEADS_PROMPT,
                              chains=DELTA_PREP_CHAINS_PROMPT, emit_state=True)
    xp = _mix_mlp(xp, mod_p, l1_norm2, [y_p.reshape(1, bp * tp, -1)], [w_out1], w1_1, w2_1)

    qkv1_s, z_s, ab_s = _project(xs, mod_s, l1_norm1, w_pieces, PROJ_ROWS_L1)
    (y_s,) = _delta_mixer(qkv1_s, z_s, ab_s, l1_conv_w, alog_row, dtb_row, l1_out_norm, state_l1.astype(F32),
                          hb=DELTA_HEADS_SAMPLE, chains=DELTA_PREP_CHAINS_SAMPLE, emit_state=False)
    xs = _mix_mlp(xs, mod_s, l1_norm2, [y_s], [w_out1], w1_1, w2_1)

    return (xp.reshape(bp, tp, d), xs,
            new_k.reshape(bp, tp, N_KV_A + N_HEADS_B, HEAD_DIM), new_v.reshape(bp, tp, N_KV_A + N_HEADS_B, HEAD_DIM),
            new_s.astype(x_prompt.dtype))
```
